```python
import math
import jax, jax.numpy as jnp
from jax import lax
import numpy as np


D_MODEL = 1024
BATCH = 16
SEQ = 2048
DEPTH = 4

HEAD_DIM = 64
BLOCK = 128
DILATED_PAIRS = ((128, 1), (512, 4), (2048, 16))
N_DIL = 3
A_HEADS = 4
A_WIDTH = A_HEADS * HEAD_DIM
A_QKV = 3 * N_DIL * A_HEADS * HEAD_DIM
S5_WIDTH = 256
S5_GROUP_CH = 16
S5_GROUPS = S5_WIDTH // S5_GROUP_CH
S5_STATE = 64
S5_DT_MIN = 1e-3
S5_DT_MAX = 1e-1
C_Q_HEADS = 8
C_KV_HEADS = 2
C_WIDTH = C_Q_HEADS * HEAD_DIM
C_WINDOW = 128
ROPE_THETA = 150000.0
N_BRANCHES = 3
IN_SPLITS = (A_QKV, S5_WIDTH, C_Q_HEADS * HEAD_DIM, C_KV_HEADS * HEAD_DIM, C_KV_HEADS * HEAD_DIM, N_BRANCHES * D_MODEL)
IN_COLS = 6400
D_FF = 3584
N_EXPERTS = 8
TOP_K = 2
N_DENSE = (DEPTH + 1) // 2
N_MOE = DEPTH // 2
RMS_EPS = 1e-6

kernel_name = 'hybrid_gated_dilated_s5_swa_moe'

F32 = jnp.float32


def rmsnorm(x, g):
    xf = x.astype(F32)
    y = xf * lax.rsqrt(jnp.mean(xf * xf, axis=-1, keepdims=True) + RMS_EPS)
    return (y * g.astype(F32)).astype(x.dtype)


def banded_attention(q, k, v, window, sinks=None):
    Bt, L, H, Dh = q.shape
    Hk = k.shape[2]
    rep = H // Hk
    Lp = -(-L // BLOCK) * BLOCK
    nb = Lp // BLOCK
    pad = ((0, 0), (0, Lp - L), (0, 0), (0, 0))
    qb = jnp.pad(q, pad).reshape(Bt, nb, BLOCK, Hk, rep, Dh)
    kb = jnp.pad(k, pad).reshape(Bt, nb, BLOCK, Hk, Dh)
    vb = jnp.pad(v, pad).reshape(Bt, nb, BLOCK, Hk, Dh)

    def with_prev(t):
        prev = jnp.concatenate([jnp.zeros_like(t[:, :1]), t[:, :-1]], axis=1)
        return jnp.concatenate([prev, t], axis=2)

    kw, vw = with_prev(kb), with_prev(vb)
    s = jnp.einsum('bnqgrd,bnkgd->bngrqk', qb, kw, preferred_element_type=F32) * (Dh ** -0.5)
    qi = jnp.arange(BLOCK)[:, None]
    ki = jnp.arange(2 * BLOCK)[None, :]
    dist = BLOCK + qi - ki
    blk = jnp.arange(nb)[:, None, None]
    valid = (dist >= 0) & (dist <= window) & ((blk > 0) | (ki >= BLOCK))
    s = jnp.where(valid[None, :, None, None], s, -jnp.inf)
    m = jnp.max(s, axis=-1)
    if sinks is not None:
        sk = sinks.astype(F32).reshape(1, 1, Hk, rep, 1)
        m = jnp.maximum(m, sk)
    p = jnp.exp(s - m[..., None])
    denom = jnp.sum(p, axis=-1)
    if sinks is not None:
        denom = denom + jnp.exp(sk - m)
    p = p / denom[..., None]
    o = jnp.einsum('bngrqk,bnkgd->bnqgrd', p, vw.astype(F32))
    o = o.reshape(Bt, Lp, H, Dh)[:, :L]
    lse = (m + jnp.log(denom)).transpose(0, 1, 4, 2, 3).reshape(Bt, Lp, H)[:, :L]
    return o, lse


def strided_window_attention(q, k, v, window, dilation):
    Bt, S, H, Dh = q.shape
    L = S // dilation

    def to_phase(t):
        return t.reshape(Bt, L, dilation, H, Dh).transpose(0, 2, 1, 3, 4).reshape(Bt * dilation, L, H, Dh)

    o, lse = banded_attention(to_phase(q), to_phase(k), to_phase(v), window // dilation)
    o = o.reshape(Bt, dilation, L, H, Dh).transpose(0, 2, 1, 3, 4).reshape(Bt, S, H, Dh)
    lse = lse.reshape(Bt, dilation, L, H).transpose(0, 2, 1, 3).reshape(Bt, S, H)
    return o, lse


def dilated_mixer(pa):
    Bt, S, _ = pa.shape
    qkv = pa.reshape(Bt, S, 3, N_DIL, A_HEADS, HEAD_DIM)
    outs, lses = [], []
    for g, (w, d) in enumerate(DILATED_PAIRS):
        o, lse = strided_window_attention(qkv[:, :, 0, g], qkv[:, :, 1, g], qkv[:, :, 2, g], w, d)
        outs.append(o)
        lses.append(lse)
    wts = jax.nn.softmax(jnp.stack(lses, axis=0), axis=0)
    o = jnp.sum(wts[..., None] * jnp.stack(outs, axis=0), axis=0)
    return o.reshape(Bt, S, A_WIDTH).astype(pa.dtype)


def _complex_affine_combine(e1, e2):
    a1r, a1i, b1r, b1i = e1
    a2r, a2i, b2r, b2i = e2
    return (a2r * a1r - a2i * a1i,
            a2r * a1i + a2i * a1r,
            a2r * b1r - a2i * b1i + b2r,
            a2r * b1i + a2i * b1r + b2i)


def s5_mixer(u, lam_re, lam_im, log_dt, b_re, b_im, c_re, c_im, d_skip, w_glu, b_glu):
    Bt, S, W = u.shape
    uf = u.astype(F32)
    ug = uf.reshape(Bt, S, S5_GROUPS, S5_GROUP_CH)
    lr, li = lam_re.astype(F32), lam_im.astype(F32)
    dt = jnp.exp(log_dt.astype(F32))[:, None]
    mag = jnp.exp(lr * dt)
    a_re, a_im = mag * jnp.cos(li * dt), mag * jnp.sin(li * dt)
    nr, ni = a_re - 1.0, a_im
    den = lr * lr + li * li
    z_re = (nr * lr + ni * li) / den
    z_im = (ni * lr - nr * li) / den
    bu_re = jnp.einsum('bsgp,gnp->bsgn', ug, b_re.astype(F32))
    bu_im = jnp.einsum('bsgp,gnp->bsgn', ug, b_im.astype(F32))
    x_re = z_re * bu_re - z_im * bu_im
    x_im = z_re * bu_im + z_im * bu_re
    ar = jnp.broadcast_to(a_re, x_re.shape)
    ai = jnp.broadcast_to(a_im, x_re.shape)
    _, _, h_re, h_im = lax.associative_scan(_complex_affine_combine, (ar, ai, x_re, x_im), axis=1)
    y = (jnp.einsum('bsgn,gpn->bsgp', h_re, c_re.astype(F32))
         - jnp.einsum('bsgn,gpn->bsgp', h_im, c_im.astype(F32)))
    y = y.reshape(Bt, S, W) + d_skip.astype(F32) * uf
    g = jax.nn.gelu(y)
    out = g * jax.nn.sigmoid(g @ w_glu.astype(F32) + b_glu.astype(F32))
    return out.astype(u.dtype)


def rope(x, positions):
    inv = ROPE_THETA ** (-jnp.arange(0, HEAD_DIM, 2, dtype=F32) / HEAD_DIM)
    ang = positions[:, None] * inv[None, :]
    cos, sin = jnp.cos(ang)[None, :, None, :], jnp.sin(ang)[None, :, None, :]
    xf = x.astype(F32)
    x1, x2 = xf[..., :HEAD_DIM // 2], xf[..., HEAD_DIM // 2:]
    return jnp.concatenate([x1 * cos - x2 * sin, x2 * cos + x1 * sin], axis=-1).astype(x.dtype)


def swa_sink_mixer(pq, pk, pv, sinks, positions):
    Bt, S, _ = pq.shape
    q = rope(pq.reshape(Bt, S, C_Q_HEADS, HEAD_DIM), positions)
    k = rope(pk.reshape(Bt, S, C_KV_HEADS, HEAD_DIM), positions)
    v = pv.reshape(Bt, S, C_KV_HEADS, HEAD_DIM)
    o, _ = banded_attention(q, k, v, C_WINDOW, sinks)
    return o.reshape(Bt, S, C_WIDTH).astype(pq.dtype)


def swiglu(h, wg, wu, wd):
    return (jax.nn.silu(h @ wg) * (h @ wu)) @ wd


def moe_swiglu(h, router, wg, wu, wd):
    logits = (h @ router).astype(F32)
    top_v, top_i = lax.top_k(logits, TOP_K)
    top_w = jax.nn.softmax(top_v, axis=-1)
    gate = jnp.sum(jax.nn.one_hot(top_i, N_EXPERTS, dtype=F32) * top_w[..., None], axis=-2)
    out = jnp.zeros_like(h)
    for e in range(N_EXPERTS):
        out = out + gate[..., e:e + 1].astype(h.dtype) * swiglu(h, wg[e], wu[e], wd[e])
    return out


def setup_inputs(seed: int = 0) -> dict:
    key = jax.random.key(seed)
    ks = jax.random.split(key, 32)
    L, D, G, N, P = DEPTH, D_MODEL, S5_GROUPS, S5_STATE, S5_GROUP_CH

    def nrm(k, shape, scale):
        return jax.random.normal(k, shape, F32) * scale

    return {
        'x': nrm(ks[0], (BATCH, SEQ, D), 1.0),
        'norm_mix': 1.0 + nrm(ks[1], (L, D), 0.02),
        'w_in': nrm(ks[2], (L, D, IN_COLS), D ** -0.5),
        's5_lambda_re': -0.5 + nrm(ks[3], (L, G, N), 0.01),
        's5_lambda_im': jnp.pi * jnp.arange(N, dtype=F32) + nrm(ks[4], (L, G, N), 0.01),
        's5_log_dt': jax.random.uniform(ks[5], (L, G), F32, math.log(S5_DT_MIN), math.log(S5_DT_MAX)),
        's5_b_re': nrm(ks[6], (L, G, N, P), (2 * P) ** -0.5),
        's5_b_im': nrm(ks[7], (L, G, N, P), (2 * P) ** -0.5),
        's5_c_re': nrm(ks[8], (L, G, P, N), (2 * N) ** -0.5),
        's5_c_im': nrm(ks[9], (L, G, P, N), (2 * N) ** -0.5),
        's5_d': nrm(ks[10], (L, S5_WIDTH), 1.0),
        's5_w_glu': nrm(ks[11], (L, S5_WIDTH, S5_WIDTH), S5_WIDTH ** -0.5),
        's5_b_glu': nrm(ks[12], (L, S5_WIDTH), 0.02),
        'c_sinks': nrm(ks[13], (L, C_Q_HEADS), 1.0),
        'w_branch_a': nrm(ks[14], (L, A_WIDTH, D), A_WIDTH ** -0.5),
        'w_branch_b': nrm(ks[15], (L, S5_WIDTH, D), S5_WIDTH ** -0.5),
        'w_branch_c': nrm(ks[16], (L, C_WIDTH, D), C_WIDTH ** -0.5),
        'w_out': nrm(ks[17], (L, D, D), D ** -0.5),
        'norm_ffn': 1.0 + nrm(ks[18], (L, D), 0.02),
        'ffn_w_gate': nrm(ks[19], (N_DENSE, D, D_FF), D ** -0.5),
        'ffn_w_up': nrm(ks[20], (N_DENSE, D, D_FF), D ** -0.5),
        'ffn_w_down': nrm(ks[21], (N_DENSE, D_FF, D), D_FF ** -0.5),
        'moe_router': nrm(ks[22], (N_MOE, D, N_EXPERTS), D ** -0.5),
        'moe_w_gate': nrm(ks[23], (N_MOE, N_EXPERTS, D, D_FF), D ** -0.5),
        'moe_w_up': nrm(ks[24], (N_MOE, N_EXPERTS, D, D_FF), D ** -0.5),
        'moe_w_down': nrm(ks[25], (N_MOE, N_EXPERTS, D_FF, D), D_FF ** -0.5),
        'norm_final': 1.0 + nrm(ks[26], (D,), 0.02),
    }


def reference(x, norm_mix, w_in, s5_lambda_re, s5_lambda_im, s5_log_dt, s5_b_re, s5_b_im,
              s5_c_re, s5_c_im, s5_d, s5_w_glu, s5_b_glu, c_sinks, w_branch_a, w_branch_b,
              w_branch_c, w_out, norm_ffn, ffn_w_gate, ffn_w_up, ffn_w_down, moe_router,
              moe_w_gate, moe_w_up, moe_w_down, norm_final):
    Bt, S, D = x.shape
    offsets = [int(o) for o in np.cumsum(IN_SPLITS)[:-1]]
    positions = jnp.arange(S, dtype=F32)
    for l in range(DEPTH):
        h = rmsnorm(x, norm_mix[l])
        proj = h @ w_in[l]
        pa, pb, pq, pk, pv, pg = jnp.split(proj, offsets, axis=-1)
        o_a = dilated_mixer(pa)
        o_b = s5_mixer(pb, s5_lambda_re[l], s5_lambda_im[l], s5_log_dt[l], s5_b_re[l], s5_b_im[l],
                       s5_c_re[l], s5_c_im[l], s5_d[l], s5_w_glu[l], s5_b_glu[l])
        o_c = swa_sink_mixer(pq, pk, pv, c_sinks[l], positions)
        gates = jax.nn.sigmoid(pg.astype(F32)).astype(x.dtype)
        g_a, g_b, g_c = jnp.split(gates, N_BRANCHES, axis=-1)
        merged = (g_a * (o_a @ w_branch_a[l])
                  + g_b * (o_b @ w_branch_b[l])
                  + g_c * (o_c @ w_branch_c[l]))
        x = x + merged @ w_out[l]
        h = rmsnorm(x, norm_ffn[l])
        if l % 2 == 0:
            i = l // 2
            x = x + swiglu(h, ffn_w_gate[i], ffn_w_up[i], ffn_w_down[i])
        else:
            i = l // 2
            x = x + moe_swiglu(h, moe_router[i], moe_w_gate[i], moe_w_up[i], moe_w_down[i])
    return rmsnorm(x, norm_final)
```

```python
import functools
import math

import jax
import jax.numpy as jnp
import numpy as np
from jax import lax
from jax.experimental import pallas as pl
from jax.experimental.pallas import tpu as pltpu

F32 = jnp.float32
BF16 = jnp.bfloat16

D_MODEL = 1024
HEAD_DIM = 64
BLOCK = 128
LANES = 128
SUBLANES = 8
DILATIONS = (1, 4, 16)
N_DIL = 3
A_HEADS = 4
A_WIDTH = A_HEADS * HEAD_DIM
A_QKV = 3 * N_DIL * A_WIDTH
S5_WIDTH = 256
S5_GROUPS = 16
S5_GROUP_CH = 16
S5_STATE = 64
S5_N = S5_GROUPS * S5_STATE
C_Q_HEADS = 8
C_KV_HEADS = 2
C_WIDTH = C_Q_HEADS * HEAD_DIM
ROPE_THETA = 150000.0
N_GATE = 3 * D_MODEL
IN_COLS = 6400
D_FF = 3584
N_EXPERTS = 8
RMS_EPS = 1e-6
ATT_SCALE = HEAD_DIM ** -0.5

OFF_GATE = 0
OFF_A = N_GATE
OFF_B = OFF_A + A_QKV
OFF_Q = OFF_B + S5_WIDTH
OFF_K = OFF_Q + C_WIDTH
OFF_V = OFF_K + C_KV_HEADS * HEAD_DIM

TM_PROJ = 1024
TN_PROJ = 1280
TM_MERGE = 512
TM_FFN = 1024
TF_FFN = 512
TS_S5 = 512
TM_MOE = 1024
CH_MOE = 128
VMEM_LIMIT = 56 * 1024 * 1024


def _cparams(sem):
    return pltpu.CompilerParams(dimension_semantics=sem, vmem_limit_bytes=VMEM_LIMIT)


def _dot(a, b):
    return jnp.dot(a, b, preferred_element_type=F32)


def _dot_nt(a, b):
    return lax.dot_general(a, b, (((1,), (1,)), ((), ())), preferred_element_type=F32)


def _rms(x, g):
    return x * lax.rsqrt(jnp.mean(x * x, axis=-1, keepdims=True) + RMS_EPS) * g


def _inproj_kernel(x_ref, g_ref, w_ref, o_ref, h_ref):
    @pl.when(pl.program_id(1) == 0)
    def _():
        h_ref[...] = _rms(x_ref[...], g_ref[...]).astype(BF16)

    o_ref[...] = _dot(h_ref[...], w_ref[...]).astype(o_ref.dtype)


def _inproj(x, g, w):
    t, d = x.shape
    n = w.shape[1]
    return pl.pallas_call(
        _inproj_kernel,
        grid=(t // TM_PROJ, n // TN_PROJ),
        in_specs=[
            pl.BlockSpec((TM_PROJ, d), lambda i, j: (i, 0)),
            pl.BlockSpec((1, d), lambda i, j: (0, 0)),
            pl.BlockSpec((d, TN_PROJ), lambda i, j: (0, j)),
        ],
        out_specs=pl.BlockSpec((TM_PROJ, TN_PROJ), lambda i, j: (i, j)),
        out_shape=jax.ShapeDtypeStruct((t, n), BF16),
        scratch_shapes=[pltpu.VMEM((TM_PROJ, d), BF16)],
        compiler_params=_cparams(("parallel", "arbitrary")),
        name="inproj",
    )(x, g, w)


def _band_mask(rows, width):
    qi = lax.broadcasted_iota(jnp.int32, (rows, width), 0) % BLOCK
    ki = lax.broadcasted_iota(jnp.int32, (rows, width), 1)
    return (ki >= qi) & (ki <= qi + BLOCK)


def _causal_mask(rows, width):
    qi = lax.broadcasted_iota(jnp.int32, (rows, width), 0) % BLOCK
    ki = lax.broadcasted_iota(jnp.int32, (rows, width), 1)
    return ki <= qi


def _softmax_pv(s, mask, v, sink=None):
    s = jnp.where(mask, s, -jnp.inf)
    m = jnp.max(s, axis=-1, keepdims=True)
    if sink is not None:
        m = jnp.maximum(m, sink)
    p = jnp.exp(s - m)
    den = jnp.sum(p, axis=-1, keepdims=True)
    if sink is not None:
        den = den + jnp.exp(sink - m)
    o = _dot(p.astype(BF16), v)
    return o, m, den


def _dilated_kernel(q0_ref, q1_ref, q2_ref, k0_ref, k1_ref, k2_ref, v0_ref, v1_ref, v2_ref,
                    o_ref, qf_ref, kf_ref, vf_ref, og_ref, lg_ref):
    seq = o_ref.shape[0]
    lane = lax.broadcasted_iota(jnp.int32, (1, LANES), 1)
    lo = lane < HEAD_DIM
    band = _band_mask(2 * BLOCK, 2 * BLOCK)
    causal = _causal_mask(2 * BLOCK, BLOCK)

    def attend(q, k, v, mask):
        zero = jnp.zeros_like(q)
        qs = jnp.concatenate([jnp.where(lo, q, zero), jnp.where(lo, zero, q)], axis=0)
        s = _dot_nt(qs, k)
        o, m, den = _softmax_pv(s, mask, v)
        o = o * (1.0 / den)
        lse = m + jnp.log(den)
        o_pair = jnp.where(lo, o[:BLOCK], o[BLOCK:])
        l_pair = jnp.where(lo, lse[:BLOCK], lse[BLOCK:])
        return o_pair, l_pair

    for idx, (qr, kr, vr) in enumerate(((q1_ref, k1_ref, v1_ref), (q2_ref, k2_ref, v2_ref))):
        qf_ref[idx] = qr[...].astype(F32) * ATT_SCALE
        kf_ref[idx] = kr[...].astype(F32)
        vf_ref[idx] = vr[...].astype(F32)

    o_p, l_p = attend(q0_ref[pl.ds(0, BLOCK), :] * ATT_SCALE, k0_ref[pl.ds(0, BLOCK), :],
                      v0_ref[pl.ds(0, BLOCK), :], causal)
    og_ref[0, pl.ds(0, BLOCK), :] = o_p
    lg_ref[0, pl.ds(0, BLOCK), :] = l_p

    def body0(n, carry):
        r0 = pl.multiple_of(n * BLOCK, BLOCK)
        w0 = pl.multiple_of((n - 1) * BLOCK, BLOCK)
        o_p, l_p = attend(q0_ref[pl.ds(r0, BLOCK), :] * ATT_SCALE, k0_ref[pl.ds(w0, 2 * BLOCK), :],
                          v0_ref[pl.ds(w0, 2 * BLOCK), :], band)
        og_ref[0, pl.ds(r0, BLOCK), :] = o_p
        lg_ref[0, pl.ds(r0, BLOCK), :] = l_p
        return carry

    lax.fori_loop(1, seq // BLOCK, body0, 0)

    for idx, d in ((0, DILATIONS[1]), (1, DILATIONS[2])):
        g = idx + 1
        nblk = seq // d // BLOCK
        for r in range(d):
            for n in range(nblk):
                q = qf_ref[idx, pl.ds(r + n * BLOCK * d, BLOCK, stride=d), :].astype(BF16)
                if n == 0:
                    k = kf_ref[idx, pl.ds(r, BLOCK, stride=d), :].astype(BF16)
                    v = vf_ref[idx, pl.ds(r, BLOCK, stride=d), :].astype(BF16)
                    o_p, l_p = attend(q, k, v, causal)
                else:
                    w0 = r + (n - 1) * BLOCK * d
                    k = kf_ref[idx, pl.ds(w0, 2 * BLOCK, stride=d), :].astype(BF16)
                    v = vf_ref[idx, pl.ds(w0, 2 * BLOCK, stride=d), :].astype(BF16)
                    o_p, l_p = attend(q, k, v, band)
                og_ref[g, pl.ds(r + n * BLOCK * d, BLOCK, stride=d), :] = o_p
                lg_ref[g, pl.ds(r + n * BLOCK * d, BLOCK, stride=d), :] = l_p

    l0, l1, l2 = lg_ref[0], lg_ref[1], lg_ref[2]
    mx = jnp.maximum(jnp.maximum(l0, l1), l2)
    e0, e1, e2 = jnp.exp(l0 - mx), jnp.exp(l1 - mx), jnp.exp(l2 - mx)
    tot = e0 + e1 + e2
    o_ref[...] = ((e0 * og_ref[0] + e1 * og_ref[1] + e2 * og_ref[2]) / tot).astype(o_ref.dtype)


def _dilated_mixer(proj3):
    b, s, _ = proj3.shape
    base = OFF_A // LANES

    def spec(which, g):
        col = base + which * (N_DIL * A_WIDTH // LANES) + g * (A_WIDTH // LANES)
        return pl.BlockSpec((None, s, LANES), lambda i, hp, col=col: (i, 0, col + hp))

    in_specs = [spec(w, g) for w in range(3) for g in range(N_DIL)]
    return pl.pallas_call(
        _dilated_kernel,
        grid=(b, A_WIDTH // LANES),
        in_specs=in_specs,
        out_specs=pl.BlockSpec((None, s, LANES), lambda i, hp: (i, 0, hp)),
        out_shape=jax.ShapeDtypeStruct((b, s, A_WIDTH), BF16),
        scratch_shapes=[
            pltpu.VMEM((2, s, LANES), F32),
            pltpu.VMEM((2, s, LANES), F32),
            pltpu.VMEM((2, s, LANES), F32),
            pltpu.VMEM((N_DIL, s, LANES), F32),
            pltpu.VMEM((N_DIL, s, LANES), F32),
        ],
        compiler_params=_cparams(("parallel", "parallel")),
        name="dilated_mixer",
    )(*([proj3] * 9))


def _swa_kernel(sink_ref, q_ref, k_ref, v_ref, cos_ref, sin_ref, o_ref, qs_ref, ks_ref, vs_ref):
    seq = o_ref.shape[0]
    g = pl.program_id(1)
    rep = C_Q_HEADS // C_KV_HEADS
    lane = lax.broadcasted_iota(jnp.int32, (1, LANES), 1)
    lo = lane < HEAD_DIM
    first_half = (lane % HEAD_DIM) < (HEAD_DIM // 2)
    cos = cos_ref[...]
    sin = sin_ref[...]

    def rope(x):
        swapped = jnp.where(first_half, pltpu.roll(x, LANES - HEAD_DIM // 2, 1),
                            pltpu.roll(x, HEAD_DIM // 2, 1))
        return x * cos + swapped * sin

    mine = (lane // HEAD_DIM) == g
    kf = jnp.where(mine, rope(k_ref[...].astype(F32)), 0.0)
    ks_ref[...] = (kf + pltpu.roll(kf, HEAD_DIM, 1)).astype(BF16)
    vf = jnp.where(mine, v_ref[...].astype(F32), 0.0)
    vs_ref[...] = (vf + pltpu.roll(vf, HEAD_DIM, 1)).astype(BF16)
    for c in range(rep // 2):
        sl = slice(c * LANES, (c + 1) * LANES)
        qs_ref[:, sl] = (rope(q_ref[:, sl].astype(F32)) * ATT_SCALE).astype(BF16)

    rows = rep * BLOCK
    hrow = lax.broadcasted_iota(jnp.int32, (rows, 1), 0) // BLOCK
    sink = jnp.zeros((rows, 1), F32)
    for h in range(rep):
        sink = jnp.where(hrow == h, sink_ref[g * rep + h], sink)
    band = _band_mask(rows, 2 * BLOCK)
    causal = _causal_mask(rows, BLOCK)

    def block(r0, k, v, mask):
        parts = []
        for h in range(rep):
            q = qs_ref[pl.ds(r0, BLOCK), (h // 2) * LANES:(h // 2 + 1) * LANES]
            zero = jnp.zeros_like(q)
            parts.append(jnp.where(lo, q, zero) if h % 2 == 0 else jnp.where(lo, zero, q))
        s = _dot_nt(jnp.concatenate(parts, axis=0), k)
        o, _, den = _softmax_pv(s, mask, v, sink)
        o = o * (1.0 / den)
        for c in range(rep // 2):
            pair = jnp.where(lo, o[2 * c * BLOCK:(2 * c + 1) * BLOCK], o[(2 * c + 1) * BLOCK:(2 * c + 2) * BLOCK])
            o_ref[pl.ds(r0, BLOCK), c * LANES:(c + 1) * LANES] = pair.astype(o_ref.dtype)

    block(0, ks_ref[pl.ds(0, BLOCK), :], vs_ref[pl.ds(0, BLOCK), :], causal)

    def body(n, carry):
        r0 = pl.multiple_of(n * BLOCK, BLOCK)
        w0 = pl.multiple_of((n - 1) * BLOCK, BLOCK)
        block(r0, ks_ref[pl.ds(w0, 2 * BLOCK), :], vs_ref[pl.ds(w0, 2 * BLOCK), :], band)
        return carry

    lax.fori_loop(1, seq // BLOCK, body, 0)


def _rope_tables(seq):
    inv = ROPE_THETA ** (-jnp.arange(0, HEAD_DIM, 2, dtype=F32) / HEAD_DIM)
    ang = jnp.arange(seq, dtype=F32)[:, None] * inv[None, :]
    reps = LANES // (HEAD_DIM // 2)
    cos = jnp.tile(jnp.cos(ang), (1, reps))
    sign = jnp.where((jnp.arange(LANES) % HEAD_DIM) < HEAD_DIM // 2, -1.0, 1.0).astype(F32)
    sin = jnp.tile(jnp.sin(ang), (1, reps)) * sign[None, :]
    return cos, sin


def _swa_mixer(proj3, sinks, cos, sin):
    b, s, _ = proj3.shape
    qw = C_WIDTH // C_KV_HEADS
    return pl.pallas_call(
        _swa_kernel,
        grid=(b, C_KV_HEADS),
        in_specs=[
            pl.BlockSpec(memory_space=pltpu.SMEM),
            pl.BlockSpec((None, s, qw), lambda i, g: (i, 0, OFF_Q // qw + g)),
            pl.BlockSpec((None, s, LANES), lambda i, g: (i, 0, OFF_K // LANES)),
            pl.BlockSpec((None, s, LANES), lambda i, g: (i, 0, OFF_V // LANES)),
            pl.BlockSpec((s, LANES), lambda i, g: (0, 0)),
            pl.BlockSpec((s, LANES), lambda i, g: (0, 0)),
        ],
        out_specs=pl.BlockSpec((None, s, qw), lambda i, g: (i, 0, g)),
        out_shape=jax.ShapeDtypeStruct((b, s, C_WIDTH), BF16),
        scratch_shapes=[
            pltpu.VMEM((s, qw), BF16),
            pltpu.VMEM((s, LANES), BF16),
            pltpu.VMEM((s, LANES), BF16),
        ],
        compiler_params=_cparams(("parallel", "parallel")),
        name="swa_mixer",
    )(sinks, proj3, proj3, proj3, cos, sin)


def _cmul(ar, ai, br, bi):
    return ar * br - ai * bi, ar * bi + ai * br


def _s5_kernel(u_ref, lr_ref, li_ref, ldt_ref, bre_ref, bim_ref, cre_ref, cim_ref, d_ref, wg_ref, bg_ref,
               o_ref, hr_ref, hi_ref, pw_ref, car_ref):
    ts = u_ref.shape[0]

    @pl.when(pl.program_id(1) == 0)
    def _():
        lr, li = lr_ref[...], li_ref[...]
        dt = jnp.exp(ldt_ref[...])
        mag = jnp.exp(lr * dt)
        a_re, a_im = mag * jnp.cos(li * dt), mag * jnp.sin(li * dt)
        nr, ni = a_re - 1.0, a_im
        den = lr * lr + li * li
        z_re = (nr * lr + ni * li) / den
        z_im = (ni * lr - nr * li) / den
        row = lax.broadcasted_iota(jnp.int32, (SUBLANES, 1), 0)
        pows = [(a_re, a_im)]
        for _ in range(SUBLANES - 1):
            pows.append(_cmul(pows[-1][0], pows[-1][1], a_re, a_im))
        for j, sft in enumerate((1, 2, 4)):
            pr, pi = pows[sft - 1]
            pw_ref[2 * j] = jnp.where(row >= sft, pr, 0.0)
            pw_ref[2 * j + 1] = jnp.where(row >= sft, pi, 0.0)
        cr = jnp.zeros((SUBLANES, S5_N), F32)
        ci = jnp.zeros((SUBLANES, S5_N), F32)
        for i in range(SUBLANES):
            cr = jnp.where(row == i, pows[i][0], cr)
            ci = jnp.where(row == i, pows[i][1], ci)
        pw_ref[6] = cr
        pw_ref[7] = ci
        pw_ref[8] = jnp.broadcast_to(z_re, (SUBLANES, S5_N))
        pw_ref[9] = jnp.broadcast_to(z_im, (SUBLANES, S5_N))
        car_ref[...] = jnp.zeros_like(car_ref)

    u = u_ref[...]
    bu_re = _dot(u, bre_ref[...])
    bu_im = _dot(u, bim_ref[...])
    z_re = pw_ref[8, 0:1, :]
    z_im = pw_ref[9, 0:1, :]
    hr_ref[...] = z_re * bu_re - z_im * bu_im
    hi_ref[...] = z_re * bu_im + z_im * bu_re

    mults = [(pw_ref[2 * j], pw_ref[2 * j + 1]) for j in range(3)]
    acr, aci = pw_ref[6], pw_ref[7]

    def chunk(k, carry):
        cr, ci = carry
        r0 = pl.multiple_of(k * SUBLANES, SUBLANES)
        xr = hr_ref[pl.ds(r0, SUBLANES), :]
        xi = hi_ref[pl.ds(r0, SUBLANES), :]
        for (mr, mi), sft in zip(mults, (1, 2, 4)):
            sr = pltpu.roll(xr, sft, 0)
            si = pltpu.roll(xi, sft, 0)
            xr, xi = xr + (mr * sr - mi * si), xi + (mr * si + mi * sr)
        xr = xr + (acr * cr - aci * ci)
        xi = xi + (acr * ci + aci * cr)
        hr_ref[pl.ds(r0, SUBLANES), :] = xr
        hi_ref[pl.ds(r0, SUBLANES), :] = xi
        return xr[SUBLANES - 1:SUBLANES, :], xi[SUBLANES - 1:SUBLANES, :]

    cr, ci = lax.fori_loop(0, ts // SUBLANES, chunk, (car_ref[0, 0:1, :], car_ref[1, 0:1, :]))
    car_ref[0] = jnp.broadcast_to(cr, (SUBLANES, S5_N))
    car_ref[1] = jnp.broadcast_to(ci, (SUBLANES, S5_N))

    y = _dot(hr_ref[...].astype(BF16), cre_ref[...]) - _dot(hi_ref[...].astype(BF16), cim_ref[...])
    y = y + d_ref[...] * u.astype(F32)
    gl = jax.nn.gelu(y)
    out = gl * jax.nn.sigmoid(_dot(gl.astype(BF16), wg_ref[...]) + bg_ref[...])
    o_ref[...] = out.astype(o_ref.dtype)


def _s5_mixer(proj3, lr, li, ldt, bre, bim, cre, cim, dsk, wglu, bglu):
    b, s, _ = proj3.shape
    full = lambda shape: pl.BlockSpec(shape, lambda i, t: (0,) * len(shape))
    return pl.pallas_call(
        _s5_kernel,
        grid=(b, s // TS_S5),
        in_specs=[
            pl.BlockSpec((None, TS_S5, S5_WIDTH), lambda i, t: (i, t, OFF_B // S5_WIDTH)),
            full((1, S5_N)), full((1, S5_N)), full((1, S5_N)),
            full((S5_WIDTH, S5_N)), full((S5_WIDTH, S5_N)),
            full((S5_N, S5_WIDTH)), full((S5_N, S5_WIDTH)),
            full((1, S5_WIDTH)), full((S5_WIDTH, S5_WIDTH)), full((1, S5_WIDTH)),
        ],
        out_specs=pl.BlockSpec((None, TS_S5, S5_WIDTH), lambda i, t: (i, t, 0)),
        out_shape=jax.ShapeDtypeStruct((b, s, S5_WIDTH), BF16),
        scratch_shapes=[
            pltpu.VMEM((TS_S5, S5_N), F32),
            pltpu.VMEM((TS_S5, S5_N), F32),
            pltpu.VMEM((10, SUBLANES, S5_N), F32),
            pltpu.VMEM((2, SUBLANES, S5_N), F32),
        ],
        compiler_params=_cparams(("parallel", "arbitrary")),
        name="s5_mixer",
    )(proj3, lr, li, ldt, bre, bim, cre, cim, dsk, wglu, bglu)


def _merge_kernel(x_ref, oa_ref, ob_ref, oc_ref, ga_ref, gb_ref, gc_ref, wa_ref, wb_ref, wc_ref, wo_ref,
                  gn_ref, *rest, with_router):
    if with_router:
        rt_ref, xo_ref, h_ref, lg_ref = rest
    else:
        xo_ref, h_ref = rest

    def sig(r):
        return jax.nn.sigmoid(r[...].astype(F32))

    merged = (sig(ga_ref) * _dot(oa_ref[...], wa_ref[...])
              + sig(gb_ref) * _dot(ob_ref[...], wb_ref[...])
              + sig(gc_ref) * _dot(oc_ref[...], wc_ref[...]))
    xn = x_ref[...] + _dot(merged.astype(BF16), wo_ref[...])
    xo_ref[...] = xn
    h = _rms(xn, gn_ref[...])
    h_ref[...] = h.astype(BF16)
    if with_router:
        lg_ref[...] = lax.dot_general(rt_ref[...], h, (((1,), (1,)), ((), ())),
                                      precision=lax.Precision.HIGHEST, preferred_element_type=F32)


def _merge(x, oa, ob, oc, proj, wa, wb, wc, wo, gn, router_t=None):
    t, d = x.shape
    tm = TM_MERGE
    with_router = router_t is not None
    row = lambda w: pl.BlockSpec((tm, w), lambda i: (i, 0))
    full = lambda a: pl.BlockSpec(a.shape, lambda i: (0, 0))
    in_specs = [row(d), row(A_WIDTH), row(S5_WIDTH), row(C_WIDTH)]
    in_specs += [pl.BlockSpec((tm, d), lambda i, c=c: (i, c)) for c in range(3)]
    in_specs += [full(wa), full(wb), full(wc), full(wo), full(gn)]
    args = [x, oa, ob, oc, proj, proj, proj, wa, wb, wc, wo, gn]
    out_specs = [row(d), row(d)]
    out_shape = [jax.ShapeDtypeStruct((t, d), F32), jax.ShapeDtypeStruct((t, d), BF16)]
    if with_router:
        in_specs.append(full(router_t))
        args.append(router_t)
        out_specs.append(pl.BlockSpec((N_EXPERTS, tm), lambda i: (0, i)))
        out_shape.append(jax.ShapeDtypeStruct((N_EXPERTS, t), F32))
    return pl.pallas_call(
        functools.partial(_merge_kernel, with_router=with_router),
        grid=(t // tm,),
        in_specs=in_specs,
        out_specs=out_specs,
        out_shape=out_shape,
        compiler_params=_cparams(("parallel",)),
        name="merge_router" if with_router else "merge",
    )(*args)


def _ffn_kernel(x_ref, h_ref, wg_ref, wu_ref, wd_ref, o_ref):
    @pl.when(pl.program_id(1) == 0)
    def _():
        o_ref[...] = x_ref[...]

    h = h_ref[...]
    act = jax.nn.silu(_dot(h, wg_ref[...])) * _dot(h, wu_ref[...])
    o_ref[...] += _dot(act.astype(BF16), wd_ref[...])


def _ffn(x, h, wg, wu, wd):
    t, d = x.shape
    ff = wg.shape[1]
    return pl.pallas_call(
        _ffn_kernel,
        grid=(t // TM_FFN, ff // TF_FFN),
        in_specs=[
            pl.BlockSpec((TM_FFN, d), lambda i, f: (i, 0)),
            pl.BlockSpec((TM_FFN, d), lambda i, f: (i, 0)),
            pl.BlockSpec((d, TF_FFN), lambda i, f: (0, f)),
            pl.BlockSpec((d, TF_FFN), lambda i, f: (0, f)),
            pl.BlockSpec((TF_FFN, d), lambda i, f: (f, 0)),
        ],
        out_specs=pl.BlockSpec((TM_FFN, d), lambda i, f: (i, 0)),
        out_shape=jax.ShapeDtypeStruct((t, d), F32),
        compiler_params=_cparams(("parallel", "arbitrary")),
        name="dense_ffn",
    )(x, h, wg, wu, wd)


def _route_kernel(lg_ref, gate_ref, rank_ref, cnt_ref):
    lg = lg_ref[...]
    tm = lg.shape[1]
    eidx = lax.broadcasted_iota(jnp.int32, lg.shape, 0)
    m1 = jnp.max(lg, axis=0, keepdims=True)
    i1 = jnp.min(jnp.where(lg == m1, eidx, N_EXPERTS), axis=0, keepdims=True)
    rest = jnp.where(eidx == i1, -jnp.inf, lg)
    m2 = jnp.max(rest, axis=0, keepdims=True)
    i2 = jnp.min(jnp.where(rest == m2, eidx, N_EXPERTS), axis=0, keepdims=True)
    e2 = jnp.exp(m2 - m1)
    tot = 1.0 + e2
    sel1 = eidx == i1
    sel2 = eidx == i2
    gate_ref[...] = jnp.where(sel1, 1.0 / tot, jnp.where(sel2, e2 / tot, 0.0))
    sel = (sel1 | sel2).astype(jnp.int32)
    lane = lax.broadcasted_iota(jnp.int32, lg.shape, 1)
    c = sel
    sft = 1
    while sft < tm:
        c = c + jnp.where(lane >= sft, pltpu.roll(c, sft, 1), 0)
        sft *= 2
    rank_ref[...] = jnp.where(sel > 0, c - 1, -1)
    cnt_ref[...] = jnp.broadcast_to(jnp.sum(sel, axis=1, keepdims=True), cnt_ref.shape)


def _route(logits_t, tm):
    e, t = logits_t.shape
    nt = t // tm
    return pl.pallas_call(
        _route_kernel,
        grid=(nt,),
        in_specs=[pl.BlockSpec((e, tm), lambda i: (0, i))],
        out_specs=[
            pl.BlockSpec((e, tm), lambda i: (0, i)),
            pl.BlockSpec((e, tm), lambda i: (0, i)),
            pl.BlockSpec((None, e, LANES), lambda i: (i, 0, 0)),
        ],
        out_shape=[
            jax.ShapeDtypeStruct((e, t), F32),
            jax.ShapeDtypeStruct((e, t), jnp.int32),
            jax.ShapeDtypeStruct((nt, e, LANES), jnp.int32),
        ],
        compiler_params=_cparams(("parallel",)),
        name="route_top2",
    )(logits_t)


def _moe_kernel(cnt_ref, x_ref, h_ref, rrow_ref, rcol_ref, gcol_ref, wg_ref, wu_ref, wd_ref,
                o_ref, hc_ref, y_ref):
    i, e, f = pl.program_id(0), pl.program_id(1), pl.program_id(2)
    nf = pl.num_programs(2)
    tm = h_ref.shape[0]
    n_sel = cnt_ref[i * N_EXPERTS + e]
    n_chunks = (n_sel + (CH_MOE - 1)) // CH_MOE

    @pl.when((e == 0) & (f == 0))
    def _():
        o_ref[...] = x_ref[...]

    @pl.when(f == 0)
    def _():
        rrow = rrow_ref[...]

        def gather(c, carry):
            r0 = pl.multiple_of(c * CH_MOE, CH_MOE)
            slot = lax.broadcasted_iota(jnp.int32, (CH_MOE, tm), 0) + r0
            onehot = jnp.where(slot == rrow, 1.0, 0.0).astype(BF16)
            hc_ref[pl.ds(r0, CH_MOE), :] = _dot(onehot, h_ref[...]).astype(BF16)
            return carry

        lax.fori_loop(0, n_chunks, gather, 0)

    def ffn(c, carry):
        r0 = pl.multiple_of(c * CH_MOE, CH_MOE)
        hc = hc_ref[pl.ds(r0, CH_MOE), :]
        act = jax.nn.silu(_dot(hc, wg_ref[...])) * _dot(hc, wu_ref[...])
        part = _dot(act.astype(BF16), wd_ref[...])

        @pl.when(f == 0)
        def _():
            y_ref[pl.ds(r0, CH_MOE), :] = part

        @pl.when(f > 0)
        def _():
            y_ref[pl.ds(r0, CH_MOE), :] += part

        return carry

    lax.fori_loop(0, n_chunks, ffn, 0)

    @pl.when(f == nf - 1)
    def _():
        rcol = rcol_ref[...]
        gcol = gcol_ref[...]

        def scatter(c, acc):
            r0 = pl.multiple_of(c * CH_MOE, CH_MOE)
            slot = lax.broadcasted_iota(jnp.int32, (tm, CH_MOE), 1) + r0
            onehot = jnp.where(slot == rcol, 1.0, 0.0).astype(BF16)
            return acc + _dot(onehot, y_ref[pl.ds(r0, CH_MOE), :].astype(BF16))

        picked = lax.fori_loop(0, n_chunks, scatter, jnp.zeros((tm, D_MODEL), F32))
        o_ref[...] += gcol * picked


def _moe(x, h, counts, rank_row, rank_col, gate_col, wg, wu, wd, tm):
    t, d = x.shape
    ne, _, ff = wg.shape
    nt = t // tm
    grid_spec = pltpu.PrefetchScalarGridSpec(
        num_scalar_prefetch=1,
        grid=(nt, ne, ff // TF_FFN),
        in_specs=[
            pl.BlockSpec((tm, d), lambda i, e, f, c: (i, 0)),
            pl.BlockSpec((tm, d), lambda i, e, f, c: (i, 0)),
            pl.BlockSpec((None, 1, tm), lambda i, e, f, c: (e, 0, i)),
            pl.BlockSpec((None, tm, 1), lambda i, e, f, c: (e, i, 0)),
            pl.BlockSpec((None, tm, 1), lambda i, e, f, c: (e, i, 0)),
            pl.BlockSpec((None, d, TF_FFN), lambda i, e, f, c: (e, 0, f)),
            pl.BlockSpec((None, d, TF_FFN), lambda i, e, f, c: (e, 0, f)),
            pl.BlockSpec((None, TF_FFN, d), lambda i, e, f, c: (e, f, 0)),
        ],
        out_specs=pl.BlockSpec((tm, d), lambda i, e, f, c: (i, 0)),
        scratch_shapes=[pltpu.VMEM((tm, d), BF16), pltpu.VMEM((tm, d), F32)],
    )
    return pl.pallas_call(
        _moe_kernel,
        grid_spec=grid_spec,
        out_shape=jax.ShapeDtypeStruct((t, d), F32),
        compiler_params=_cparams(("parallel", "arbitrary", "arbitrary")),
        name="moe_experts",
    )(counts, x, h, rank_row, rank_col, gate_col, wg, wu, wd)


def _final_norm_kernel(x_ref, g_ref, o_ref):
    o_ref[...] = _rms(x_ref[...], g_ref[...])


def _final_norm(x, g):
    t, d = x.shape
    tm = TM_PROJ
    return pl.pallas_call(
        _final_norm_kernel,
        grid=(t // tm,),
        in_specs=[pl.BlockSpec((tm, d), lambda i: (i, 0)), pl.BlockSpec((1, d), lambda i: (0, 0))],
        out_specs=pl.BlockSpec((tm, d), lambda i: (i, 0)),
        out_shape=jax.ShapeDtypeStruct((t, d), F32),
        compiler_params=_cparams(("parallel",)),
        name="final_norm",
    )(x, g)


def _block_diag(blocks):
    l, g, r, c = blocks.shape
    on_diag = jnp.eye(g, dtype=bool)[None, :, None, :, None]
    out = jnp.where(on_diag, blocks[:, :, :, None, :], jnp.zeros((), blocks.dtype))
    return out.reshape(l, g * r, g * c)


def kernel(x, norm_mix, w_in, s5_lambda_re, s5_lambda_im, s5_log_dt, s5_b_re, s5_b_im, s5_c_re, s5_c_im,
           s5_d, s5_w_glu, s5_b_glu, c_sinks, w_branch_a, w_branch_b, w_branch_c, w_out, norm_ffn,
           ffn_w_gate, ffn_w_up, ffn_w_down, moe_router, moe_w_gate, moe_w_up, moe_w_down, norm_final):
    b, s, d = x.shape
    depth = w_in.shape[0]
    t = b * s
    n_split = A_QKV + S5_WIDTH + C_WIDTH + 2 * C_KV_HEADS * HEAD_DIM

    w_in_p = jnp.concatenate([w_in[:, :, n_split:], w_in[:, :, :n_split]], axis=-1).astype(BF16)
    wa, wb, wc, wo = (w.astype(BF16) for w in (w_branch_a, w_branch_b, w_branch_c, w_out))
    fg, fu, fd = (w.astype(BF16) for w in (ffn_w_gate, ffn_w_up, ffn_w_down))
    mg, mu, md = (w.astype(BF16) for w in (moe_w_gate, moe_w_up, moe_w_down))
    router_t = jnp.swapaxes(moe_router, 1, 2)
    lam_re = s5_lambda_re.reshape(depth, 1, S5_N)
    lam_im = s5_lambda_im.reshape(depth, 1, S5_N)
    log_dt = jnp.repeat(s5_log_dt, S5_STATE, axis=-1).reshape(depth, 1, S5_N)
    bre = _block_diag(jnp.swapaxes(s5_b_re, 2, 3)).astype(BF16)
    bim = _block_diag(jnp.swapaxes(s5_b_im, 2, 3)).astype(BF16)
    cre = _block_diag(jnp.swapaxes(s5_c_re, 2, 3)).astype(BF16)
    cim = _block_diag(jnp.swapaxes(s5_c_im, 2, 3)).astype(BF16)
    wglu = s5_w_glu.astype(BF16)
    cos, sin = _rope_tables(s)

    xt = x.reshape(t, d)
    for l in range(depth):
        proj = _inproj(xt, norm_mix[l][None, :], w_in_p[l])
        proj3 = proj.reshape(b, s, IN_COLS)
        o_a = _dilated_mixer(proj3).reshape(t, A_WIDTH)
        o_b = _s5_mixer(proj3, lam_re[l], lam_im[l], log_dt[l], bre[l], bim[l], cre[l], cim[l],
                        s5_d[l][None, :], wglu[l], s5_b_glu[l][None, :]).reshape(t, S5_WIDTH)
        o_c = _swa_mixer(proj3, c_sinks[l], cos, sin).reshape(t, C_WIDTH)
        i = l // 2
        if l % 2 == 0:
            xt, h = _merge(xt, o_a, o_b, o_c, proj, wa[l], wb[l], wc[l], wo[l], norm_ffn[l][None, :])
            xt = _ffn(xt, h, fg[i], fu[i], fd[i])
        else:
            xt, h, logits_t = _merge(xt, o_a, o_b, o_c, proj, wa[l], wb[l], wc[l], wo[l],
                                     norm_ffn[l][None, :], router_t[i])
            gate, rank, cnt = _route(logits_t, TM_MOE)
            counts = cnt[:, :, 0].reshape(-1)
            xt = _moe(xt, h, counts, rank[:, None, :], rank[:, :, None], gate[:, :, None],
                      mg[i], mu[i], md[i], TM_MOE)
    return _final_norm(xt, norm_final[None, :]).reshape(b, s, d)
```

```python
import functools
import math

import jax
import jax.numpy as jnp
import numpy as np
from jax import lax
from jax.experimental import pallas as pl
from jax.experimental.pallas import tpu as pltpu

F32 = jnp.float32
BF16 = jnp.bfloat16

D_MODEL = 1024
HEAD_DIM = 64
BLOCK = 128
LANES = 128
SUBLANES = 8
DILATIONS = (1, 4, 16)
N_DIL = 3
A_HEADS = 4
A_WIDTH = A_HEADS * HEAD_DIM
A_QKV = 3 * N_DIL * A_WIDTH
S5_WIDTH = 256
S5_GROUPS = 16
S5_GROUP_CH = 16
S5_STATE = 64
S5_N = S5_GROUPS * S5_STATE
C_Q_HEADS = 8
C_KV_HEADS = 2
C_WIDTH = C_Q_HEADS * HEAD_DIM
ROPE_THETA = 150000.0
N_GATE = 3 * D_MODEL
IN_COLS = 6400
D_FF = 3584
N_EXPERTS = 8
RMS_EPS = 1e-6
ATT_SCALE = HEAD_DIM ** -0.5

OFF_GATE = 0
OFF_A = N_GATE
OFF_B = OFF_A + A_QKV
OFF_Q = OFF_B + S5_WIDTH
OFF_K = OFF_Q + C_WIDTH
OFF_V = OFF_K + C_KV_HEADS * HEAD_DIM

TM_PROJ = 1024
TN_PROJ = 1280
TM_MERGE = 512
TM_FFN = 1024
TF_FFN = 512
TS_S5 = 512
TM_MOE = 2048
M0_MOE = 576
CH_MOE = 128
SC_MOE = 512
VMEM_LIMIT = 56 * 1024 * 1024


def _cparams(sem):
    return pltpu.CompilerParams(dimension_semantics=sem, vmem_limit_bytes=VMEM_LIMIT)


def _dot(a, b):
    return jnp.dot(a, b, preferred_element_type=F32)


def _dot_nt(a, b):
    return lax.dot_general(a, b, (((1,), (1,)), ((), ())), preferred_element_type=F32)


def _rms(x, g):
    return x * lax.rsqrt(jnp.mean(x * x, axis=-1, keepdims=True) + RMS_EPS) * g


def _inproj_kernel(x_ref, g_ref, w_ref, o_ref, h_ref):
    @pl.when(pl.program_id(1) == 0)
    def _():
        h_ref[...] = _rms(x_ref[...], g_ref[...]).astype(BF16)

    o_ref[...] = _dot(h_ref[...], w_ref[...]).astype(o_ref.dtype)


def _inproj(x, g, w, l):
    t, d = x.shape
    n = w.shape[2]
    return pl.pallas_call(
        _inproj_kernel,
        grid=(t // TM_PROJ, n // TN_PROJ),
        in_specs=[
            pl.BlockSpec((TM_PROJ, d), lambda i, j: (i, 0)),
            pl.BlockSpec((1, d), lambda i, j: (0, 0)),
            pl.BlockSpec((None, d, TN_PROJ), lambda i, j: (l, 0, j)),
        ],
        out_specs=pl.BlockSpec((TM_PROJ, TN_PROJ), lambda i, j: (i, j)),
        out_shape=jax.ShapeDtypeStruct((t, n), BF16),
        scratch_shapes=[pltpu.VMEM((TM_PROJ, d), BF16)],
        compiler_params=_cparams(("parallel", "arbitrary")),
        name="inproj",
    )(x, g, w)


def _band_mask(rows, width):
    qi = lax.broadcasted_iota(jnp.int32, (rows, width), 0) % BLOCK
    ki = lax.broadcasted_iota(jnp.int32, (rows, width), 1)
    return (ki >= qi) & (ki <= qi + BLOCK)


def _causal_mask(rows, width):
    qi = lax.broadcasted_iota(jnp.int32, (rows, width), 0) % BLOCK
    ki = lax.broadcasted_iota(jnp.int32, (rows, width), 1)
    return ki <= qi


def _softmax_pv(s, mask, v, sink=None):
    s = jnp.where(mask, s, -jnp.inf)
    m = jnp.max(s, axis=-1, keepdims=True)
    if sink is not None:
        m = jnp.maximum(m, sink)
    p = jnp.exp(s - m)
    den = jnp.sum(p, axis=-1, keepdims=True)
    if sink is not None:
        den = den + jnp.exp(sink - m)
    o = _dot(p.astype(BF16), v)
    return o, m, den


def _dilated_kernel(q0_ref, q1_ref, q2_ref, k0_ref, k1_ref, k2_ref, v0_ref, v1_ref, v2_ref,
                    o_ref, qf_ref, kf_ref, vf_ref, og_ref, lg_ref):
    seq = o_ref.shape[0]
    lane = lax.broadcasted_iota(jnp.int32, (1, LANES), 1)
    lo = lane < HEAD_DIM
    band = _band_mask(2 * BLOCK, 2 * BLOCK)
    causal = _causal_mask(2 * BLOCK, BLOCK)

    def attend(q, k, v, mask):
        zero = jnp.zeros_like(q)
        qs = jnp.concatenate([jnp.where(lo, q, zero), jnp.where(lo, zero, q)], axis=0)
        s = _dot_nt(qs, k)
        o, m, den = _softmax_pv(s, mask, v)
        o = o * (1.0 / den)
        lse = m + jnp.log(den)
        o_pair = jnp.where(lo, o[:BLOCK], o[BLOCK:])
        l_pair = jnp.where(lo, lse[:BLOCK], lse[BLOCK:])
        return o_pair, l_pair

    for idx, (qr, kr, vr) in enumerate(((q1_ref, k1_ref, v1_ref), (q2_ref, k2_ref, v2_ref))):
        qf_ref[idx] = qr[...].astype(F32) * ATT_SCALE
        kf_ref[idx] = kr[...].astype(F32)
        vf_ref[idx] = vr[...].astype(F32)

    o_p, l_p = attend(q0_ref[pl.ds(0, BLOCK), :] * ATT_SCALE, k0_ref[pl.ds(0, BLOCK), :],
                      v0_ref[pl.ds(0, BLOCK), :], causal)
    og_ref[0, pl.ds(0, BLOCK), :] = o_p
    lg_ref[0, pl.ds(0, BLOCK), :] = l_p

    def body0(n, carry):
        r0 = pl.multiple_of(n * BLOCK, BLOCK)
        w0 = pl.multiple_of((n - 1) * BLOCK, BLOCK)
        o_p, l_p = attend(q0_ref[pl.ds(r0, BLOCK), :] * ATT_SCALE, k0_ref[pl.ds(w0, 2 * BLOCK), :],
                          v0_ref[pl.ds(w0, 2 * BLOCK), :], band)
        og_ref[0, pl.ds(r0, BLOCK), :] = o_p
        lg_ref[0, pl.ds(r0, BLOCK), :] = l_p
        return carry

    lax.fori_loop(1, seq // BLOCK, body0, 0)

    for idx, d in ((0, DILATIONS[1]), (1, DILATIONS[2])):
        g = idx + 1
        nblk = seq // d // BLOCK
        for r in range(d):
            for n in range(nblk):
                q = qf_ref[idx, pl.ds(r + n * BLOCK * d, BLOCK, stride=d), :].astype(BF16)
                if n == 0:
                    k = kf_ref[idx, pl.ds(r, BLOCK, stride=d), :].astype(BF16)
                    v = vf_ref[idx, pl.ds(r, BLOCK, stride=d), :].astype(BF16)
                    o_p, l_p = attend(q, k, v, causal)
                else:
                    w0 = r + (n - 1) * BLOCK * d
                    k = kf_ref[idx, pl.ds(w0, 2 * BLOCK, stride=d), :].astype(BF16)
                    v = vf_ref[idx, pl.ds(w0, 2 * BLOCK, stride=d), :].astype(BF16)
                    o_p, l_p = attend(q, k, v, band)
                og_ref[g, pl.ds(r + n * BLOCK * d, BLOCK, stride=d), :] = o_p
                lg_ref[g, pl.ds(r + n * BLOCK * d, BLOCK, stride=d), :] = l_p

    l0, l1, l2 = lg_ref[0], lg_ref[1], lg_ref[2]
    mx = jnp.maximum(jnp.maximum(l0, l1), l2)
    e0, e1, e2 = jnp.exp(l0 - mx), jnp.exp(l1 - mx), jnp.exp(l2 - mx)
    tot = e0 + e1 + e2
    o_ref[...] = ((e0 * og_ref[0] + e1 * og_ref[1] + e2 * og_ref[2]) / tot).astype(o_ref.dtype)


def _dilated_mixer(proj3):
    b, s, _ = proj3.shape
    base = OFF_A // LANES

    def spec(which, g):
        col = base + which * (N_DIL * A_WIDTH // LANES) + g * (A_WIDTH // LANES)
        return pl.BlockSpec((None, s, LANES), lambda i, hp, col=col: (i, 0, col + hp))

    in_specs = [spec(w, g) for w in range(3) for g in range(N_DIL)]
    return pl.pallas_call(
        _dilated_kernel,
        grid=(b, A_WIDTH // LANES),
        in_specs=in_specs,
        out_specs=pl.BlockSpec((None, s, LANES), lambda i, hp: (i, 0, hp)),
        out_shape=jax.ShapeDtypeStruct((b, s, A_WIDTH), BF16),
        scratch_shapes=[
            pltpu.VMEM((2, s, LANES), F32),
            pltpu.VMEM((2, s, LANES), F32),
            pltpu.VMEM((2, s, LANES), F32),
            pltpu.VMEM((N_DIL, s, LANES), F32),
            pltpu.VMEM((N_DIL, s, LANES), F32),
        ],
        compiler_params=_cparams(("parallel", "parallel")),
        name="dilated_mixer",
    )(*([proj3] * 9))


def _swa_kernel(sink_ref, q_ref, k_ref, v_ref, cos_ref, sin_ref, o_ref, qs_ref, ks_ref, vs_ref):
    seq = o_ref.shape[0]
    g = pl.program_id(1)
    rep = C_Q_HEADS // C_KV_HEADS
    lane = lax.broadcasted_iota(jnp.int32, (1, LANES), 1)
    lo = lane < HEAD_DIM
    first_half = (lane % HEAD_DIM) < (HEAD_DIM // 2)
    cos = cos_ref[...]
    sin = sin_ref[...]

    def rope(x):
        swapped = jnp.where(first_half, pltpu.roll(x, LANES - HEAD_DIM // 2, 1),
                            pltpu.roll(x, HEAD_DIM // 2, 1))
        return x * cos + swapped * sin

    mine = (lane // HEAD_DIM) == g
    kf = jnp.where(mine, rope(k_ref[...].astype(F32)), 0.0)
    ks_ref[...] = (kf + pltpu.roll(kf, HEAD_DIM, 1)).astype(BF16)
    vf = jnp.where(mine, v_ref[...].astype(F32), 0.0)
    vs_ref[...] = (vf + pltpu.roll(vf, HEAD_DIM, 1)).astype(BF16)
    for c in range(rep // 2):
        sl = slice(c * LANES, (c + 1) * LANES)
        qs_ref[:, sl] = (rope(q_ref[:, sl].astype(F32)) * ATT_SCALE).astype(BF16)

    rows = rep * BLOCK
    hrow = lax.broadcasted_iota(jnp.int32, (rows, 1), 0) // BLOCK
    sink = jnp.zeros((rows, 1), F32)
    for h in range(rep):
        sink = jnp.where(hrow == h, sink_ref[g * rep + h], sink)
    band = _band_mask(rows, 2 * BLOCK)
    causal = _causal_mask(rows, BLOCK)

    def block(r0, k, v, mask):
        parts = []
        for h in range(rep):
            q = qs_ref[pl.ds(r0, BLOCK), (h // 2) * LANES:(h // 2 + 1) * LANES]
            zero = jnp.zeros_like(q)
            parts.append(jnp.where(lo, q, zero) if h % 2 == 0 else jnp.where(lo, zero, q))
        s = _dot_nt(jnp.concatenate(parts, axis=0), k)
        o, _, den = _softmax_pv(s, mask, v, sink)
        o = o * (1.0 / den)
        for c in range(rep // 2):
            pair = jnp.where(lo, o[2 * c * BLOCK:(2 * c + 1) * BLOCK], o[(2 * c + 1) * BLOCK:(2 * c + 2) * BLOCK])
            o_ref[pl.ds(r0, BLOCK), c * LANES:(c + 1) * LANES] = pair.astype(o_ref.dtype)

    block(0, ks_ref[pl.ds(0, BLOCK), :], vs_ref[pl.ds(0, BLOCK), :], causal)

    def body(n, carry):
        r0 = pl.multiple_of(n * BLOCK, BLOCK)
        w0 = pl.multiple_of((n - 1) * BLOCK, BLOCK)
        block(r0, ks_ref[pl.ds(w0, 2 * BLOCK), :], vs_ref[pl.ds(w0, 2 * BLOCK), :], band)
        return carry

    lax.fori_loop(1, seq // BLOCK, body, 0)


def _rope_tables(seq):
    inv = ROPE_THETA ** (-jnp.arange(0, HEAD_DIM, 2, dtype=F32) / HEAD_DIM)
    ang = jnp.arange(seq, dtype=F32)[:, None] * inv[None, :]
    reps = LANES // (HEAD_DIM // 2)
    cos = jnp.tile(jnp.cos(ang), (1, reps))
    sign = jnp.where((jnp.arange(LANES) % HEAD_DIM) < HEAD_DIM // 2, -1.0, 1.0).astype(F32)
    sin = jnp.tile(jnp.sin(ang), (1, reps)) * sign[None, :]
    return cos, sin


def _swa_mixer(proj3, sinks, cos, sin):
    b, s, _ = proj3.shape
    qw = C_WIDTH // C_KV_HEADS
    return pl.pallas_call(
        _swa_kernel,
        grid=(b, C_KV_HEADS),
        in_specs=[
            pl.BlockSpec(memory_space=pltpu.SMEM),
            pl.BlockSpec((None, s, qw), lambda i, g: (i, 0, OFF_Q // qw + g)),
            pl.BlockSpec((None, s, LANES), lambda i, g: (i, 0, OFF_K // LANES)),
            pl.BlockSpec((None, s, LANES), lambda i, g: (i, 0, OFF_V // LANES)),
            pl.BlockSpec((s, LANES), lambda i, g: (0, 0)),
            pl.BlockSpec((s, LANES), lambda i, g: (0, 0)),
        ],
        out_specs=pl.BlockSpec((None, s, qw), lambda i, g: (i, 0, g)),
        out_shape=jax.ShapeDtypeStruct((b, s, C_WIDTH), BF16),
        scratch_shapes=[
            pltpu.VMEM((s, qw), BF16),
            pltpu.VMEM((s, LANES), BF16),
            pltpu.VMEM((s, LANES), BF16),
        ],
        compiler_params=_cparams(("parallel", "parallel")),
        name="swa_mixer",
    )(sinks, proj3, proj3, proj3, cos, sin)


def _cmul(ar, ai, br, bi):
    return ar * br - ai * bi, ar * bi + ai * br


def _s5_kernel(u_ref, lr_ref, li_ref, ldt_ref, bre_ref, bim_ref, cre_ref, cim_ref, d_ref, wg_ref, bg_ref,
               o_ref, hr_ref, hi_ref, pw_ref, car_ref):
    ts = u_ref.shape[0]

    @pl.when(pl.program_id(1) == 0)
    def _():
        lr, li = lr_ref[...], li_ref[...]
        dt = jnp.exp(ldt_ref[...])
        mag = jnp.exp(lr * dt)
        a_re, a_im = mag * jnp.cos(li * dt), mag * jnp.sin(li * dt)
        nr, ni = a_re - 1.0, a_im
        den = lr * lr + li * li
        z_re = (nr * lr + ni * li) / den
        z_im = (ni * lr - nr * li) / den
        row = lax.broadcasted_iota(jnp.int32, (SUBLANES, 1), 0)
        pows = [(a_re, a_im)]
        for _ in range(SUBLANES - 1):
            pows.append(_cmul(pows[-1][0], pows[-1][1], a_re, a_im))
        for j, sft in enumerate((1, 2, 4)):
            pr, pi = pows[sft - 1]
            pw_ref[2 * j] = jnp.where(row >= sft, pr, 0.0)
            pw_ref[2 * j + 1] = jnp.where(row >= sft, pi, 0.0)
        cr = jnp.zeros((SUBLANES, S5_N), F32)
        ci = jnp.zeros((SUBLANES, S5_N), F32)
        for i in range(SUBLANES):
            cr = jnp.where(row == i, pows[i][0], cr)
            ci = jnp.where(row == i, pows[i][1], ci)
        pw_ref[6] = cr
        pw_ref[7] = ci
        pw_ref[8] = jnp.broadcast_to(z_re, (SUBLANES, S5_N))
        pw_ref[9] = jnp.broadcast_to(z_im, (SUBLANES, S5_N))
        car_ref[...] = jnp.zeros_like(car_ref)

    u = u_ref[...]
    bu_re = _dot(u, bre_ref[...])
    bu_im = _dot(u, bim_ref[...])
    z_re = pw_ref[8, 0:1, :]
    z_im = pw_ref[9, 0:1, :]
    hr_ref[...] = z_re * bu_re - z_im * bu_im
    hi_ref[...] = z_re * bu_im + z_im * bu_re

    mults = [(pw_ref[2 * j], pw_ref[2 * j + 1]) for j in range(3)]
    acr, aci = pw_ref[6], pw_ref[7]

    def chunk(k, carry):
        cr, ci = carry
        r0 = pl.multiple_of(k * SUBLANES, SUBLANES)
        xr = hr_ref[pl.ds(r0, SUBLANES), :]
        xi = hi_ref[pl.ds(r0, SUBLANES), :]
        for (mr, mi), sft in zip(mults, (1, 2, 4)):
            sr = pltpu.roll(xr, sft, 0)
            si = pltpu.roll(xi, sft, 0)
            xr, xi = xr + (mr * sr - mi * si), xi + (mr * si + mi * sr)
        xr = xr + (acr * cr - aci * ci)
        xi = xi + (acr * ci + aci * cr)
        hr_ref[pl.ds(r0, SUBLANES), :] = xr
        hi_ref[pl.ds(r0, SUBLANES), :] = xi
        return xr[SUBLANES - 1:SUBLANES, :], xi[SUBLANES - 1:SUBLANES, :]

    cr, ci = lax.fori_loop(0, ts // SUBLANES, chunk, (car_ref[0, 0:1, :], car_ref[1, 0:1, :]))
    car_ref[0] = jnp.broadcast_to(cr, (SUBLANES, S5_N))
    car_ref[1] = jnp.broadcast_to(ci, (SUBLANES, S5_N))

    y = _dot(hr_ref[...].astype(BF16), cre_ref[...]) - _dot(hi_ref[...].astype(BF16), cim_ref[...])
    y = y + d_ref[...] * u.astype(F32)
    gl = jax.nn.gelu(y)
    out = gl * jax.nn.sigmoid(_dot(gl.astype(BF16), wg_ref[...]) + bg_ref[...])
    o_ref[...] = out.astype(o_ref.dtype)


def _s5_mixer(proj3, lr, li, ldt, bre, bim, cre, cim, dsk, wglu, bglu):
    b, s, _ = proj3.shape
    full = lambda shape: pl.BlockSpec(shape, lambda i, t: (0,) * len(shape))
    return pl.pallas_call(
        _s5_kernel,
        grid=(b, s // TS_S5),
        in_specs=[
            pl.BlockSpec((None, TS_S5, S5_WIDTH), lambda i, t: (i, t, OFF_B // S5_WIDTH)),
            full((1, S5_N)), full((1, S5_N)), full((1, S5_N)),
            full((S5_WIDTH, S5_N)), full((S5_WIDTH, S5_N)),
            full((S5_N, S5_WIDTH)), full((S5_N, S5_WIDTH)),
            full((1, S5_WIDTH)), full((S5_WIDTH, S5_WIDTH)), full((1, S5_WIDTH)),
        ],
        out_specs=pl.BlockSpec((None, TS_S5, S5_WIDTH), lambda i, t: (i, t, 0)),
        out_shape=jax.ShapeDtypeStruct((b, s, S5_WIDTH), BF16),
        scratch_shapes=[
            pltpu.VMEM((TS_S5, S5_N), F32),
            pltpu.VMEM((TS_S5, S5_N), F32),
            pltpu.VMEM((10, SUBLANES, S5_N), F32),
            pltpu.VMEM((2, SUBLANES, S5_N), F32),
        ],
        compiler_params=_cparams(("parallel", "arbitrary")),
        name="s5_mixer",
    )(proj3, lr, li, ldt, bre, bim, cre, cim, dsk, wglu, bglu)


def _merge_kernel(x_ref, oa_ref, ob_ref, oc_ref, ga_ref, gb_ref, gc_ref, wa_ref, wb_ref, wc_ref, wo_ref,
                  gn_ref, *rest, with_router):
    if with_router:
        rt_ref, xo_ref, h_ref, lg_ref = rest
    else:
        xo_ref, h_ref = rest

    def sig(r):
        return jax.nn.sigmoid(r[...].astype(F32))

    merged = (sig(ga_ref) * _dot(oa_ref[...], wa_ref[...])
              + sig(gb_ref) * _dot(ob_ref[...], wb_ref[...])
              + sig(gc_ref) * _dot(oc_ref[...], wc_ref[...]))
    xn = x_ref[...] + _dot(merged.astype(BF16), wo_ref[...])
    xo_ref[...] = xn
    h = _rms(xn, gn_ref[...])
    h_ref[...] = h.astype(BF16)
    if with_router:
        lg_ref[...] = lax.dot_general(rt_ref[...], h, (((1,), (1,)), ((), ())),
                                      precision=lax.Precision.HIGHEST, preferred_element_type=F32)


def _merge(x, oa, ob, oc, proj, wa, wb, wc, wo, gn, router_t=None):
    t, d = x.shape
    tm = TM_MERGE
    with_router = router_t is not None
    row = lambda w: pl.BlockSpec((tm, w), lambda i: (i, 0))
    full = lambda a: pl.BlockSpec(a.shape, lambda i: (0, 0))
    in_specs = [row(d), row(A_WIDTH), row(S5_WIDTH), row(C_WIDTH)]
    in_specs += [pl.BlockSpec((tm, d), lambda i, c=c: (i, c)) for c in range(3)]
    in_specs += [full(wa), full(wb), full(wc), full(wo), full(gn)]
    args = [x, oa, ob, oc, proj, proj, proj, wa, wb, wc, wo, gn]
    out_specs = [row(d), row(d)]
    out_shape = [jax.ShapeDtypeStruct((t, d), F32), jax.ShapeDtypeStruct((t, d), BF16)]
    if with_router:
        in_specs.append(full(router_t))
        args.append(router_t)
        out_specs.append(pl.BlockSpec((N_EXPERTS, tm), lambda i: (0, i)))
        out_shape.append(jax.ShapeDtypeStruct((N_EXPERTS, t), F32))
    return pl.pallas_call(
        functools.partial(_merge_kernel, with_router=with_router),
        grid=(t // tm,),
        in_specs=in_specs,
        out_specs=out_specs,
        out_shape=out_shape,
        compiler_params=_cparams(("parallel",)),
        name="merge_router" if with_router else "merge",
    )(*args)


def _ffn_kernel(x_ref, h_ref, wg_ref, wu_ref, wd_ref, o_ref):
    @pl.when(pl.program_id(1) == 0)
    def _():
        o_ref[...] = x_ref[...]

    h = h_ref[...]
    act = jax.nn.silu(_dot(h, wg_ref[...])) * _dot(h, wu_ref[...])
    o_ref[...] += _dot(act.astype(BF16), wd_ref[...])


def _ffn(x, h, wg, wu, wd, li):
    t, d = x.shape
    ff = wg.shape[2]
    return pl.pallas_call(
        _ffn_kernel,
        grid=(t // TM_FFN, ff // TF_FFN),
        in_specs=[
            pl.BlockSpec((TM_FFN, d), lambda i, f: (i, 0)),
            pl.BlockSpec((TM_FFN, d), lambda i, f: (i, 0)),
            pl.BlockSpec((None, d, TF_FFN), lambda i, f: (li, 0, f)),
            pl.BlockSpec((None, d, TF_FFN), lambda i, f: (li, 0, f)),
            pl.BlockSpec((None, TF_FFN, d), lambda i, f: (li, f, 0)),
        ],
        out_specs=pl.BlockSpec((TM_FFN, d), lambda i, f: (i, 0)),
        out_shape=jax.ShapeDtypeStruct((t, d), F32),
        compiler_params=_cparams(("parallel", "arbitrary")),
        name="dense_ffn",
    )(x, h, wg, wu, wd)


def _route_kernel(lg_ref, gate_ref, rank_ref, cnt_ref):
    lg = lg_ref[...]
    tm = lg.shape[1]
    eidx = lax.broadcasted_iota(jnp.int32, lg.shape, 0)
    m1 = jnp.max(lg, axis=0, keepdims=True)
    i1 = jnp.min(jnp.where(lg == m1, eidx, N_EXPERTS), axis=0, keepdims=True)
    rest = jnp.where(eidx == i1, -jnp.inf, lg)
    m2 = jnp.max(rest, axis=0, keepdims=True)
    i2 = jnp.min(jnp.where(rest == m2, eidx, N_EXPERTS), axis=0, keepdims=True)
    e2 = jnp.exp(m2 - m1)
    tot = 1.0 + e2
    sel1 = eidx == i1
    sel2 = eidx == i2
    gate_ref[...] = jnp.where(sel1, 1.0 / tot, jnp.where(sel2, e2 / tot, 0.0))
    sel = (sel1 | sel2).astype(jnp.int32)
    lane = lax.broadcasted_iota(jnp.int32, lg.shape, 1)
    c = sel
    sft = 1
    while sft < tm:
        c = c + jnp.where(lane >= sft, pltpu.roll(c, sft, 1), 0)
        sft *= 2
    rank_ref[...] = jnp.where(sel > 0, c - 1, -1)
    cnt_ref[...] = jnp.broadcast_to(jnp.sum(sel, axis=1, keepdims=True), cnt_ref.shape)


def _route(logits_t, tm):
    e, t = logits_t.shape
    nt = t // tm
    return pl.pallas_call(
        _route_kernel,
        grid=(nt,),
        in_specs=[pl.BlockSpec((e, tm), lambda i: (0, i))],
        out_specs=[
            pl.BlockSpec((e, tm), lambda i: (0, i)),
            pl.BlockSpec((e, tm), lambda i: (0, i)),
            pl.BlockSpec((None, e, LANES), lambda i: (i, 0, 0)),
        ],
        out_shape=[
            jax.ShapeDtypeStruct((e, t), F32),
            jax.ShapeDtypeStruct((e, t), jnp.int32),
            jax.ShapeDtypeStruct((nt, e, LANES), jnp.int32),
        ],
        compiler_params=_cparams(("parallel",)),
        name="route_top2",
    )(logits_t)


def _moe_kernel(cnt_ref, x_ref, h_ref, rrow_ref, rcol_ref, gcol_ref, wg_ref, wu_ref, wd_ref,
                o_ref, hc_ref, y_ref):
    i, e, f = pl.program_id(0), pl.program_id(1), pl.program_id(2)
    nf = pl.num_programs(2)
    tm = h_ref.shape[0]
    n_sel = cnt_ref[i * N_EXPERTS + e]
    n_over = (jnp.maximum(n_sel - M0_MOE, 0) + (CH_MOE - 1)) // CH_MOE

    def expert_part(hc):
        act = jax.nn.silu(_dot(hc, wg_ref[...])) * _dot(hc, wu_ref[...])
        return _dot(act.astype(BF16), wd_ref[...])

    def compact(r0, rows):
        slot = lax.broadcasted_iota(jnp.int32, (rows, tm), 0) + r0
        onehot = jnp.where(slot == rrow_ref[...], 1.0, 0.0).astype(BF16)
        return _dot(onehot, h_ref[...]).astype(BF16)

    def scatter_add(r0, y):
        rows = y.shape[0]
        yb = y.astype(BF16)
        for q in range(tm // SC_MOE):
            sl = pl.ds(q * SC_MOE, SC_MOE)
            slot = lax.broadcasted_iota(jnp.int32, (SC_MOE, rows), 1) + r0
            onehot = jnp.where(slot == rcol_ref[sl, :], 1.0, 0.0).astype(BF16)
            o_ref[sl, :] += gcol_ref[sl, :] * _dot(onehot, yb)

    @pl.when((e == 0) & (f == 0))
    def _():
        o_ref[...] = x_ref[...]

    @pl.when(f == 0)
    def _():
        hc_ref[...] = compact(0, M0_MOE)

    part = expert_part(hc_ref[...])

    @pl.when(f == 0)
    def _():
        y_ref[...] = part

    @pl.when(f > 0)
    def _():
        y_ref[...] += part

    @pl.when(f == nf - 1)
    def _():
        scatter_add(0, y_ref[...])

    def overflow(c, carry):
        r0 = M0_MOE + c * CH_MOE
        scatter_add(r0, expert_part(compact(r0, CH_MOE)))
        return carry

    lax.fori_loop(0, n_over, overflow, 0)


def _moe(x, h, counts, rank_row, rank_col, gate_col, wg, wu, wd, li, tm):
    t, d = x.shape
    _, ne, _, ff = wg.shape
    nt = t // tm
    once = pl.Buffered(1)
    grid_spec = pltpu.PrefetchScalarGridSpec(
        num_scalar_prefetch=1,
        grid=(nt, ne, ff // TF_FFN),
        in_specs=[
            pl.BlockSpec((tm, d), lambda i, e, f, c: (i, 0), pipeline_mode=once),
            pl.BlockSpec((tm, d), lambda i, e, f, c: (i, 0), pipeline_mode=once),
            pl.BlockSpec((None, 1, tm), lambda i, e, f, c: (e, 0, i)),
            pl.BlockSpec((None, tm, 1), lambda i, e, f, c: (e, i, 0)),
            pl.BlockSpec((None, tm, 1), lambda i, e, f, c: (e, i, 0)),
            pl.BlockSpec((None, None, d, TF_FFN), lambda i, e, f, c: (li, e, 0, f)),
            pl.BlockSpec((None, None, d, TF_FFN), lambda i, e, f, c: (li, e, 0, f)),
            pl.BlockSpec((None, None, TF_FFN, d), lambda i, e, f, c: (li, e, f, 0)),
        ],
        out_specs=pl.BlockSpec((tm, d), lambda i, e, f, c: (i, 0)),
        scratch_shapes=[pltpu.VMEM((M0_MOE, d), BF16), pltpu.VMEM((M0_MOE, d), F32)],
    )
    return pl.pallas_call(
        _moe_kernel,
        grid_spec=grid_spec,
        out_shape=jax.ShapeDtypeStruct((t, d), F32),
        compiler_params=_cparams(("parallel", "arbitrary", "arbitrary")),
        name="moe_experts",
    )(counts, x, h, rank_row, rank_col, gate_col, wg, wu, wd)


def _final_norm_kernel(x_ref, g_ref, o_ref):
    o_ref[...] = _rms(x_ref[...], g_ref[...])


def _final_norm(x, g):
    t, d = x.shape
    tm = TM_PROJ
    return pl.pallas_call(
        _final_norm_kernel,
        grid=(t // tm,),
        in_specs=[pl.BlockSpec((tm, d), lambda i: (i, 0)), pl.BlockSpec((1, d), lambda i: (0, 0))],
        out_specs=pl.BlockSpec((tm, d), lambda i: (i, 0)),
        out_shape=jax.ShapeDtypeStruct((t, d), F32),
        compiler_params=_cparams(("parallel",)),
        name="final_norm",
    )(x, g)


def _block_diag(blocks):
    l, g, r, c = blocks.shape
    on_diag = jnp.eye(g, dtype=bool)[None, :, None, :, None]
    out = jnp.where(on_diag, blocks[:, :, :, None, :], jnp.zeros((), blocks.dtype))
    return out.reshape(l, g * r, g * c)


def kernel(x, norm_mix, w_in, s5_lambda_re, s5_lambda_im, s5_log_dt, s5_b_re, s5_b_im, s5_c_re, s5_c_im,
           s5_d, s5_w_glu, s5_b_glu, c_sinks, w_branch_a, w_branch_b, w_branch_c, w_out, norm_ffn,
           ffn_w_gate, ffn_w_up, ffn_w_down, moe_router, moe_w_gate, moe_w_up, moe_w_down, norm_final):
    b, s, d = x.shape
    depth = w_in.shape[0]
    t = b * s
    n_split = A_QKV + S5_WIDTH + C_WIDTH + 2 * C_KV_HEADS * HEAD_DIM

    w_in_p = jnp.concatenate([w_in[:, :, n_split:], w_in[:, :, :n_split]], axis=-1).astype(BF16)
    wa, wb, wc, wo = (w.astype(BF16) for w in (w_branch_a, w_branch_b, w_branch_c, w_out))
    fg, fu, fd = (w.astype(BF16) for w in (ffn_w_gate, ffn_w_up, ffn_w_down))
    mg, mu, md = (w.astype(BF16) for w in (moe_w_gate, moe_w_up, moe_w_down))
    router_t = jnp.swapaxes(moe_router, 1, 2)
    lam_re = s5_lambda_re.reshape(depth, 1, S5_N)
    lam_im = s5_lambda_im.reshape(depth, 1, S5_N)
    log_dt = jnp.repeat(s5_log_dt, S5_STATE, axis=-1).reshape(depth, 1, S5_N)
    bre = _block_diag(jnp.swapaxes(s5_b_re, 2, 3)).astype(BF16)
    bim = _block_diag(jnp.swapaxes(s5_b_im, 2, 3)).astype(BF16)
    cre = _block_diag(jnp.swapaxes(s5_c_re, 2, 3)).astype(BF16)
    cim = _block_diag(jnp.swapaxes(s5_c_im, 2, 3)).astype(BF16)
    wglu = s5_w_glu.astype(BF16)
    cos, sin = _rope_tables(s)

    xt = x.reshape(t, d)
    for l in range(depth):
        proj = _inproj(xt, norm_mix[l][None, :], w_in_p, l)
        proj3 = proj.reshape(b, s, IN_COLS)
        o_a = _dilated_mixer(proj3).reshape(t, A_WIDTH)
        o_b = _s5_mixer(proj3, lam_re[l], lam_im[l], log_dt[l], bre[l], bim[l], cre[l], cim[l],
                        s5_d[l][None, :], wglu[l], s5_b_glu[l][None, :]).reshape(t, S5_WIDTH)
        o_c = _swa_mixer(proj3, c_sinks[l], cos, sin).reshape(t, C_WIDTH)
        i = l // 2
        if l % 2 == 0:
            xt, h = _merge(xt, o_a, o_b, o_c, proj, wa[l], wb[l], wc[l], wo[l], norm_ffn[l][None, :])
            xt = _ffn(xt, h, fg, fu, fd, i)
        else:
            xt, h, logits_t = _merge(xt, o_a, o_b, o_c, proj, wa[l], wb[l], wc[l], wo[l],
                                     norm_ffn[l][None, :], router_t[i])
            gate, rank, cnt = _route(logits_t, TM_MOE)
            counts = cnt[:, :, 0].reshape(-1)
            xt = _moe(xt, h, counts, rank[:, None, :], rank[:, :, None], gate[:, :, None],
                      mg, mu, md, i, TM_MOE)
    return _final_norm(xt, norm_final[None, :]).reshape(b, s, d)
```

```python
import functools
import math

import jax
import jax.numpy as jnp
import numpy as np
from jax import lax
from jax.experimental import pallas as pl
from jax.experimental.pallas import tpu as pltpu

F32 = jnp.float32
BF16 = jnp.bfloat16

D_MODEL = 1024
HEAD_DIM = 64
BLOCK = 128
LANES = 128
SUBLANES = 8
DILATIONS = (1, 4, 16)
N_DIL = 3
A_HEADS = 4
A_WIDTH = A_HEADS * HEAD_DIM
A_QKV = 3 * N_DIL * A_WIDTH
S5_WIDTH = 256
S5_GROUPS = 16
S5_GROUP_CH = 16
S5_STATE = 64
S5_N = S5_GROUPS * S5_STATE
C_Q_HEADS = 8
C_KV_HEADS = 2
C_WIDTH = C_Q_HEADS * HEAD_DIM
ROPE_THETA = 150000.0
N_GATE = 3 * D_MODEL
IN_COLS = 6400
D_FF = 3584
N_EXPERTS = 8
RMS_EPS = 1e-6
ATT_SCALE = HEAD_DIM ** -0.5

OFF_GATE = 0
OFF_A = N_GATE
OFF_B = OFF_A + A_QKV
OFF_Q = OFF_B + S5_WIDTH
OFF_K = OFF_Q + C_WIDTH
OFF_V = OFF_K + C_KV_HEADS * HEAD_DIM

TM_PROJ = 1024
TN_PROJ = 1280
TM_MERGE = 512
TM_FFN = 1024
TF_FFN = 512
TS_S5 = 512
TM_MOE = 2048
TF_MOE = 512
CH_MOE = 128
KMIN_MOE = 3
KMAX_MOE = 6
SC_MOE = 512
VMEM_LIMIT = 56 * 1024 * 1024


def _cparams(sem):
    return pltpu.CompilerParams(dimension_semantics=sem, vmem_limit_bytes=VMEM_LIMIT)


def _dot(a, b):
    return jnp.dot(a, b, preferred_element_type=F32)


def _dot_nt(a, b):
    return lax.dot_general(a, b, (((1,), (1,)), ((), ())), preferred_element_type=F32)


def _rms(x, g):
    return x * lax.rsqrt(jnp.mean(x * x, axis=-1, keepdims=True) + RMS_EPS) * g


def _inproj_kernel(x_ref, g_ref, w_ref, o_ref, h_ref):
    @pl.when(pl.program_id(1) == 0)
    def _():
        h_ref[...] = _rms(x_ref[...], g_ref[...]).astype(BF16)

    o_ref[...] = _dot(h_ref[...], w_ref[...]).astype(o_ref.dtype)


def _inproj(x, g, w, l):
    t, d = x.shape
    n = w.shape[2]
    return pl.pallas_call(
        _inproj_kernel,
        grid=(t // TM_PROJ, n // TN_PROJ),
        in_specs=[
            pl.BlockSpec((TM_PROJ, d), lambda i, j: (i, 0)),
            pl.BlockSpec((1, d), lambda i, j: (0, 0)),
            pl.BlockSpec((None, d, TN_PROJ), lambda i, j: (l, 0, j)),
        ],
        out_specs=pl.BlockSpec((TM_PROJ, TN_PROJ), lambda i, j: (i, j)),
        out_shape=jax.ShapeDtypeStruct((t, n), BF16),
        scratch_shapes=[pltpu.VMEM((TM_PROJ, d), BF16)],
        compiler_params=_cparams(("parallel", "arbitrary")),
        name="inproj",
    )(x, g, w)


def _band_mask(rows, width):
    qi = lax.broadcasted_iota(jnp.int32, (rows, width), 0) % BLOCK
    ki = lax.broadcasted_iota(jnp.int32, (rows, width), 1)
    return (ki >= qi) & (ki <= qi + BLOCK)


def _causal_mask(rows, width):
    qi = lax.broadcasted_iota(jnp.int32, (rows, width), 0) % BLOCK
    ki = lax.broadcasted_iota(jnp.int32, (rows, width), 1)
    return ki <= qi


def _softmax_pv(s, mask, v, sink=None):
    s = jnp.where(mask, s, -jnp.inf)
    m = jnp.max(s, axis=-1, keepdims=True)
    if sink is not None:
        m = jnp.maximum(m, sink)
    p = jnp.exp(s - m)
    den = jnp.sum(p, axis=-1, keepdims=True)
    if sink is not None:
        den = den + jnp.exp(sink - m)
    o = _dot(p.astype(BF16), v)
    return o, m, den


def _dilated_kernel(q0_ref, q1_ref, q2_ref, k0_ref, k1_ref, k2_ref, v0_ref, v1_ref, v2_ref,
                    o_ref, qf_ref, kf_ref, vf_ref, og_ref, lg_ref):
    seq = o_ref.shape[0]
    lane = lax.broadcasted_iota(jnp.int32, (1, LANES), 1)
    lo = lane < HEAD_DIM
    band = _band_mask(2 * BLOCK, 2 * BLOCK)
    causal = _causal_mask(2 * BLOCK, BLOCK)

    def attend(q, k, v, mask):
        zero = jnp.zeros_like(q)
        qs = jnp.concatenate([jnp.where(lo, q, zero), jnp.where(lo, zero, q)], axis=0)
        s = _dot_nt(qs, k)
        o, m, den = _softmax_pv(s, mask, v)
        o = o * (1.0 / den)
        lse = m + jnp.log(den)
        o_pair = jnp.where(lo, o[:BLOCK], o[BLOCK:])
        l_pair = jnp.where(lo, lse[:BLOCK], lse[BLOCK:])
        return o_pair, l_pair

    for idx, (qr, kr, vr) in enumerate(((q1_ref, k1_ref, v1_ref), (q2_ref, k2_ref, v2_ref))):
        qf_ref[idx] = qr[...].astype(F32) * ATT_SCALE
        kf_ref[idx] = kr[...].astype(F32)
        vf_ref[idx] = vr[...].astype(F32)

    o_p, l_p = attend(q0_ref[pl.ds(0, BLOCK), :] * ATT_SCALE, k0_ref[pl.ds(0, BLOCK), :],
                      v0_ref[pl.ds(0, BLOCK), :], causal)
    og_ref[0, pl.ds(0, BLOCK), :] = o_p
    lg_ref[0, pl.ds(0, BLOCK), :] = l_p

    for n in range(1, seq // BLOCK):
        r0, w0 = n * BLOCK, (n - 1) * BLOCK
        o_p, l_p = attend(q0_ref[pl.ds(r0, BLOCK), :] * ATT_SCALE, k0_ref[pl.ds(w0, 2 * BLOCK), :],
                          v0_ref[pl.ds(w0, 2 * BLOCK), :], band)
        og_ref[0, pl.ds(r0, BLOCK), :] = o_p
        lg_ref[0, pl.ds(r0, BLOCK), :] = l_p

    for idx, d in ((0, DILATIONS[1]), (1, DILATIONS[2])):
        g = idx + 1
        nblk = seq // d // BLOCK
        for r in range(d):
            for n in range(nblk):
                q = qf_ref[idx, pl.ds(r + n * BLOCK * d, BLOCK, stride=d), :].astype(BF16)
                if n == 0:
                    k = kf_ref[idx, pl.ds(r, BLOCK, stride=d), :].astype(BF16)
                    v = vf_ref[idx, pl.ds(r, BLOCK, stride=d), :].astype(BF16)
                    o_p, l_p = attend(q, k, v, causal)
                else:
                    w0 = r + (n - 1) * BLOCK * d
                    k = kf_ref[idx, pl.ds(w0, 2 * BLOCK, stride=d), :].astype(BF16)
                    v = vf_ref[idx, pl.ds(w0, 2 * BLOCK, stride=d), :].astype(BF16)
                    o_p, l_p = attend(q, k, v, band)
                og_ref[g, pl.ds(r + n * BLOCK * d, BLOCK, stride=d), :] = o_p
                lg_ref[g, pl.ds(r + n * BLOCK * d, BLOCK, stride=d), :] = l_p

    l0, l1, l2 = lg_ref[0], lg_ref[1], lg_ref[2]
    mx = jnp.maximum(jnp.maximum(l0, l1), l2)
    e0, e1, e2 = jnp.exp(l0 - mx), jnp.exp(l1 - mx), jnp.exp(l2 - mx)
    tot = e0 + e1 + e2
    o_ref[...] = ((e0 * og_ref[0] + e1 * og_ref[1] + e2 * og_ref[2]) / tot).astype(o_ref.dtype)


def _dilated_mixer(proj3):
    b, s, _ = proj3.shape
    base = OFF_A // LANES

    def spec(which, g):
        col = base + which * (N_DIL * A_WIDTH // LANES) + g * (A_WIDTH // LANES)
        return pl.BlockSpec((None, s, LANES), lambda i, hp, col=col: (i, 0, col + hp))

    in_specs = [spec(w, g) for w in range(3) for g in range(N_DIL)]
    return pl.pallas_call(
        _dilated_kernel,
        grid=(b, A_WIDTH // LANES),
        in_specs=in_specs,
        out_specs=pl.BlockSpec((None, s, LANES), lambda i, hp: (i, 0, hp)),
        out_shape=jax.ShapeDtypeStruct((b, s, A_WIDTH), BF16),
        scratch_shapes=[
            pltpu.VMEM((2, s, LANES), F32),
            pltpu.VMEM((2, s, LANES), F32),
            pltpu.VMEM((2, s, LANES), F32),
            pltpu.VMEM((N_DIL, s, LANES), F32),
            pltpu.VMEM((N_DIL, s, LANES), F32),
        ],
        compiler_params=_cparams(("parallel", "parallel")),
        name="dilated_mixer",
    )(*([proj3] * 9))


def _swa_kernel(sink_ref, q_ref, k_ref, v_ref, cos_ref, sin_ref, o_ref, qs_ref, ks_ref, vs_ref):
    seq = o_ref.shape[0]
    g = pl.program_id(1)
    rep = C_Q_HEADS // C_KV_HEADS
    lane = lax.broadcasted_iota(jnp.int32, (1, LANES), 1)
    lo = lane < HEAD_DIM
    first_half = (lane % HEAD_DIM) < (HEAD_DIM // 2)
    cos = cos_ref[...]
    sin = sin_ref[...]

    def rope(x):
        swapped = jnp.where(first_half, pltpu.roll(x, LANES - HEAD_DIM // 2, 1),
                            pltpu.roll(x, HEAD_DIM // 2, 1))
        return x * cos + swapped * sin

    mine = (lane // HEAD_DIM) == g
    kf = jnp.where(mine, rope(k_ref[...].astype(F32)), 0.0)
    ks_ref[...] = (kf + pltpu.roll(kf, HEAD_DIM, 1)).astype(BF16)
    vf = jnp.where(mine, v_ref[...].astype(F32), 0.0)
    vs_ref[...] = (vf + pltpu.roll(vf, HEAD_DIM, 1)).astype(BF16)
    for c in range(rep // 2):
        sl = slice(c * LANES, (c + 1) * LANES)
        qs_ref[:, sl] = (rope(q_ref[:, sl].astype(F32)) * ATT_SCALE).astype(BF16)

    rows = rep * BLOCK
    hrow = lax.broadcasted_iota(jnp.int32, (rows, 1), 0) // BLOCK
    sink = jnp.zeros((rows, 1), F32)
    for h in range(rep):
        sink = jnp.where(hrow == h, sink_ref[g * rep + h], sink)
    band = _band_mask(rows, 2 * BLOCK)
    causal = _causal_mask(rows, BLOCK)

    def block(r0, k, v, mask):
        parts = []
        for h in range(rep):
            q = qs_ref[pl.ds(r0, BLOCK), (h // 2) * LANES:(h // 2 + 1) * LANES]
            zero = jnp.zeros_like(q)
            parts.append(jnp.where(lo, q, zero) if h % 2 == 0 else jnp.where(lo, zero, q))
        s = _dot_nt(jnp.concatenate(parts, axis=0), k)
        o, _, den = _softmax_pv(s, mask, v, sink)
        o = o * (1.0 / den)
        for c in range(rep // 2):
            pair = jnp.where(lo, o[2 * c * BLOCK:(2 * c + 1) * BLOCK], o[(2 * c + 1) * BLOCK:(2 * c + 2) * BLOCK])
            o_ref[pl.ds(r0, BLOCK), c * LANES:(c + 1) * LANES] = pair.astype(o_ref.dtype)

    block(0, ks_ref[pl.ds(0, BLOCK), :], vs_ref[pl.ds(0, BLOCK), :], causal)

    def body(n, carry):
        r0 = pl.multiple_of(n * BLOCK, BLOCK)
        w0 = pl.multiple_of((n - 1) * BLOCK, BLOCK)
        block(r0, ks_ref[pl.ds(w0, 2 * BLOCK), :], vs_ref[pl.ds(w0, 2 * BLOCK), :], band)
        return carry

    lax.fori_loop(1, seq // BLOCK, body, 0, unroll=3)


def _rope_tables(seq):
    inv = ROPE_THETA ** (-jnp.arange(0, HEAD_DIM, 2, dtype=F32) / HEAD_DIM)
    ang = jnp.arange(seq, dtype=F32)[:, None] * inv[None, :]
    reps = LANES // (HEAD_DIM // 2)
    cos = jnp.tile(jnp.cos(ang), (1, reps))
    sign = jnp.where((jnp.arange(LANES) % HEAD_DIM) < HEAD_DIM // 2, -1.0, 1.0).astype(F32)
    sin = jnp.tile(jnp.sin(ang), (1, reps)) * sign[None, :]
    return cos, sin


def _swa_mixer(proj3, sinks, cos, sin):
    b, s, _ = proj3.shape
    qw = C_WIDTH // C_KV_HEADS
    return pl.pallas_call(
        _swa_kernel,
        grid=(b, C_KV_HEADS),
        in_specs=[
            pl.BlockSpec(memory_space=pltpu.SMEM),
            pl.BlockSpec((None, s, qw), lambda i, g: (i, 0, OFF_Q // qw + g)),
            pl.BlockSpec((None, s, LANES), lambda i, g: (i, 0, OFF_K // LANES)),
            pl.BlockSpec((None, s, LANES), lambda i, g: (i, 0, OFF_V // LANES)),
            pl.BlockSpec((s, LANES), lambda i, g: (0, 0)),
            pl.BlockSpec((s, LANES), lambda i, g: (0, 0)),
        ],
        out_specs=pl.BlockSpec((None, s, qw), lambda i, g: (i, 0, g)),
        out_shape=jax.ShapeDtypeStruct((b, s, C_WIDTH), BF16),
        scratch_shapes=[
            pltpu.VMEM((s, qw), BF16),
            pltpu.VMEM((s, LANES), BF16),
            pltpu.VMEM((s, LANES), BF16),
        ],
        compiler_params=_cparams(("parallel", "parallel")),
        name="swa_mixer",
    )(sinks, proj3, proj3, proj3, cos, sin)


def _cmul(ar, ai, br, bi):
    return ar * br - ai * bi, ar * bi + ai * br


def _s5_kernel(u_ref, lr_ref, li_ref, ldt_ref, bre_ref, bim_ref, cre_ref, cim_ref, d_ref, wg_ref, bg_ref,
               o_ref, hr_ref, hi_ref, pw_ref, car_ref):
    ts = u_ref.shape[0]

    @pl.when(pl.program_id(1) == 0)
    def _():
        lr, li = lr_ref[...], li_ref[...]
        dt = jnp.exp(ldt_ref[...])
        mag = jnp.exp(lr * dt)
        a_re, a_im = mag * jnp.cos(li * dt), mag * jnp.sin(li * dt)
        nr, ni = a_re - 1.0, a_im
        den = lr * lr + li * li
        z_re = (nr * lr + ni * li) / den
        z_im = (ni * lr - nr * li) / den
        row = lax.broadcasted_iota(jnp.int32, (SUBLANES, 1), 0)
        pows = [(a_re, a_im)]
        for _ in range(SUBLANES - 1):
            pows.append(_cmul(pows[-1][0], pows[-1][1], a_re, a_im))
        for j, sft in enumerate((1, 2, 4)):
            pr, pi = pows[sft - 1]
            pw_ref[2 * j] = jnp.where(row >= sft, pr, 0.0)
            pw_ref[2 * j + 1] = jnp.where(row >= sft, pi, 0.0)
        cr = jnp.zeros((SUBLANES, S5_N), F32)
        ci = jnp.zeros((SUBLANES, S5_N), F32)
        for i in range(SUBLANES):
            cr = jnp.where(row == i, pows[i][0], cr)
            ci = jnp.where(row == i, pows[i][1], ci)
        pw_ref[6] = cr
        pw_ref[7] = ci
        pw_ref[8] = jnp.broadcast_to(z_re, (SUBLANES, S5_N))
        pw_ref[9] = jnp.broadcast_to(z_im, (SUBLANES, S5_N))
        car_ref[...] = jnp.zeros_like(car_ref)

    u = u_ref[...]
    bu_re = _dot(u, bre_ref[...])
    bu_im = _dot(u, bim_ref[...])
    z_re = pw_ref[8, 0:1, :]
    z_im = pw_ref[9, 0:1, :]
    hr_ref[...] = z_re * bu_re - z_im * bu_im
    hi_ref[...] = z_re * bu_im + z_im * bu_re

    mults = [(pw_ref[2 * j], pw_ref[2 * j + 1]) for j in range(3)]
    acr, aci = pw_ref[6], pw_ref[7]

    def chunk(k, carry):
        cr, ci = carry
        r0 = pl.multiple_of(k * SUBLANES, SUBLANES)
        xr = hr_ref[pl.ds(r0, SUBLANES), :]
        xi = hi_ref[pl.ds(r0, SUBLANES), :]
        for (mr, mi), sft in zip(mults, (1, 2, 4)):
            sr = pltpu.roll(xr, sft, 0)
            si = pltpu.roll(xi, sft, 0)
            xr, xi = xr + (mr * sr - mi * si), xi + (mr * si + mi * sr)
        xr = xr + (acr * cr - aci * ci)
        xi = xi + (acr * ci + aci * cr)
        hr_ref[pl.ds(r0, SUBLANES), :] = xr
        hi_ref[pl.ds(r0, SUBLANES), :] = xi
        return xr[SUBLANES - 1:SUBLANES, :], xi[SUBLANES - 1:SUBLANES, :]

    cr, ci = lax.fori_loop(0, ts // SUBLANES, chunk, (car_ref[0, 0:1, :], car_ref[1, 0:1, :]))
    car_ref[0] = jnp.broadcast_to(cr, (SUBLANES, S5_N))
    car_ref[1] = jnp.broadcast_to(ci, (SUBLANES, S5_N))

    y = _dot(hr_ref[...].astype(BF16), cre_ref[...]) - _dot(hi_ref[...].astype(BF16), cim_ref[...])
    y = y + d_ref[...] * u.astype(F32)
    gl = jax.nn.gelu(y)
    out = gl * jax.nn.sigmoid(_dot(gl.astype(BF16), wg_ref[...]) + bg_ref[...])
    o_ref[...] = out.astype(o_ref.dtype)


def _s5_mixer(proj3, lr, li, ldt, bre, bim, cre, cim, dsk, wglu, bglu):
    b, s, _ = proj3.shape
    full = lambda shape: pl.BlockSpec(shape, lambda i, t: (0,) * len(shape))
    return pl.pallas_call(
        _s5_kernel,
        grid=(b, s // TS_S5),
        in_specs=[
            pl.BlockSpec((None, TS_S5, S5_WIDTH), lambda i, t: (i, t, OFF_B // S5_WIDTH)),
            full((1, S5_N)), full((1, S5_N)), full((1, S5_N)),
            full((S5_WIDTH, S5_N)), full((S5_WIDTH, S5_N)),
            full((S5_N, S5_WIDTH)), full((S5_N, S5_WIDTH)),
            full((1, S5_WIDTH)), full((S5_WIDTH, S5_WIDTH)), full((1, S5_WIDTH)),
        ],
        out_specs=pl.BlockSpec((None, TS_S5, S5_WIDTH), lambda i, t: (i, t, 0)),
        out_shape=jax.ShapeDtypeStruct((b, s, S5_WIDTH), BF16),
        scratch_shapes=[
            pltpu.VMEM((TS_S5, S5_N), F32),
            pltpu.VMEM((TS_S5, S5_N), F32),
            pltpu.VMEM((10, SUBLANES, S5_N), F32),
            pltpu.VMEM((2, SUBLANES, S5_N), F32),
        ],
        compiler_params=_cparams(("parallel", "arbitrary")),
        name="s5_mixer",
    )(proj3, lr, li, ldt, bre, bim, cre, cim, dsk, wglu, bglu)


def _merge_kernel(x_ref, oa_ref, ob_ref, oc_ref, ga_ref, gb_ref, gc_ref, wa_ref, wb_ref, wc_ref, wo_ref,
                  gn_ref, *rest, with_router):
    if with_router:
        rt_ref, xo_ref, h_ref, lg_ref = rest
    else:
        xo_ref, h_ref = rest

    def sig(r):
        return jax.nn.sigmoid(r[...].astype(F32))

    merged = (sig(ga_ref) * _dot(oa_ref[...], wa_ref[...])
              + sig(gb_ref) * _dot(ob_ref[...], wb_ref[...])
              + sig(gc_ref) * _dot(oc_ref[...], wc_ref[...]))
    xn = x_ref[...] + _dot(merged.astype(BF16), wo_ref[...])
    xo_ref[...] = xn
    h = _rms(xn, gn_ref[...])
    h_ref[...] = h.astype(BF16)
    if with_router:
        lg_ref[...] = lax.dot_general(rt_ref[...], h, (((1,), (1,)), ((), ())),
                                      precision=lax.Precision.HIGHEST, preferred_element_type=F32)


def _merge(x, oa, ob, oc, proj, wa, wb, wc, wo, gn, router_t=None):
    t, d = x.shape
    tm = TM_MERGE
    with_router = router_t is not None
    row = lambda w: pl.BlockSpec((tm, w), lambda i: (i, 0))
    full = lambda a: pl.BlockSpec(a.shape, lambda i: (0, 0))
    in_specs = [row(d), row(A_WIDTH), row(S5_WIDTH), row(C_WIDTH)]
    in_specs += [pl.BlockSpec((tm, d), lambda i, c=c: (i, c)) for c in range(3)]
    in_specs += [full(wa), full(wb), full(wc), full(wo), full(gn)]
    args = [x, oa, ob, oc, proj, proj, proj, wa, wb, wc, wo, gn]
    out_specs = [row(d), row(d)]
    out_shape = [jax.ShapeDtypeStruct((t, d), F32), jax.ShapeDtypeStruct((t, d), BF16)]
    if with_router:
        in_specs.append(full(router_t))
        args.append(router_t)
        out_specs.append(pl.BlockSpec((N_EXPERTS, tm), lambda i: (0, i)))
        out_shape.append(jax.ShapeDtypeStruct((N_EXPERTS, t), F32))
    return pl.pallas_call(
        functools.partial(_merge_kernel, with_router=with_router),
        grid=(t // tm,),
        in_specs=in_specs,
        out_specs=out_specs,
        out_shape=out_shape,
        compiler_params=_cparams(("parallel",)),
        name="merge_router" if with_router else "merge",
    )(*args)


def _ffn_kernel(x_ref, h_ref, wg_ref, wu_ref, wd_ref, o_ref):
    @pl.when(pl.program_id(1) == 0)
    def _():
        o_ref[...] = x_ref[...]

    h = h_ref[...]
    act = jax.nn.silu(_dot(h, wg_ref[...])) * _dot(h, wu_ref[...])
    o_ref[...] += _dot(act.astype(BF16), wd_ref[...])


def _ffn(x, h, wg, wu, wd, li):
    t, d = x.shape
    ff = wg.shape[2]
    return pl.pallas_call(
        _ffn_kernel,
        grid=(t // TM_FFN, ff // TF_FFN),
        in_specs=[
            pl.BlockSpec((TM_FFN, d), lambda i, f: (i, 0)),
            pl.BlockSpec((TM_FFN, d), lambda i, f: (i, 0)),
            pl.BlockSpec((None, d, TF_FFN), lambda i, f: (li, 0, f)),
            pl.BlockSpec((None, d, TF_FFN), lambda i, f: (li, 0, f)),
            pl.BlockSpec((None, TF_FFN, d), lambda i, f: (li, f, 0)),
        ],
        out_specs=pl.BlockSpec((TM_FFN, d), lambda i, f: (i, 0)),
        out_shape=jax.ShapeDtypeStruct((t, d), F32),
        compiler_params=_cparams(("parallel", "arbitrary")),
        name="dense_ffn",
    )(x, h, wg, wu, wd)


def _route_kernel(lg_ref, gate_ref, rank_ref, cnt_ref):
    lg = lg_ref[...]
    tm = lg.shape[1]
    eidx = lax.broadcasted_iota(jnp.int32, lg.shape, 0)
    m1 = jnp.max(lg, axis=0, keepdims=True)
    i1 = jnp.min(jnp.where(lg == m1, eidx, N_EXPERTS), axis=0, keepdims=True)
    rest = jnp.where(eidx == i1, -jnp.inf, lg)
    m2 = jnp.max(rest, axis=0, keepdims=True)
    i2 = jnp.min(jnp.where(rest == m2, eidx, N_EXPERTS), axis=0, keepdims=True)
    e2 = jnp.exp(m2 - m1)
    tot = 1.0 + e2
    sel1 = eidx == i1
    sel2 = eidx == i2
    gate_ref[...] = jnp.where(sel1, 1.0 / tot, jnp.where(sel2, e2 / tot, 0.0))
    sel = (sel1 | sel2).astype(jnp.int32)
    lane = lax.broadcasted_iota(jnp.int32, lg.shape, 1)
    c = sel
    sft = 1
    while sft < tm:
        c = c + jnp.where(lane >= sft, pltpu.roll(c, sft, 1), 0)
        sft *= 2
    rank_ref[...] = jnp.where(sel > 0, c - 1, -1)
    cnt_ref[...] = jnp.broadcast_to(jnp.sum(sel, axis=1, keepdims=True), cnt_ref.shape)


def _route(logits_t, tm):
    e, t = logits_t.shape
    nt = t // tm
    return pl.pallas_call(
        _route_kernel,
        grid=(nt,),
        in_specs=[pl.BlockSpec((e, tm), lambda i: (0, i))],
        out_specs=[
            pl.BlockSpec((e, tm), lambda i: (0, i)),
            pl.BlockSpec((e, tm), lambda i: (0, i)),
            pl.BlockSpec((None, e, LANES), lambda i: (i, 0, 0)),
        ],
        out_shape=[
            jax.ShapeDtypeStruct((e, t), F32),
            jax.ShapeDtypeStruct((e, t), jnp.int32),
            jax.ShapeDtypeStruct((nt, e, LANES), jnp.int32),
        ],
        compiler_params=_cparams(("parallel",)),
        name="route_top2",
    )(logits_t)


def _moe_kernel(cnt_ref, x_ref, h_ref, rrow_ref, rcol_ref, gcol_ref, wg_ref, wu_ref, wd_ref,
                o_ref, hc_ref, y_ref):
    i, e, f = pl.program_id(0), pl.program_id(1), pl.program_id(2)
    nf = pl.num_programs(2)
    tm = h_ref.shape[0]
    n_sel = cnt_ref[i * N_EXPERTS + e]
    n_chunks = (n_sel + (CH_MOE - 1)) // CH_MOE
    k_path = jnp.clip(n_chunks, KMIN_MOE, KMAX_MOE)
    n_over = jnp.maximum(n_chunks - KMAX_MOE, 0)

    def expert_part(hc):
        act = jax.nn.silu(_dot(hc, wg_ref[...])) * _dot(hc, wu_ref[...])
        return _dot(act.astype(BF16), wd_ref[...])

    def compact(r0, rows):
        slot = lax.broadcasted_iota(jnp.int32, (rows, tm), 0) + r0
        onehot = jnp.where(slot == rrow_ref[...], 1.0, 0.0).astype(BF16)
        return _dot(onehot, h_ref[...]).astype(BF16)

    def scatter_add(r0, y):
        rows = y.shape[0]
        yb = y.astype(BF16)
        for q in range(tm // SC_MOE):
            sl = pl.ds(q * SC_MOE, SC_MOE)
            slot = lax.broadcasted_iota(jnp.int32, (SC_MOE, rows), 1) + r0
            onehot = jnp.where(slot == rcol_ref[sl, :], 1.0, 0.0).astype(BF16)
            o_ref[sl, :] += gcol_ref[sl, :] * _dot(onehot, yb)

    @pl.when((e == 0) & (f == 0))
    def _():
        o_ref[...] = x_ref[...]

    def static_path(rows):
        sl = pl.ds(0, rows)

        @pl.when(f == 0)
        def _():
            hc_ref[sl, :] = compact(0, rows)
            y_ref[sl, :] = jnp.zeros((rows, D_MODEL), F32)

        y_ref[sl, :] += expert_part(hc_ref[sl, :])

        @pl.when(f == nf - 1)
        def _():
            scatter_add(0, y_ref[sl, :])

    for k in range(KMIN_MOE, KMAX_MOE + 1):
        pl.when(k_path == k)(functools.partial(static_path, k * CH_MOE))

    def overflow(c, carry):
        r0 = (KMAX_MOE + c) * CH_MOE
        scatter_add(r0, expert_part(compact(r0, CH_MOE)))
        return carry

    lax.fori_loop(0, n_over, overflow, 0)


def _moe(x, h, counts, rank_row, rank_col, gate_col, wg, wu, wd, li, tm):
    t, d = x.shape
    _, ne, _, ff = wg.shape
    nt = t // tm
    once = pl.Buffered(1)
    grid_spec = pltpu.PrefetchScalarGridSpec(
        num_scalar_prefetch=1,
        grid=(nt, ne, ff // TF_MOE),
        in_specs=[
            pl.BlockSpec((tm, d), lambda i, e, f, c: (i, 0), pipeline_mode=once),
            pl.BlockSpec((tm, d), lambda i, e, f, c: (i, 0), pipeline_mode=once),
            pl.BlockSpec((None, 1, tm), lambda i, e, f, c: (e, 0, i)),
            pl.BlockSpec((None, tm, 1), lambda i, e, f, c: (e, i, 0)),
            pl.BlockSpec((None, tm, 1), lambda i, e, f, c: (e, i, 0)),
            pl.BlockSpec((None, None, d, TF_MOE), lambda i, e, f, c: (li, e, 0, f)),
            pl.BlockSpec((None, None, d, TF_MOE), lambda i, e, f, c: (li, e, 0, f)),
            pl.BlockSpec((None, None, TF_MOE, d), lambda i, e, f, c: (li, e, f, 0)),
        ],
        out_specs=pl.BlockSpec((tm, d), lambda i, e, f, c: (i, 0)),
        scratch_shapes=[pltpu.VMEM((KMAX_MOE * CH_MOE, d), BF16), pltpu.VMEM((KMAX_MOE * CH_MOE, d), F32)],
    )
    return pl.pallas_call(
        _moe_kernel,
        grid_spec=grid_spec,
        out_shape=jax.ShapeDtypeStruct((t, d), F32),
        compiler_params=_cparams(("parallel", "arbitrary", "arbitrary")),
        name="moe_experts",
    )(counts, x, h, rank_row, rank_col, gate_col, wg, wu, wd)


def _final_norm_kernel(x_ref, g_ref, o_ref):
    o_ref[...] = _rms(x_ref[...], g_ref[...])


def _final_norm(x, g):
    t, d = x.shape
    tm = TM_PROJ
    return pl.pallas_call(
        _final_norm_kernel,
        grid=(t // tm,),
        in_specs=[pl.BlockSpec((tm, d), lambda i: (i, 0)), pl.BlockSpec((1, d), lambda i: (0, 0))],
        out_specs=pl.BlockSpec((tm, d), lambda i: (i, 0)),
        out_shape=jax.ShapeDtypeStruct((t, d), F32),
        compiler_params=_cparams(("parallel",)),
        name="final_norm",
    )(x, g)


def _block_diag(blocks):
    l, g, r, c = blocks.shape
    on_diag = jnp.eye(g, dtype=bool)[None, :, None, :, None]
    out = jnp.where(on_diag, blocks[:, :, :, None, :], jnp.zeros((), blocks.dtype))
    return out.reshape(l, g * r, g * c)


def kernel(x, norm_mix, w_in, s5_lambda_re, s5_lambda_im, s5_log_dt, s5_b_re, s5_b_im, s5_c_re, s5_c_im,
           s5_d, s5_w_glu, s5_b_glu, c_sinks, w_branch_a, w_branch_b, w_branch_c, w_out, norm_ffn,
           ffn_w_gate, ffn_w_up, ffn_w_down, moe_router, moe_w_gate, moe_w_up, moe_w_down, norm_final):
    b, s, d = x.shape
    depth = w_in.shape[0]
    t = b * s
    n_split = A_QKV + S5_WIDTH + C_WIDTH + 2 * C_KV_HEADS * HEAD_DIM

    w_in_p = jnp.concatenate([w_in[:, :, n_split:], w_in[:, :, :n_split]], axis=-1).astype(BF16)
    wa, wb, wc, wo = (w.astype(BF16) for w in (w_branch_a, w_branch_b, w_branch_c, w_out))
    fg, fu, fd = (w.astype(BF16) for w in (ffn_w_gate, ffn_w_up, ffn_w_down))
    mg, mu, md = (w.astype(BF16) for w in (moe_w_gate, moe_w_up, moe_w_down))
    router_t = jnp.swapaxes(moe_router, 1, 2)
    lam_re = s5_lambda_re.reshape(depth, 1, S5_N)
    lam_im = s5_lambda_im.reshape(depth, 1, S5_N)
    log_dt = jnp.repeat(s5_log_dt, S5_STATE, axis=-1).reshape(depth, 1, S5_N)
    bre = _block_diag(jnp.swapaxes(s5_b_re, 2, 3)).astype(BF16)
    bim = _block_diag(jnp.swapaxes(s5_b_im, 2, 3)).astype(BF16)
    cre = _block_diag(jnp.swapaxes(s5_c_re, 2, 3)).astype(BF16)
    cim = _block_diag(jnp.swapaxes(s5_c_im, 2, 3)).astype(BF16)
    wglu = s5_w_glu.astype(BF16)
    cos, sin = _rope_tables(s)

    xt = x.reshape(t, d)
    for l in range(depth):
        proj = _inproj(xt, norm_mix[l][None, :], w_in_p, l)
        proj3 = proj.reshape(b, s, IN_COLS)
        o_a = _dilated_mixer(proj3).reshape(t, A_WIDTH)
        o_b = _s5_mixer(proj3, lam_re[l], lam_im[l], log_dt[l], bre[l], bim[l], cre[l], cim[l],
                        s5_d[l][None, :], wglu[l], s5_b_glu[l][None, :]).reshape(t, S5_WIDTH)
        o_c = _swa_mixer(proj3, c_sinks[l], cos, sin).reshape(t, C_WIDTH)
        i = l // 2
        if l % 2 == 0:
            xt, h = _merge(xt, o_a, o_b, o_c, proj, wa[l], wb[l], wc[l], wo[l], norm_ffn[l][None, :])
            xt = _ffn(xt, h, fg, fu, fd, i)
        else:
            xt, h, logits_t = _merge(xt, o_a, o_b, o_c, proj, wa[l], wb[l], wc[l], wo[l],
                                     norm_ffn[l][None, :], router_t[i])
            gate, rank, cnt = _route(logits_t, TM_MOE)
            counts = cnt[:, :, 0].reshape(-1)
            xt = _moe(xt, h, counts, rank[:, None, :], rank[:, :, None], gate[:, :, None],
                      mg, mu, md, i, TM_MOE)
    return _final_norm(xt, norm_final[None, :]).reshape(b, s, d)
```

```python
import functools
import math

import jax
import jax.numpy as jnp
import numpy as np
from jax import lax
from jax.experimental import pallas as pl
from jax.experimental.pallas import tpu as pltpu

F32 = jnp.float32
BF16 = jnp.bfloat16

D_MODEL = 1024
HEAD_DIM = 64
BLOCK = 128
LANES = 128
SUBLANES = 8
DILATIONS = (1, 4, 16)
N_DIL = 3
A_HEADS = 4
A_WIDTH = A_HEADS * HEAD_DIM
A_QKV = 3 * N_DIL * A_WIDTH
S5_WIDTH = 256
S5_GROUPS = 16
S5_GROUP_CH = 16
S5_STATE = 64
S5_N = S5_GROUPS * S5_STATE
C_Q_HEADS = 8
C_KV_HEADS = 2
C_WIDTH = C_Q_HEADS * HEAD_DIM
ROPE_THETA = 150000.0
N_GATE = 3 * D_MODEL
IN_COLS = 6400
D_FF = 3584
N_EXPERTS = 8
RMS_EPS = 1e-6
ATT_SCALE = HEAD_DIM ** -0.5

OFF_GATE = 0
OFF_A = N_GATE
OFF_B = OFF_A + A_QKV
OFF_Q = OFF_B + S5_WIDTH
OFF_K = OFF_Q + C_WIDTH
OFF_V = OFF_K + C_KV_HEADS * HEAD_DIM

TM_PROJ = 1024
TN_PROJ = 1280
TM_MERGE = 512
TM_FFN = 1024
TF_FFN = 512
TS_S5 = 512
TM_MOE = 2048
TF_MOE = 512
CH_MOE = 128
KMIN_MOE = 3
KMAX_MOE = 6
SC_MOE = 512
VMEM_LIMIT = 56 * 1024 * 1024
VMEM_LIMIT_MOE = 60 * 1024 * 1024


def _cparams(sem):
    return pltpu.CompilerParams(dimension_semantics=sem, vmem_limit_bytes=VMEM_LIMIT)


def _dot(a, b):
    return jnp.dot(a, b, preferred_element_type=F32)


def _dot_nt(a, b):
    return lax.dot_general(a, b, (((1,), (1,)), ((), ())), preferred_element_type=F32)


def _rms(x, g):
    return x * lax.rsqrt(jnp.mean(x * x, axis=-1, keepdims=True) + RMS_EPS) * g


def _inproj_kernel(x_ref, g_ref, w_ref, o_ref, h_ref):
    @pl.when(pl.program_id(1) == 0)
    def _():
        h_ref[...] = _rms(x_ref[...], g_ref[...]).astype(BF16)

    o_ref[...] = _dot(h_ref[...], w_ref[...]).astype(o_ref.dtype)


def _inproj(x, g, w, l):
    t, d = x.shape
    n = w.shape[2]
    return pl.pallas_call(
        _inproj_kernel,
        grid=(t // TM_PROJ, n // TN_PROJ),
        in_specs=[
            pl.BlockSpec((TM_PROJ, d), lambda i, j: (i, 0)),
            pl.BlockSpec((1, d), lambda i, j: (0, 0)),
            pl.BlockSpec((None, d, TN_PROJ), lambda i, j: (l, 0, j)),
        ],
        out_specs=pl.BlockSpec((TM_PROJ, TN_PROJ), lambda i, j: (i, j)),
        out_shape=jax.ShapeDtypeStruct((t, n), BF16),
        scratch_shapes=[pltpu.VMEM((TM_PROJ, d), BF16)],
        compiler_params=_cparams(("parallel", "arbitrary")),
        name="inproj",
    )(x, g, w)


def _band_mask(rows, width):
    qi = lax.broadcasted_iota(jnp.int32, (rows, width), 0) % BLOCK
    ki = lax.broadcasted_iota(jnp.int32, (rows, width), 1)
    return (ki >= qi) & (ki <= qi + BLOCK)


def _causal_mask(rows, width):
    qi = lax.broadcasted_iota(jnp.int32, (rows, width), 0) % BLOCK
    ki = lax.broadcasted_iota(jnp.int32, (rows, width), 1)
    return ki <= qi


def _softmax_pv(s, mask, v, sink=None):
    s = jnp.where(mask, s, -jnp.inf)
    m = jnp.max(s, axis=-1, keepdims=True)
    if sink is not None:
        m = jnp.maximum(m, sink)
    p = jnp.exp(s - m)
    den = jnp.sum(p, axis=-1, keepdims=True)
    if sink is not None:
        den = den + jnp.exp(sink - m)
    o = _dot(p.astype(BF16), v)
    return o, m, den


def _dilated_kernel(q0_ref, q1_ref, q2_ref, k0_ref, k1_ref, k2_ref, v0_ref, v1_ref, v2_ref,
                    o_ref, qf_ref, kf_ref, vf_ref, og_ref, lg_ref):
    seq = o_ref.shape[0]
    lane = lax.broadcasted_iota(jnp.int32, (1, LANES), 1)
    lo = lane < HEAD_DIM
    band = _band_mask(2 * BLOCK, 2 * BLOCK)
    causal = _causal_mask(2 * BLOCK, BLOCK)

    def attend(q, k, v, mask):
        zero = jnp.zeros_like(q)
        qs = jnp.concatenate([jnp.where(lo, q, zero), jnp.where(lo, zero, q)], axis=0)
        s = _dot_nt(qs, k)
        o, m, den = _softmax_pv(s, mask, v)
        o = o * (1.0 / den)
        lse = m + jnp.log(den)
        o_pair = jnp.where(lo, o[:BLOCK], o[BLOCK:])
        l_pair = jnp.where(lo, lse[:BLOCK], lse[BLOCK:])
        return o_pair, l_pair

    for idx, (qr, kr, vr) in enumerate(((q1_ref, k1_ref, v1_ref), (q2_ref, k2_ref, v2_ref))):
        qf_ref[idx] = qr[...].astype(F32) * ATT_SCALE
        kf_ref[idx] = kr[...].astype(F32)
        vf_ref[idx] = vr[...].astype(F32)

    o_p, l_p = attend(q0_ref[pl.ds(0, BLOCK), :] * ATT_SCALE, k0_ref[pl.ds(0, BLOCK), :],
                      v0_ref[pl.ds(0, BLOCK), :], causal)
    og_ref[0, pl.ds(0, BLOCK), :] = o_p
    lg_ref[0, pl.ds(0, BLOCK), :] = l_p

    for n in range(1, seq // BLOCK):
        r0, w0 = n * BLOCK, (n - 1) * BLOCK
        o_p, l_p = attend(q0_ref[pl.ds(r0, BLOCK), :] * ATT_SCALE, k0_ref[pl.ds(w0, 2 * BLOCK), :],
                          v0_ref[pl.ds(w0, 2 * BLOCK), :], band)
        og_ref[0, pl.ds(r0, BLOCK), :] = o_p
        lg_ref[0, pl.ds(r0, BLOCK), :] = l_p

    for idx, d in ((0, DILATIONS[1]), (1, DILATIONS[2])):
        g = idx + 1
        nblk = seq // d // BLOCK
        for r in range(d):
            for n in range(nblk):
                q = qf_ref[idx, pl.ds(r + n * BLOCK * d, BLOCK, stride=d), :].astype(BF16)
                if n == 0:
                    k = kf_ref[idx, pl.ds(r, BLOCK, stride=d), :].astype(BF16)
                    v = vf_ref[idx, pl.ds(r, BLOCK, stride=d), :].astype(BF16)
                    o_p, l_p = attend(q, k, v, causal)
                else:
                    w0 = r + (n - 1) * BLOCK * d
                    k = kf_ref[idx, pl.ds(w0, 2 * BLOCK, stride=d), :].astype(BF16)
                    v = vf_ref[idx, pl.ds(w0, 2 * BLOCK, stride=d), :].astype(BF16)
                    o_p, l_p = attend(q, k, v, band)
                og_ref[g, pl.ds(r + n * BLOCK * d, BLOCK, stride=d), :] = o_p
                lg_ref[g, pl.ds(r + n * BLOCK * d, BLOCK, stride=d), :] = l_p

    l0, l1, l2 = lg_ref[0], lg_ref[1], lg_ref[2]
    mx = jnp.maximum(jnp.maximum(l0, l1), l2)
    e0, e1, e2 = jnp.exp(l0 - mx), jnp.exp(l1 - mx), jnp.exp(l2 - mx)
    tot = e0 + e1 + e2
    o_ref[...] = ((e0 * og_ref[0] + e1 * og_ref[1] + e2 * og_ref[2]) / tot).astype(o_ref.dtype)


def _dilated_mixer(proj3):
    b, s, _ = proj3.shape
    base = OFF_A // LANES

    def spec(which, g):
        col = base + which * (N_DIL * A_WIDTH // LANES) + g * (A_WIDTH // LANES)
        return pl.BlockSpec((None, s, LANES), lambda i, hp, col=col: (i, 0, col + hp))

    in_specs = [spec(w, g) for w in range(3) for g in range(N_DIL)]
    return pl.pallas_call(
        _dilated_kernel,
        grid=(b, A_WIDTH // LANES),
        in_specs=in_specs,
        out_specs=pl.BlockSpec((None, s, LANES), lambda i, hp: (i, 0, hp)),
        out_shape=jax.ShapeDtypeStruct((b, s, A_WIDTH), BF16),
        scratch_shapes=[
            pltpu.VMEM((2, s, LANES), F32),
            pltpu.VMEM((2, s, LANES), F32),
            pltpu.VMEM((2, s, LANES), F32),
            pltpu.VMEM((N_DIL, s, LANES), F32),
            pltpu.VMEM((N_DIL, s, LANES), F32),
        ],
        compiler_params=_cparams(("parallel", "parallel")),
        name="dilated_mixer",
    )(*([proj3] * 9))


def _swa_kernel(sink_ref, q_ref, k_ref, v_ref, cos_ref, sin_ref, o_ref, qs_ref, ks_ref, vs_ref):
    seq = o_ref.shape[0]
    g = pl.program_id(1)
    rep = C_Q_HEADS // C_KV_HEADS
    lane = lax.broadcasted_iota(jnp.int32, (1, LANES), 1)
    lo = lane < HEAD_DIM
    cos = cos_ref[...]
    sin = sin_ref[...]

    src = lax.broadcasted_iota(jnp.int32, (LANES, LANES), 0)
    dst = lax.broadcasted_iota(jnp.int32, (LANES, LANES), 1)
    half = HEAD_DIM // 2
    partner = jnp.where((dst % HEAD_DIM) < half, dst + half, dst - half)
    swap_mat = jnp.where(src == partner, 1.0, 0.0).astype(BF16)
    rep_mat = jnp.where(src == g * HEAD_DIM + dst % HEAD_DIM, 1.0, 0.0).astype(BF16)

    def rope(x):
        return x.astype(F32) * cos + _dot(x, swap_mat) * sin

    ks_ref[...] = _dot(rope(k_ref[...]).astype(BF16), rep_mat).astype(BF16)
    vs_ref[...] = _dot(v_ref[...], rep_mat).astype(BF16)
    for c in range(rep // 2):
        sl = slice(c * LANES, (c + 1) * LANES)
        qs_ref[:, sl] = (rope(q_ref[:, sl]) * ATT_SCALE).astype(BF16)

    rows = rep * BLOCK
    hrow = lax.broadcasted_iota(jnp.int32, (rows, 1), 0) // BLOCK
    sink = jnp.zeros((rows, 1), F32)
    for h in range(rep):
        sink = jnp.where(hrow == h, sink_ref[g * rep + h], sink)
    band = _band_mask(rows, 2 * BLOCK)
    causal = _causal_mask(rows, BLOCK)

    def block(r0, k, v, mask):
        parts = []
        for h in range(rep):
            q = qs_ref[pl.ds(r0, BLOCK), (h // 2) * LANES:(h // 2 + 1) * LANES]
            zero = jnp.zeros_like(q)
            parts.append(jnp.where(lo, q, zero) if h % 2 == 0 else jnp.where(lo, zero, q))
        s = _dot_nt(jnp.concatenate(parts, axis=0), k)
        o, _, den = _softmax_pv(s, mask, v, sink)
        o = o * (1.0 / den)
        for c in range(rep // 2):
            pair = jnp.where(lo, o[2 * c * BLOCK:(2 * c + 1) * BLOCK], o[(2 * c + 1) * BLOCK:(2 * c + 2) * BLOCK])
            o_ref[pl.ds(r0, BLOCK), c * LANES:(c + 1) * LANES] = pair.astype(o_ref.dtype)

    block(0, ks_ref[pl.ds(0, BLOCK), :], vs_ref[pl.ds(0, BLOCK), :], causal)

    def body(n, carry):
        r0 = pl.multiple_of(n * BLOCK, BLOCK)
        w0 = pl.multiple_of((n - 1) * BLOCK, BLOCK)
        block(r0, ks_ref[pl.ds(w0, 2 * BLOCK), :], vs_ref[pl.ds(w0, 2 * BLOCK), :], band)
        return carry

    lax.fori_loop(1, seq // BLOCK, body, 0, unroll=3)


def _rope_tables(seq):
    inv = ROPE_THETA ** (-jnp.arange(0, HEAD_DIM, 2, dtype=F32) / HEAD_DIM)
    ang = jnp.arange(seq, dtype=F32)[:, None] * inv[None, :]
    reps = LANES // (HEAD_DIM // 2)
    cos = jnp.tile(jnp.cos(ang), (1, reps))
    sign = jnp.where((jnp.arange(LANES) % HEAD_DIM) < HEAD_DIM // 2, -1.0, 1.0).astype(F32)
    sin = jnp.tile(jnp.sin(ang), (1, reps)) * sign[None, :]
    return cos, sin


def _swa_mixer(proj3, sinks, cos, sin):
    b, s, _ = proj3.shape
    qw = C_WIDTH // C_KV_HEADS
    return pl.pallas_call(
        _swa_kernel,
        grid=(b, C_KV_HEADS),
        in_specs=[
            pl.BlockSpec(memory_space=pltpu.SMEM),
            pl.BlockSpec((None, s, qw), lambda i, g: (i, 0, OFF_Q // qw + g)),
            pl.BlockSpec((None, s, LANES), lambda i, g: (i, 0, OFF_K // LANES)),
            pl.BlockSpec((None, s, LANES), lambda i, g: (i, 0, OFF_V // LANES)),
            pl.BlockSpec((s, LANES), lambda i, g: (0, 0)),
            pl.BlockSpec((s, LANES), lambda i, g: (0, 0)),
        ],
        out_specs=pl.BlockSpec((None, s, qw), lambda i, g: (i, 0, g)),
        out_shape=jax.ShapeDtypeStruct((b, s, C_WIDTH), BF16),
        scratch_shapes=[
            pltpu.VMEM((s, qw), BF16),
            pltpu.VMEM((s, LANES), BF16),
            pltpu.VMEM((s, LANES), BF16),
        ],
        compiler_params=_cparams(("parallel", "parallel")),
        name="swa_mixer",
    )(sinks, proj3, proj3, proj3, cos, sin)


def _cmul(ar, ai, br, bi):
    return ar * br - ai * bi, ar * bi + ai * br


def _s5_kernel(u_ref, lr_ref, li_ref, ldt_ref, bre_ref, bim_ref, cre_ref, cim_ref, d_ref, wg_ref, bg_ref,
               o_ref, hr_ref, hi_ref, pw_ref, car_ref):
    ts = u_ref.shape[0]

    @pl.when(pl.program_id(1) == 0)
    def _():
        lr, li = lr_ref[...], li_ref[...]
        dt = jnp.exp(ldt_ref[...])
        mag = jnp.exp(lr * dt)
        a_re, a_im = mag * jnp.cos(li * dt), mag * jnp.sin(li * dt)
        nr, ni = a_re - 1.0, a_im
        den = lr * lr + li * li
        z_re = (nr * lr + ni * li) / den
        z_im = (ni * lr - nr * li) / den
        row = lax.broadcasted_iota(jnp.int32, (SUBLANES, 1), 0)
        pows = [(a_re, a_im)]
        for _ in range(SUBLANES - 1):
            pows.append(_cmul(pows[-1][0], pows[-1][1], a_re, a_im))
        for j, sft in enumerate((1, 2, 4)):
            pr, pi = pows[sft - 1]
            pw_ref[2 * j] = jnp.where(row >= sft, pr, 0.0)
            pw_ref[2 * j + 1] = jnp.where(row >= sft, pi, 0.0)
        cr = jnp.zeros((SUBLANES, S5_N), F32)
        ci = jnp.zeros((SUBLANES, S5_N), F32)
        for i in range(SUBLANES):
            cr = jnp.where(row == i, pows[i][0], cr)
            ci = jnp.where(row == i, pows[i][1], ci)
        pw_ref[6] = cr
        pw_ref[7] = ci
        pw_ref[8] = jnp.broadcast_to(z_re, (SUBLANES, S5_N))
        pw_ref[9] = jnp.broadcast_to(z_im, (SUBLANES, S5_N))
        car_ref[...] = jnp.zeros_like(car_ref)

    u = u_ref[...]
    bu_re = _dot(u, bre_ref[...])
    bu_im = _dot(u, bim_ref[...])
    z_re = pw_ref[8, 0:1, :]
    z_im = pw_ref[9, 0:1, :]
    hr_ref[...] = z_re * bu_re - z_im * bu_im
    hi_ref[...] = z_re * bu_im + z_im * bu_re

    mults = [(pw_ref[2 * j], pw_ref[2 * j + 1]) for j in range(3)]
    acr, aci = pw_ref[6], pw_ref[7]

    def chunk(k, carry):
        cr, ci = carry
        r0 = pl.multiple_of(k * SUBLANES, SUBLANES)
        xr = hr_ref[pl.ds(r0, SUBLANES), :]
        xi = hi_ref[pl.ds(r0, SUBLANES), :]
        for (mr, mi), sft in zip(mults, (1, 2, 4)):
            sr = pltpu.roll(xr, sft, 0)
            si = pltpu.roll(xi, sft, 0)
            xr, xi = xr + (mr * sr - mi * si), xi + (mr * si + mi * sr)
        xr = xr + (acr * cr - aci * ci)
        xi = xi + (acr * ci + aci * cr)
        hr_ref[pl.ds(r0, SUBLANES), :] = xr
        hi_ref[pl.ds(r0, SUBLANES), :] = xi
        return xr[SUBLANES - 1:SUBLANES, :], xi[SUBLANES - 1:SUBLANES, :]

    cr, ci = lax.fori_loop(0, ts // SUBLANES, chunk, (car_ref[0, 0:1, :], car_ref[1, 0:1, :]))
    car_ref[0] = jnp.broadcast_to(cr, (SUBLANES, S5_N))
    car_ref[1] = jnp.broadcast_to(ci, (SUBLANES, S5_N))

    y = _dot(hr_ref[...].astype(BF16), cre_ref[...]) - _dot(hi_ref[...].astype(BF16), cim_ref[...])
    y = y + d_ref[...] * u.astype(F32)
    gl = jax.nn.gelu(y)
    out = gl * jax.nn.sigmoid(_dot(gl.astype(BF16), wg_ref[...]) + bg_ref[...])
    o_ref[...] = out.astype(o_ref.dtype)


def _s5_mixer(proj3, lr, li, ldt, bre, bim, cre, cim, dsk, wglu, bglu):
    b, s, _ = proj3.shape
    full = lambda shape: pl.BlockSpec(shape, lambda i, t: (0,) * len(shape))
    return pl.pallas_call(
        _s5_kernel,
        grid=(b, s // TS_S5),
        in_specs=[
            pl.BlockSpec((None, TS_S5, S5_WIDTH), lambda i, t: (i, t, OFF_B // S5_WIDTH)),
            full((1, S5_N)), full((1, S5_N)), full((1, S5_N)),
            full((S5_WIDTH, S5_N)), full((S5_WIDTH, S5_N)),
            full((S5_N, S5_WIDTH)), full((S5_N, S5_WIDTH)),
            full((1, S5_WIDTH)), full((S5_WIDTH, S5_WIDTH)), full((1, S5_WIDTH)),
        ],
        out_specs=pl.BlockSpec((None, TS_S5, S5_WIDTH), lambda i, t: (i, t, 0)),
        out_shape=jax.ShapeDtypeStruct((b, s, S5_WIDTH), BF16),
        scratch_shapes=[
            pltpu.VMEM((TS_S5, S5_N), F32),
            pltpu.VMEM((TS_S5, S5_N), F32),
            pltpu.VMEM((10, SUBLANES, S5_N), F32),
            pltpu.VMEM((2, SUBLANES, S5_N), F32),
        ],
        compiler_params=_cparams(("parallel", "arbitrary")),
        name="s5_mixer",
    )(proj3, lr, li, ldt, bre, bim, cre, cim, dsk, wglu, bglu)


def _merge_kernel(x_ref, oa_ref, ob_ref, oc_ref, ga_ref, gb_ref, gc_ref, wa_ref, wb_ref, wc_ref, wo_ref,
                  gn_ref, *rest, with_router):
    if with_router:
        rt_ref, xo_ref, h_ref, lg_ref = rest
    else:
        xo_ref, h_ref = rest

    def sig(r):
        return jax.nn.sigmoid(r[...].astype(F32))

    merged = (sig(ga_ref) * _dot(oa_ref[...], wa_ref[...])
              + sig(gb_ref) * _dot(ob_ref[...], wb_ref[...])
              + sig(gc_ref) * _dot(oc_ref[...], wc_ref[...]))
    xn = x_ref[...] + _dot(merged.astype(BF16), wo_ref[...])
    xo_ref[...] = xn
    h = _rms(xn, gn_ref[...])
    h_hi = h.astype(BF16)
    h_ref[...] = h_hi
    if with_router:
        h_lo = (h - h_hi.astype(F32)).astype(BF16)
        rt = rt_ref[...]
        r_hi = rt.astype(BF16)
        r_lo = (rt - r_hi.astype(F32)).astype(BF16)
        lg_ref[...] = _dot_nt(r_hi, h_hi) + (_dot_nt(r_hi, h_lo) + _dot_nt(r_lo, h_hi))


def _merge(x, oa, ob, oc, proj, wa, wb, wc, wo, gn, router_t=None):
    t, d = x.shape
    tm = TM_MERGE
    with_router = router_t is not None
    row = lambda w: pl.BlockSpec((tm, w), lambda i: (i, 0))
    full = lambda a: pl.BlockSpec(a.shape, lambda i: (0, 0))
    in_specs = [row(d), row(A_WIDTH), row(S5_WIDTH), row(C_WIDTH)]
    in_specs += [pl.BlockSpec((tm, d), lambda i, c=c: (i, c)) for c in range(3)]
    in_specs += [full(wa), full(wb), full(wc), full(wo), full(gn)]
    args = [x, oa, ob, oc, proj, proj, proj, wa, wb, wc, wo, gn]
    out_specs = [row(d), row(d)]
    out_shape = [jax.ShapeDtypeStruct((t, d), F32), jax.ShapeDtypeStruct((t, d), BF16)]
    if with_router:
        in_specs.append(full(router_t))
        args.append(router_t)
        out_specs.append(pl.BlockSpec((N_EXPERTS, tm), lambda i: (0, i)))
        out_shape.append(jax.ShapeDtypeStruct((N_EXPERTS, t), F32))
    return pl.pallas_call(
        functools.partial(_merge_kernel, with_router=with_router),
        grid=(t // tm,),
        in_specs=in_specs,
        out_specs=out_specs,
        out_shape=out_shape,
        compiler_params=_cparams(("parallel",)),
        name="merge_router" if with_router else "merge",
    )(*args)


def _ffn_kernel(x_ref, h_ref, wg_ref, wu_ref, wd_ref, o_ref):
    @pl.when(pl.program_id(1) == 0)
    def _():
        o_ref[...] = x_ref[...]

    h = h_ref[...]
    act = jax.nn.silu(_dot(h, wg_ref[...])) * _dot(h, wu_ref[...])
    o_ref[...] += _dot(act.astype(BF16), wd_ref[...])


def _ffn(x, h, wg, wu, wd, li):
    t, d = x.shape
    ff = wg.shape[2]
    return pl.pallas_call(
        _ffn_kernel,
        grid=(t // TM_FFN, ff // TF_FFN),
        in_specs=[
            pl.BlockSpec((TM_FFN, d), lambda i, f: (i, 0)),
            pl.BlockSpec((TM_FFN, d), lambda i, f: (i, 0)),
            pl.BlockSpec((None, d, TF_FFN), lambda i, f: (li, 0, f)),
            pl.BlockSpec((None, d, TF_FFN), lambda i, f: (li, 0, f)),
            pl.BlockSpec((None, TF_FFN, d), lambda i, f: (li, f, 0)),
        ],
        out_specs=pl.BlockSpec((TM_FFN, d), lambda i, f: (i, 0)),
        out_shape=jax.ShapeDtypeStruct((t, d), F32),
        compiler_params=_cparams(("parallel", "arbitrary")),
        name="dense_ffn",
    )(x, h, wg, wu, wd)


def _route_kernel(lg_ref, gate_ref, rank_ref, cnt_ref):
    lg = lg_ref[...]
    tm = lg.shape[1]
    eidx = lax.broadcasted_iota(jnp.int32, lg.shape, 0)
    m1 = jnp.max(lg, axis=0, keepdims=True)
    i1 = jnp.min(jnp.where(lg == m1, eidx, N_EXPERTS), axis=0, keepdims=True)
    rest = jnp.where(eidx == i1, -jnp.inf, lg)
    m2 = jnp.max(rest, axis=0, keepdims=True)
    i2 = jnp.min(jnp.where(rest == m2, eidx, N_EXPERTS), axis=0, keepdims=True)
    e2 = jnp.exp(m2 - m1)
    tot = 1.0 + e2
    sel1 = eidx == i1
    sel2 = eidx == i2
    gate_ref[...] = jnp.where(sel1, 1.0 / tot, jnp.where(sel2, e2 / tot, 0.0))
    sel = (sel1 | sel2).astype(jnp.int32)
    lane = lax.broadcasted_iota(jnp.int32, lg.shape, 1)
    c = sel
    sft = 1
    while sft < tm:
        c = c + jnp.where(lane >= sft, pltpu.roll(c, sft, 1), 0)
        sft *= 2
    rank_ref[...] = jnp.where(sel > 0, c - 1, -1)
    cnt_ref[...] = jnp.broadcast_to(jnp.sum(sel, axis=1, keepdims=True), cnt_ref.shape)


def _route(logits_t, tm):
    e, t = logits_t.shape
    nt = t // tm
    return pl.pallas_call(
        _route_kernel,
        grid=(nt,),
        in_specs=[pl.BlockSpec((e, tm), lambda i: (0, i))],
        out_specs=[
            pl.BlockSpec((e, tm), lambda i: (0, i)),
            pl.BlockSpec((e, tm), lambda i: (0, i)),
            pl.BlockSpec((None, e, LANES), lambda i: (i, 0, 0)),
        ],
        out_shape=[
            jax.ShapeDtypeStruct((e, t), F32),
            jax.ShapeDtypeStruct((e, t), jnp.int32),
            jax.ShapeDtypeStruct((nt, e, LANES), jnp.int32),
        ],
        compiler_params=_cparams(("parallel",)),
        name="route_top2",
    )(logits_t)


def _moe_kernel(cnt_ref, x_ref, h_ref, rrow_ref, rcol_ref, gcol_ref, wg_ref, wu_ref, wd_ref, gfin_ref,
                o_ref, hc_ref, y_ref, *, final_norm):
    i, e, f = pl.program_id(0), pl.program_id(1), pl.program_id(2)
    ne, nf = pl.num_programs(1), pl.num_programs(2)
    tm = h_ref.shape[0]
    n_sel = cnt_ref[i * N_EXPERTS + e]
    n_chunks = (n_sel + (CH_MOE - 1)) // CH_MOE
    k_path = jnp.clip(n_chunks, KMIN_MOE, KMAX_MOE)
    n_over = jnp.maximum(n_chunks - KMAX_MOE, 0)

    def expert_part(hc):
        act = jax.nn.silu(_dot(hc, wg_ref[...].astype(BF16))) * _dot(hc, wu_ref[...].astype(BF16))
        return _dot(act.astype(BF16), wd_ref[...].astype(BF16))

    def compact(r0, rows):
        slot = lax.broadcasted_iota(jnp.int32, (rows, tm), 0) + r0
        onehot = jnp.where(slot == rrow_ref[...], 1.0, 0.0).astype(BF16)
        return _dot(onehot, h_ref[...]).astype(BF16)

    def scatter_add(r0, y):
        rows = y.shape[0]
        yb = y.astype(BF16)
        for q in range(tm // SC_MOE):
            sl = pl.ds(q * SC_MOE, SC_MOE)
            slot = lax.broadcasted_iota(jnp.int32, (SC_MOE, rows), 1) + r0
            onehot = jnp.where(slot == rcol_ref[sl, :], 1.0, 0.0).astype(BF16)
            o_ref[sl, :] += gcol_ref[sl, :] * _dot(onehot, yb)

    @pl.when((e == 0) & (f == 0))
    def _():
        o_ref[...] = x_ref[...]

    def static_path(rows):
        sl = pl.ds(0, rows)

        @pl.when(f == 0)
        def _():
            hc_ref[sl, :] = compact(0, rows)
            y_ref[sl, :] = jnp.zeros((rows, D_MODEL), F32)

        y_ref[sl, :] += expert_part(hc_ref[sl, :])

        @pl.when(f == nf - 1)
        def _():
            scatter_add(0, y_ref[sl, :])

    for k in range(KMIN_MOE, KMAX_MOE + 1):
        pl.when(k_path == k)(functools.partial(static_path, k * CH_MOE))

    def overflow(c, carry):
        r0 = (KMAX_MOE + c) * CH_MOE
        scatter_add(r0, expert_part(compact(r0, CH_MOE)))
        return carry

    lax.fori_loop(0, n_over, overflow, 0)

    if final_norm:
        @pl.when((e == ne - 1) & (f == nf - 1))
        def _():
            o_ref[...] = _rms(o_ref[...], gfin_ref[...])


def _moe(x, h, counts, rank_row, rank_col, gate_col, wg, wu, wd, li, tm, final_gain=None):
    final_norm = final_gain is not None
    if not final_norm:
        final_gain = jnp.ones((1, x.shape[1]), F32)
    t, d = x.shape
    _, ne, _, ff = wg.shape
    nt = t // tm
    once = pl.Buffered(1)
    grid_spec = pltpu.PrefetchScalarGridSpec(
        num_scalar_prefetch=1,
        grid=(nt, ne, ff // TF_MOE),
        in_specs=[
            pl.BlockSpec((tm, d), lambda i, e, f, c: (i, 0), pipeline_mode=once),
            pl.BlockSpec((tm, d), lambda i, e, f, c: (i, 0), pipeline_mode=once),
            pl.BlockSpec((None, 1, tm), lambda i, e, f, c: (e, 0, i)),
            pl.BlockSpec((None, tm, 1), lambda i, e, f, c: (e, i, 0)),
            pl.BlockSpec((None, tm, 1), lambda i, e, f, c: (e, i, 0)),
            pl.BlockSpec((None, None, d, TF_MOE), lambda i, e, f, c: (li, e, 0, f)),
            pl.BlockSpec((None, None, d, TF_MOE), lambda i, e, f, c: (li, e, 0, f)),
            pl.BlockSpec((None, None, TF_MOE, d), lambda i, e, f, c: (li, e, f, 0)),
            pl.BlockSpec((1, d), lambda i, e, f, c: (0, 0)),
        ],
        out_specs=pl.BlockSpec((tm, d), lambda i, e, f, c: (i, 0)),
        scratch_shapes=[pltpu.VMEM((KMAX_MOE * CH_MOE, d), BF16), pltpu.VMEM((KMAX_MOE * CH_MOE, d), F32)],
    )
    return pl.pallas_call(
        functools.partial(_moe_kernel, final_norm=final_norm),
        grid_spec=grid_spec,
        out_shape=jax.ShapeDtypeStruct((t, d), F32),
        compiler_params=pltpu.CompilerParams(dimension_semantics=("parallel", "arbitrary", "arbitrary"),
                                             vmem_limit_bytes=VMEM_LIMIT_MOE),
        name="moe_experts",
    )(counts, x, h, rank_row, rank_col, gate_col, wg, wu, wd, final_gain)


def _final_norm_kernel(x_ref, g_ref, o_ref):
    o_ref[...] = _rms(x_ref[...], g_ref[...])


def _final_norm(x, g):
    t, d = x.shape
    tm = TM_PROJ
    return pl.pallas_call(
        _final_norm_kernel,
        grid=(t // tm,),
        in_specs=[pl.BlockSpec((tm, d), lambda i: (i, 0)), pl.BlockSpec((1, d), lambda i: (0, 0))],
        out_specs=pl.BlockSpec((tm, d), lambda i: (i, 0)),
        out_shape=jax.ShapeDtypeStruct((t, d), F32),
        compiler_params=_cparams(("parallel",)),
        name="final_norm",
    )(x, g)


def _block_diag(blocks):
    l, g, r, c = blocks.shape
    on_diag = jnp.eye(g, dtype=bool)[None, :, None, :, None]
    out = jnp.where(on_diag, blocks[:, :, :, None, :], jnp.zeros((), blocks.dtype))
    return out.reshape(l, g * r, g * c)


def kernel(x, norm_mix, w_in, s5_lambda_re, s5_lambda_im, s5_log_dt, s5_b_re, s5_b_im, s5_c_re, s5_c_im,
           s5_d, s5_w_glu, s5_b_glu, c_sinks, w_branch_a, w_branch_b, w_branch_c, w_out, norm_ffn,
           ffn_w_gate, ffn_w_up, ffn_w_down, moe_router, moe_w_gate, moe_w_up, moe_w_down, norm_final):
    b, s, d = x.shape
    depth = w_in.shape[0]
    t = b * s
    n_split = A_QKV + S5_WIDTH + C_WIDTH + 2 * C_KV_HEADS * HEAD_DIM

    w_in_p = jnp.concatenate([w_in[:, :, n_split:], w_in[:, :, :n_split]], axis=-1).astype(BF16)
    wa, wb, wc, wo = (w.astype(BF16) for w in (w_branch_a, w_branch_b, w_branch_c, w_out))
    fg, fu, fd = (w.astype(BF16) for w in (ffn_w_gate, ffn_w_up, ffn_w_down))
    router_t = jnp.swapaxes(moe_router, 1, 2)
    lam_re = s5_lambda_re.reshape(depth, 1, S5_N)
    lam_im = s5_lambda_im.reshape(depth, 1, S5_N)
    log_dt = jnp.repeat(s5_log_dt, S5_STATE, axis=-1).reshape(depth, 1, S5_N)
    bre = _block_diag(jnp.swapaxes(s5_b_re, 2, 3)).astype(BF16)
    bim = _block_diag(jnp.swapaxes(s5_b_im, 2, 3)).astype(BF16)
    cre = _block_diag(jnp.swapaxes(s5_c_re, 2, 3)).astype(BF16)
    cim = _block_diag(jnp.swapaxes(s5_c_im, 2, 3)).astype(BF16)
    wglu = s5_w_glu.astype(BF16)
    cos, sin = _rope_tables(s)

    xt = x.reshape(t, d)
    for l in range(depth):
        proj = _inproj(xt, norm_mix[l][None, :], w_in_p, l)
        proj3 = proj.reshape(b, s, IN_COLS)
        o_a = _dilated_mixer(proj3).reshape(t, A_WIDTH)
        o_b = _s5_mixer(proj3, lam_re[l], lam_im[l], log_dt[l], bre[l], bim[l], cre[l], cim[l],
                        s5_d[l][None, :], wglu[l], s5_b_glu[l][None, :]).reshape(t, S5_WIDTH)
        o_c = _swa_mixer(proj3, c_sinks[l], cos, sin).reshape(t, C_WIDTH)
        i = l // 2
        if l % 2 == 0:
            xt, h = _merge(xt, o_a, o_b, o_c, proj, wa[l], wb[l], wc[l], wo[l], norm_ffn[l][None, :])
            xt = _ffn(xt, h, fg, fu, fd, i)
        else:
            xt, h, logits_t = _merge(xt, o_a, o_b, o_c, proj, wa[l], wb[l], wc[l], wo[l],
                                     norm_ffn[l][None, :], router_t[i])
            gate, rank, cnt = _route(logits_t, TM_MOE)
            counts = cnt[:, :, 0].reshape(-1)
            last = l == depth - 1
            xt = _moe(xt, h, counts, rank[:, None, :], rank[:, :, None], gate[:, :, None],
                      moe_w_gate, moe_w_up, moe_w_down, i, TM_MOE,
                      final_gain=norm_final[None, :] if last else None)
    if depth % 2 == 1:
        xt = _final_norm(xt, norm_final[None, :])
    return xt.reshape(b, s, d)
```

```python
import functools
import math

import jax
import jax.numpy as jnp
import numpy as np
from jax import lax
from jax.experimental import pallas as pl
from jax.experimental.pallas import tpu as pltpu

F32 = jnp.float32
BF16 = jnp.bfloat16

D_MODEL = 1024
HEAD_DIM = 64
BLOCK = 128
LANES = 128
SUBLANES = 8
DILATIONS = (1, 4, 16)
N_DIL = 3
A_HEADS = 4
A_WIDTH = A_HEADS * HEAD_DIM
A_QKV = 3 * N_DIL * A_WIDTH
S5_WIDTH = 256
S5_GROUPS = 16
S5_GROUP_CH = 16
S5_STATE = 64
S5_N = S5_GROUPS * S5_STATE
C_Q_HEADS = 8
C_KV_HEADS = 2
C_WIDTH = C_Q_HEADS * HEAD_DIM
ROPE_THETA = 150000.0
N_GATE = 3 * D_MODEL
IN_COLS = 6400
D_FF = 3584
N_EXPERTS = 8
RMS_EPS = 1e-6
ATT_SCALE = HEAD_DIM ** -0.5

OFF_GATE = 0
OFF_A = N_GATE
OFF_B = OFF_A + A_QKV
OFF_Q = OFF_B + S5_WIDTH
OFF_K = OFF_Q + C_WIDTH
OFF_V = OFF_K + C_KV_HEADS * HEAD_DIM

TM_PROJ = 1024
TN_PROJ = 1280
TM_MERGE = 512
TM_FFN = 1024
TF_FFN = 512
TS_S5 = 512
TM_MOE = 2048
TF_MOE = 512
CH_MOE = 128
KMIN_MOE = 3
KMAX_MOE = 6
SC_MOE = 512
VMEM_LIMIT = 56 * 1024 * 1024
VMEM_LIMIT_MOE = 60 * 1024 * 1024


def _cparams(sem):
    return pltpu.CompilerParams(dimension_semantics=sem, vmem_limit_bytes=VMEM_LIMIT)


def _dot(a, b):
    return jnp.dot(a, b, preferred_element_type=F32)


def _dot_nt(a, b):
    return lax.dot_general(a, b, (((1,), (1,)), ((), ())), preferred_element_type=F32)


def _rms(x, g):
    return x * lax.rsqrt(jnp.mean(x * x, axis=-1, keepdims=True) + RMS_EPS) * g


def _inproj_kernel(x_ref, g_ref, w_ref, o_ref, h_ref):
    @pl.when(pl.program_id(1) == 0)
    def _():
        h_ref[...] = _rms(x_ref[...], g_ref[...]).astype(BF16)

    o_ref[...] = _dot(h_ref[...], w_ref[...]).astype(o_ref.dtype)


def _inproj(x, g, w, l):
    t, d = x.shape
    n = w.shape[2]
    return pl.pallas_call(
        _inproj_kernel,
        grid=(t // TM_PROJ, n // TN_PROJ),
        in_specs=[
            pl.BlockSpec((TM_PROJ, d), lambda i, j: (i, 0)),
            pl.BlockSpec((1, d), lambda i, j: (0, 0)),
            pl.BlockSpec((None, d, TN_PROJ), lambda i, j: (l, 0, j)),
        ],
        out_specs=pl.BlockSpec((TM_PROJ, TN_PROJ), lambda i, j: (i, j)),
        out_shape=jax.ShapeDtypeStruct((t, n), BF16),
        scratch_shapes=[pltpu.VMEM((TM_PROJ, d), BF16)],
        compiler_params=_cparams(("parallel", "arbitrary")),
        name="inproj",
    )(x, g, w)


def _band_mask(rows, width):
    qi = lax.broadcasted_iota(jnp.int32, (rows, width), 0) % BLOCK
    ki = lax.broadcasted_iota(jnp.int32, (rows, width), 1)
    return (ki >= qi) & (ki <= qi + BLOCK)


def _causal_mask(rows, width):
    qi = lax.broadcasted_iota(jnp.int32, (rows, width), 0) % BLOCK
    ki = lax.broadcasted_iota(jnp.int32, (rows, width), 1)
    return ki <= qi


def _softmax_pv(s, mask, v, sink=None):
    s = jnp.where(mask, s, -jnp.inf)
    m = jnp.max(s, axis=-1, keepdims=True)
    if sink is not None:
        m = jnp.maximum(m, sink)
    p = jnp.exp(s - m)
    den = jnp.sum(p, axis=-1, keepdims=True)
    if sink is not None:
        den = den + jnp.exp(sink - m)
    o = _dot(p.astype(BF16), v)
    return o, m, den


def _dilated_kernel(q0_ref, q1_ref, q2_ref, k0_ref, k1_ref, k2_ref, v0_ref, v1_ref, v2_ref,
                    o_ref, qf_ref, kf_ref, vf_ref, og_ref, lg_ref):
    seq = o_ref.shape[0]
    lane = lax.broadcasted_iota(jnp.int32, (1, LANES), 1)
    lo = lane < HEAD_DIM
    band = _band_mask(2 * BLOCK, 2 * BLOCK)
    causal = _causal_mask(2 * BLOCK, BLOCK)

    def attend(q, k, v, mask):
        zero = jnp.zeros_like(q)
        qs = jnp.concatenate([jnp.where(lo, q, zero), jnp.where(lo, zero, q)], axis=0)
        s = _dot_nt(qs, k)
        o, m, den = _softmax_pv(s, mask, v)
        o = o * (1.0 / den)
        lse = m + jnp.log(den)
        o_pair = jnp.where(lo, o[:BLOCK], o[BLOCK:])
        l_pair = jnp.where(lo, lse[:BLOCK], lse[BLOCK:])
        return o_pair, l_pair

    for idx, (qr, kr, vr) in enumerate(((q1_ref, k1_ref, v1_ref), (q2_ref, k2_ref, v2_ref))):
        qf_ref[idx] = qr[...].astype(F32) * ATT_SCALE
        kf_ref[idx] = kr[...].astype(F32)
        vf_ref[idx] = vr[...].astype(F32)

    o_p, l_p = attend(q0_ref[pl.ds(0, BLOCK), :] * ATT_SCALE, k0_ref[pl.ds(0, BLOCK), :],
                      v0_ref[pl.ds(0, BLOCK), :], causal)
    og_ref[0, pl.ds(0, BLOCK), :] = o_p
    lg_ref[0, pl.ds(0, BLOCK), :] = l_p

    for n in range(1, seq // BLOCK):
        r0, w0 = n * BLOCK, (n - 1) * BLOCK
        o_p, l_p = attend(q0_ref[pl.ds(r0, BLOCK), :] * ATT_SCALE, k0_ref[pl.ds(w0, 2 * BLOCK), :],
                          v0_ref[pl.ds(w0, 2 * BLOCK), :], band)
        og_ref[0, pl.ds(r0, BLOCK), :] = o_p
        lg_ref[0, pl.ds(r0, BLOCK), :] = l_p

    for idx, d in ((0, DILATIONS[1]), (1, DILATIONS[2])):
        g = idx + 1
        nblk = seq // d // BLOCK
        for r in range(d):
            for n in range(nblk):
                q = qf_ref[idx, pl.ds(r + n * BLOCK * d, BLOCK, stride=d), :].astype(BF16)
                if n == 0:
                    k = kf_ref[idx, pl.ds(r, BLOCK, stride=d), :].astype(BF16)
                    v = vf_ref[idx, pl.ds(r, BLOCK, stride=d), :].astype(BF16)
                    o_p, l_p = attend(q, k, v, causal)
                else:
                    w0 = r + (n - 1) * BLOCK * d
                    k = kf_ref[idx, pl.ds(w0, 2 * BLOCK, stride=d), :].astype(BF16)
                    v = vf_ref[idx, pl.ds(w0, 2 * BLOCK, stride=d), :].astype(BF16)
                    o_p, l_p = attend(q, k, v, band)
                og_ref[g, pl.ds(r + n * BLOCK * d, BLOCK, stride=d), :] = o_p
                lg_ref[g, pl.ds(r + n * BLOCK * d, BLOCK, stride=d), :] = l_p

    l0, l1, l2 = lg_ref[0], lg_ref[1], lg_ref[2]
    mx = jnp.maximum(jnp.maximum(l0, l1), l2)
    e0, e1, e2 = jnp.exp(l0 - mx), jnp.exp(l1 - mx), jnp.exp(l2 - mx)
    tot = e0 + e1 + e2
    o_ref[...] = ((e0 * og_ref[0] + e1 * og_ref[1] + e2 * og_ref[2]) / tot).astype(o_ref.dtype)


def _dilated_mixer(proj3):
    b, s, _ = proj3.shape
    base = OFF_A // LANES

    def spec(which, g):
        col = base + which * (N_DIL * A_WIDTH // LANES) + g * (A_WIDTH // LANES)
        return pl.BlockSpec((None, s, LANES), lambda i, hp, col=col: (i, 0, col + hp))

    in_specs = [spec(w, g) for w in range(3) for g in range(N_DIL)]
    return pl.pallas_call(
        _dilated_kernel,
        grid=(b, A_WIDTH // LANES),
        in_specs=in_specs,
        out_specs=pl.BlockSpec((None, s, LANES), lambda i, hp: (i, 0, hp)),
        out_shape=jax.ShapeDtypeStruct((b, s, A_WIDTH), BF16),
        scratch_shapes=[
            pltpu.VMEM((2, s, LANES), F32),
            pltpu.VMEM((2, s, LANES), F32),
            pltpu.VMEM((2, s, LANES), F32),
            pltpu.VMEM((N_DIL, s, LANES), F32),
            pltpu.VMEM((N_DIL, s, LANES), F32),
        ],
        compiler_params=_cparams(("parallel", "parallel")),
        name="dilated_mixer",
    )(*([proj3] * 9))


def _swa_kernel(sink_ref, q_ref, k_ref, v_ref, cos_ref, sin_ref, o_ref, qs_ref, ks_ref, vs_ref):
    seq = o_ref.shape[0]
    g = pl.program_id(1)
    rep = C_Q_HEADS // C_KV_HEADS
    lane = lax.broadcasted_iota(jnp.int32, (1, LANES), 1)
    lo = lane < HEAD_DIM
    cos = cos_ref[...]
    sin = sin_ref[...]

    src = lax.broadcasted_iota(jnp.int32, (LANES, LANES), 0)
    dst = lax.broadcasted_iota(jnp.int32, (LANES, LANES), 1)
    half = HEAD_DIM // 2
    partner = jnp.where((dst % HEAD_DIM) < half, dst + half, dst - half)
    swap_mat = jnp.where(src == partner, 1.0, 0.0).astype(BF16)
    rep_mat = jnp.where(src == g * HEAD_DIM + dst % HEAD_DIM, 1.0, 0.0).astype(BF16)

    def rope(x):
        return x.astype(F32) * cos + _dot(x, swap_mat) * sin

    ks_ref[...] = _dot(rope(k_ref[...]).astype(BF16), rep_mat).astype(BF16)
    vs_ref[...] = _dot(v_ref[...], rep_mat).astype(BF16)
    for c in range(rep // 2):
        sl = slice(c * LANES, (c + 1) * LANES)
        qs_ref[:, sl] = (rope(q_ref[:, sl]) * ATT_SCALE).astype(BF16)

    rows = rep * BLOCK
    hrow = lax.broadcasted_iota(jnp.int32, (rows, 1), 0) // BLOCK
    sink = jnp.zeros((rows, 1), F32)
    for h in range(rep):
        sink = jnp.where(hrow == h, sink_ref[g * rep + h], sink)
    band = _band_mask(rows, 2 * BLOCK)
    causal = _causal_mask(rows, BLOCK)

    def block(r0, k, v, mask):
        parts = []
        for h in range(rep):
            q = qs_ref[pl.ds(r0, BLOCK), (h // 2) * LANES:(h // 2 + 1) * LANES]
            zero = jnp.zeros_like(q)
            parts.append(jnp.where(lo, q, zero) if h % 2 == 0 else jnp.where(lo, zero, q))
        s = _dot_nt(jnp.concatenate(parts, axis=0), k)
        o, _, den = _softmax_pv(s, mask, v, sink)
        o = o * (1.0 / den)
        for c in range(rep // 2):
            pair = jnp.where(lo, o[2 * c * BLOCK:(2 * c + 1) * BLOCK], o[(2 * c + 1) * BLOCK:(2 * c + 2) * BLOCK])
            o_ref[pl.ds(r0, BLOCK), c * LANES:(c + 1) * LANES] = pair.astype(o_ref.dtype)

    block(0, ks_ref[pl.ds(0, BLOCK), :], vs_ref[pl.ds(0, BLOCK), :], causal)

    def body(n, carry):
        r0 = pl.multiple_of(n * BLOCK, BLOCK)
        w0 = pl.multiple_of((n - 1) * BLOCK, BLOCK)
        block(r0, ks_ref[pl.ds(w0, 2 * BLOCK), :], vs_ref[pl.ds(w0, 2 * BLOCK), :], band)
        return carry

    lax.fori_loop(1, seq // BLOCK, body, 0, unroll=3)


def _rope_tables(seq):
    inv = ROPE_THETA ** (-jnp.arange(0, HEAD_DIM, 2, dtype=F32) / HEAD_DIM)
    ang = jnp.arange(seq, dtype=F32)[:, None] * inv[None, :]
    reps = LANES // (HEAD_DIM // 2)
    cos = jnp.tile(jnp.cos(ang), (1, reps))
    sign = jnp.where((jnp.arange(LANES) % HEAD_DIM) < HEAD_DIM // 2, -1.0, 1.0).astype(F32)
    sin = jnp.tile(jnp.sin(ang), (1, reps)) * sign[None, :]
    return cos, sin


def _swa_mixer(proj3, sinks, cos, sin):
    b, s, _ = proj3.shape
    qw = C_WIDTH // C_KV_HEADS
    return pl.pallas_call(
        _swa_kernel,
        grid=(b, C_KV_HEADS),
        in_specs=[
            pl.BlockSpec(memory_space=pltpu.SMEM),
            pl.BlockSpec((None, s, qw), lambda i, g: (i, 0, OFF_Q // qw + g)),
            pl.BlockSpec((None, s, LANES), lambda i, g: (i, 0, OFF_K // LANES)),
            pl.BlockSpec((None, s, LANES), lambda i, g: (i, 0, OFF_V // LANES)),
            pl.BlockSpec((s, LANES), lambda i, g: (0, 0)),
            pl.BlockSpec((s, LANES), lambda i, g: (0, 0)),
        ],
        out_specs=pl.BlockSpec((None, s, qw), lambda i, g: (i, 0, g)),
        out_shape=jax.ShapeDtypeStruct((b, s, C_WIDTH), BF16),
        scratch_shapes=[
            pltpu.VMEM((s, qw), BF16),
            pltpu.VMEM((s, LANES), BF16),
            pltpu.VMEM((s, LANES), BF16),
        ],
        compiler_params=_cparams(("parallel", "parallel")),
        name="swa_mixer",
    )(sinks, proj3, proj3, proj3, cos, sin)


def _cmul(ar, ai, br, bi):
    return ar * br - ai * bi, ar * bi + ai * br


def _s5_kernel(u_ref, lr_ref, li_ref, ldt_ref, bre_ref, bim_ref, cre_ref, cim_ref, d_ref, wg_ref, bg_ref,
               o_ref, hr_ref, hi_ref, pw_ref, car_ref):
    ts = u_ref.shape[0]

    @pl.when(pl.program_id(1) == 0)
    def _():
        lr, li = lr_ref[...], li_ref[...]
        dt = jnp.exp(ldt_ref[...])
        mag = jnp.exp(lr * dt)
        a_re, a_im = mag * jnp.cos(li * dt), mag * jnp.sin(li * dt)
        nr, ni = a_re - 1.0, a_im
        den = lr * lr + li * li
        z_re = (nr * lr + ni * li) / den
        z_im = (ni * lr - nr * li) / den
        row = lax.broadcasted_iota(jnp.int32, (SUBLANES, 1), 0)
        pows = [(a_re, a_im)]
        for _ in range(SUBLANES - 1):
            pows.append(_cmul(pows[-1][0], pows[-1][1], a_re, a_im))
        for j, sft in enumerate((1, 2, 4)):
            pr, pi = pows[sft - 1]
            pw_ref[2 * j] = jnp.where(row >= sft, pr, 0.0)
            pw_ref[2 * j + 1] = jnp.where(row >= sft, pi, 0.0)
        cr = jnp.zeros((SUBLANES, S5_N), F32)
        ci = jnp.zeros((SUBLANES, S5_N), F32)
        for i in range(SUBLANES):
            cr = jnp.where(row == i, pows[i][0], cr)
            ci = jnp.where(row == i, pows[i][1], ci)
        pw_ref[6] = cr
        pw_ref[7] = ci
        pw_ref[8] = jnp.broadcast_to(z_re, (SUBLANES, S5_N))
        pw_ref[9] = jnp.broadcast_to(z_im, (SUBLANES, S5_N))
        car_ref[...] = jnp.zeros_like(car_ref)

    u = u_ref[...]
    bu_re = _dot(u, bre_ref[...])
    bu_im = _dot(u, bim_ref[...])
    z_re = pw_ref[8, 0:1, :]
    z_im = pw_ref[9, 0:1, :]
    hr_ref[...] = z_re * bu_re - z_im * bu_im
    hi_ref[...] = z_re * bu_im + z_im * bu_re

    mults = [(pw_ref[2 * j], pw_ref[2 * j + 1]) for j in range(3)]
    acr, aci = pw_ref[6], pw_ref[7]

    def chunk(k, carry):
        cr, ci = carry
        r0 = pl.multiple_of(k * SUBLANES, SUBLANES)
        xr = hr_ref[pl.ds(r0, SUBLANES), :]
        xi = hi_ref[pl.ds(r0, SUBLANES), :]
        for (mr, mi), sft in zip(mults, (1, 2, 4)):
            sr = pltpu.roll(xr, sft, 0)
            si = pltpu.roll(xi, sft, 0)
            xr, xi = xr + (mr * sr - mi * si), xi + (mr * si + mi * sr)
        xr = xr + (acr * cr - aci * ci)
        xi = xi + (acr * ci + aci * cr)
        hr_ref[pl.ds(r0, SUBLANES), :] = xr
        hi_ref[pl.ds(r0, SUBLANES), :] = xi
        return xr[SUBLANES - 1:SUBLANES, :], xi[SUBLANES - 1:SUBLANES, :]

    cr, ci = lax.fori_loop(0, ts // SUBLANES, chunk, (car_ref[0, 0:1, :], car_ref[1, 0:1, :]), unroll=2)
    car_ref[0] = jnp.broadcast_to(cr, (SUBLANES, S5_N))
    car_ref[1] = jnp.broadcast_to(ci, (SUBLANES, S5_N))

    y = _dot(hr_ref[...].astype(BF16), cre_ref[...]) - _dot(hi_ref[...].astype(BF16), cim_ref[...])
    y = y + d_ref[...] * u.astype(F32)
    gl = jax.nn.gelu(y)
    out = gl * jax.nn.sigmoid(_dot(gl.astype(BF16), wg_ref[...]) + bg_ref[...])
    o_ref[...] = out.astype(o_ref.dtype)


def _s5_mixer(proj3, lr, li, ldt, bre, bim, cre, cim, dsk, wglu, bglu):
    b, s, _ = proj3.shape
    full = lambda shape: pl.BlockSpec(shape, lambda i, t: (0,) * len(shape))
    return pl.pallas_call(
        _s5_kernel,
        grid=(b, s // TS_S5),
        in_specs=[
            pl.BlockSpec((None, TS_S5, S5_WIDTH), lambda i, t: (i, t, OFF_B // S5_WIDTH)),
            full((1, S5_N)), full((1, S5_N)), full((1, S5_N)),
            full((S5_WIDTH, S5_N)), full((S5_WIDTH, S5_N)),
            full((S5_N, S5_WIDTH)), full((S5_N, S5_WIDTH)),
            full((1, S5_WIDTH)), full((S5_WIDTH, S5_WIDTH)), full((1, S5_WIDTH)),
        ],
        out_specs=pl.BlockSpec((None, TS_S5, S5_WIDTH), lambda i, t: (i, t, 0)),
        out_shape=jax.ShapeDtypeStruct((b, s, S5_WIDTH), BF16),
        scratch_shapes=[
            pltpu.VMEM((TS_S5, S5_N), F32),
            pltpu.VMEM((TS_S5, S5_N), F32),
            pltpu.VMEM((10, SUBLANES, S5_N), F32),
            pltpu.VMEM((2, SUBLANES, S5_N), F32),
        ],
        compiler_params=_cparams(("parallel", "arbitrary")),
        name="s5_mixer",
    )(proj3, lr, li, ldt, bre, bim, cre, cim, dsk, wglu, bglu)


def _merge_kernel(x_ref, oa_ref, ob_ref, oc_ref, ga_ref, gb_ref, gc_ref, wa_ref, wb_ref, wc_ref, wo_ref,
                  gn_ref, *rest, with_router):
    if with_router:
        rt_ref, xo_ref, h_ref, lg_ref = rest
    else:
        xo_ref, h_ref = rest

    def sig(r):
        return jax.nn.sigmoid(r[...].astype(F32))

    merged = (sig(ga_ref) * _dot(oa_ref[...], wa_ref[...])
              + sig(gb_ref) * _dot(ob_ref[...], wb_ref[...])
              + sig(gc_ref) * _dot(oc_ref[...], wc_ref[...]))
    xn = x_ref[...] + _dot(merged.astype(BF16), wo_ref[...])
    xo_ref[...] = xn
    h = _rms(xn, gn_ref[...])
    h_hi = h.astype(BF16)
    h_ref[...] = h_hi
    if with_router:
        h_lo = (h - h_hi.astype(F32)).astype(BF16)
        rt = rt_ref[...]
        r_hi = rt.astype(BF16)
        r_lo = (rt - r_hi.astype(F32)).astype(BF16)
        lg_ref[...] = _dot_nt(r_hi, h_hi) + (_dot_nt(r_hi, h_lo) + _dot_nt(r_lo, h_hi))


def _merge(x, oa, ob, oc, proj, wa, wb, wc, wo, gn, router_t=None):
    t, d = x.shape
    tm = TM_MERGE
    with_router = router_t is not None
    row = lambda w: pl.BlockSpec((tm, w), lambda i: (i, 0))
    full = lambda a: pl.BlockSpec(a.shape, lambda i: (0, 0))
    in_specs = [row(d), row(A_WIDTH), row(S5_WIDTH), row(C_WIDTH)]
    in_specs += [pl.BlockSpec((tm, d), lambda i, c=c: (i, c)) for c in range(3)]
    in_specs += [full(wa), full(wb), full(wc), full(wo), full(gn)]
    args = [x, oa, ob, oc, proj, proj, proj, wa, wb, wc, wo, gn]
    out_specs = [row(d), row(d)]
    out_shape = [jax.ShapeDtypeStruct((t, d), F32), jax.ShapeDtypeStruct((t, d), BF16)]
    if with_router:
        in_specs.append(full(router_t))
        args.append(router_t)
        out_specs.append(pl.BlockSpec((N_EXPERTS, tm), lambda i: (0, i)))
        out_shape.append(jax.ShapeDtypeStruct((N_EXPERTS, t), F32))
    return pl.pallas_call(
        functools.partial(_merge_kernel, with_router=with_router),
        grid=(t // tm,),
        in_specs=in_specs,
        out_specs=out_specs,
        out_shape=out_shape,
        compiler_params=_cparams(("parallel",)),
        name="merge_router" if with_router else "merge",
    )(*args)


def _ffn_kernel(x_ref, h_ref, wg_ref, wu_ref, wd_ref, o_ref):
    @pl.when(pl.program_id(1) == 0)
    def _():
        o_ref[...] = x_ref[...]

    h = h_ref[...]
    act = jax.nn.silu(_dot(h, wg_ref[...])) * _dot(h, wu_ref[...])
    o_ref[...] += _dot(act.astype(BF16), wd_ref[...])


def _ffn(x, h, wg, wu, wd, li):
    t, d = x.shape
    ff = wg.shape[2]
    return pl.pallas_call(
        _ffn_kernel,
        grid=(t // TM_FFN, ff // TF_FFN),
        in_specs=[
            pl.BlockSpec((TM_FFN, d), lambda i, f: (i, 0)),
            pl.BlockSpec((TM_FFN, d), lambda i, f: (i, 0)),
            pl.BlockSpec((None, d, TF_FFN), lambda i, f: (li, 0, f)),
            pl.BlockSpec((None, d, TF_FFN), lambda i, f: (li, 0, f)),
            pl.BlockSpec((None, TF_FFN, d), lambda i, f: (li, f, 0)),
        ],
        out_specs=pl.BlockSpec((TM_FFN, d), lambda i, f: (i, 0)),
        out_shape=jax.ShapeDtypeStruct((t, d), F32),
        compiler_params=_cparams(("parallel", "arbitrary")),
        name="dense_ffn",
    )(x, h, wg, wu, wd)


def _route_kernel(lg_ref, gate_ref, rank_ref, cnt_ref):
    lg = lg_ref[...]
    tm = lg.shape[1]
    eidx = lax.broadcasted_iota(jnp.int32, lg.shape, 0)
    m1 = jnp.max(lg, axis=0, keepdims=True)
    i1 = jnp.min(jnp.where(lg == m1, eidx, N_EXPERTS), axis=0, keepdims=True)
    rest = jnp.where(eidx == i1, -jnp.inf, lg)
    m2 = jnp.max(rest, axis=0, keepdims=True)
    i2 = jnp.min(jnp.where(rest == m2, eidx, N_EXPERTS), axis=0, keepdims=True)
    e2 = jnp.exp(m2 - m1)
    tot = 1.0 + e2
    sel1 = eidx == i1
    sel2 = eidx == i2
    gate_ref[...] = jnp.where(sel1, 1.0 / tot, jnp.where(sel2, e2 / tot, 0.0))
    sel = (sel1 | sel2).astype(jnp.int32)
    lane = lax.broadcasted_iota(jnp.int32, lg.shape, 1)
    c = sel
    sft = 1
    while sft < tm:
        c = c + jnp.where(lane >= sft, pltpu.roll(c, sft, 1), 0)
        sft *= 2
    rank_ref[...] = jnp.where(sel > 0, c - 1, -1)
    cnt_ref[...] = jnp.broadcast_to(jnp.sum(sel, axis=1, keepdims=True), cnt_ref.shape)


def _route(logits_t, tm):
    e, t = logits_t.shape
    nt = t // tm
    return pl.pallas_call(
        _route_kernel,
        grid=(nt,),
        in_specs=[pl.BlockSpec((e, tm), lambda i: (0, i))],
        out_specs=[
            pl.BlockSpec((e, tm), lambda i: (0, i)),
            pl.BlockSpec((e, tm), lambda i: (0, i)),
            pl.BlockSpec((None, e, LANES), lambda i: (i, 0, 0)),
        ],
        out_shape=[
            jax.ShapeDtypeStruct((e, t), F32),
            jax.ShapeDtypeStruct((e, t), jnp.int32),
            jax.ShapeDtypeStruct((nt, e, LANES), jnp.int32),
        ],
        compiler_params=_cparams(("parallel",)),
        name="route_top2",
    )(logits_t)


def _moe_kernel(cnt_ref, x_ref, h_ref, rrow_ref, grow_ref, wg_ref, wu_ref, wd_ref, gfin_ref,
                o_ref, hc_ref, y_ref, *, final_norm):
    i, e, f = pl.program_id(0), pl.program_id(1), pl.program_id(2)
    ne, nf = pl.num_programs(1), pl.num_programs(2)
    tm = h_ref.shape[0]
    n_sel = cnt_ref[i * N_EXPERTS + e]
    n_chunks = (n_sel + (CH_MOE - 1)) // CH_MOE
    k_path = jnp.clip(n_chunks, KMIN_MOE, KMAX_MOE)
    n_over = jnp.maximum(n_chunks - KMAX_MOE, 0)

    def expert_part(hc):
        act = jax.nn.silu(_dot(hc, wg_ref[...])) * _dot(hc, wu_ref[...])
        return _dot(act.astype(BF16), wd_ref[...])

    def picks(r0, rows):
        slot = lax.broadcasted_iota(jnp.int32, (rows, tm), 0) + r0
        return slot == rrow_ref[...]

    def compact(r0, rows):
        onehot = jnp.where(picks(r0, rows), 1.0, 0.0).astype(BF16)
        return _dot(onehot, h_ref[...]).astype(BF16)

    def scatter_add(r0, y):
        sel = picks(r0, y.shape[0])
        gate_c = jnp.sum(jnp.where(sel, grow_ref[...], 0.0), axis=1, keepdims=True)
        yb = (y * gate_c).astype(BF16)
        onehot = jnp.where(sel, 1.0, 0.0).astype(BF16)
        for q in range(tm // SC_MOE):
            cols = slice(q * SC_MOE, (q + 1) * SC_MOE)
            o_ref[pl.ds(q * SC_MOE, SC_MOE), :] += lax.dot_general(
                onehot[:, cols], yb, (((0,), (0,)), ((), ())), preferred_element_type=F32)

    @pl.when((e == 0) & (f == 0))
    def _():
        o_ref[...] = x_ref[...]

    def static_path(rows):
        sl = pl.ds(0, rows)

        @pl.when(f == 0)
        def _():
            hc_ref[sl, :] = compact(0, rows)
            y_ref[sl, :] = jnp.zeros((rows, D_MODEL), F32)

        y_ref[sl, :] += expert_part(hc_ref[sl, :])

        @pl.when(f == nf - 1)
        def _():
            scatter_add(0, y_ref[sl, :])

    for k in range(KMIN_MOE, KMAX_MOE + 1):
        pl.when(k_path == k)(functools.partial(static_path, k * CH_MOE))

    def overflow(c, carry):
        r0 = (KMAX_MOE + c) * CH_MOE
        scatter_add(r0, expert_part(compact(r0, CH_MOE)))
        return carry

    lax.fori_loop(0, n_over, overflow, 0)

    if final_norm:
        @pl.when((e == ne - 1) & (f == nf - 1))
        def _():
            o_ref[...] = _rms(o_ref[...], gfin_ref[...])


def _moe(x, h, counts, rank_row, gate_row, wg, wu, wd, li, tm, final_gain=None):
    final_norm = final_gain is not None
    if not final_norm:
        final_gain = jnp.ones((1, x.shape[1]), F32)
    t, d = x.shape
    _, ne, _, ff = wg.shape
    nt = t // tm
    once = pl.Buffered(1)
    grid_spec = pltpu.PrefetchScalarGridSpec(
        num_scalar_prefetch=1,
        grid=(nt, ne, ff // TF_MOE),
        in_specs=[
            pl.BlockSpec((tm, d), lambda i, e, f, c: (i, 0), pipeline_mode=once),
            pl.BlockSpec((tm, d), lambda i, e, f, c: (i, 0), pipeline_mode=once),
            pl.BlockSpec((None, 1, tm), lambda i, e, f, c: (e, 0, i)),
            pl.BlockSpec((None, 1, tm), lambda i, e, f, c: (e, 0, i)),
            pl.BlockSpec((None, None, d, TF_MOE), lambda i, e, f, c: (li, e, 0, f)),
            pl.BlockSpec((None, None, d, TF_MOE), lambda i, e, f, c: (li, e, 0, f)),
            pl.BlockSpec((None, None, TF_MOE, d), lambda i, e, f, c: (li, e, f, 0)),
            pl.BlockSpec((1, d), lambda i, e, f, c: (0, 0)),
        ],
        out_specs=pl.BlockSpec((tm, d), lambda i, e, f, c: (i, 0)),
        scratch_shapes=[pltpu.VMEM((KMAX_MOE * CH_MOE, d), BF16), pltpu.VMEM((KMAX_MOE * CH_MOE, d), F32)],
    )
    return pl.pallas_call(
        functools.partial(_moe_kernel, final_norm=final_norm),
        grid_spec=grid_spec,
        out_shape=jax.ShapeDtypeStruct((t, d), F32),
        compiler_params=pltpu.CompilerParams(dimension_semantics=("parallel", "arbitrary", "arbitrary"),
                                             vmem_limit_bytes=VMEM_LIMIT_MOE),
        name="moe_experts",
    )(counts, x, h, rank_row, gate_row, wg, wu, wd, final_gain)


def _final_norm_kernel(x_ref, g_ref, o_ref):
    o_ref[...] = _rms(x_ref[...], g_ref[...])


def _final_norm(x, g):
    t, d = x.shape
    tm = TM_PROJ
    return pl.pallas_call(
        _final_norm_kernel,
        grid=(t // tm,),
        in_specs=[pl.BlockSpec((tm, d), lambda i: (i, 0)), pl.BlockSpec((1, d), lambda i: (0, 0))],
        out_specs=pl.BlockSpec((tm, d), lambda i: (i, 0)),
        out_shape=jax.ShapeDtypeStruct((t, d), F32),
        compiler_params=_cparams(("parallel",)),
        name="final_norm",
    )(x, g)


def _block_diag(blocks):
    l, g, r, c = blocks.shape
    on_diag = jnp.eye(g, dtype=bool)[None, :, None, :, None]
    out = jnp.where(on_diag, blocks[:, :, :, None, :], jnp.zeros((), blocks.dtype))
    return out.reshape(l, g * r, g * c)


def kernel(x, norm_mix, w_in, s5_lambda_re, s5_lambda_im, s5_log_dt, s5_b_re, s5_b_im, s5_c_re, s5_c_im,
           s5_d, s5_w_glu, s5_b_glu, c_sinks, w_branch_a, w_branch_b, w_branch_c, w_out, norm_ffn,
           ffn_w_gate, ffn_w_up, ffn_w_down, moe_router, moe_w_gate, moe_w_up, moe_w_down, norm_final):
    b, s, d = x.shape
    depth = w_in.shape[0]
    t = b * s
    n_split = A_QKV + S5_WIDTH + C_WIDTH + 2 * C_KV_HEADS * HEAD_DIM

    w_in_p = jnp.concatenate([w_in[:, :, n_split:], w_in[:, :, :n_split]], axis=-1).astype(BF16)
    wa, wb, wc, wo = (w.astype(BF16) for w in (w_branch_a, w_branch_b, w_branch_c, w_out))
    fg, fu, fd = (w.astype(BF16) for w in (ffn_w_gate, ffn_w_up, ffn_w_down))
    mg, mu, md = (w.astype(BF16) for w in (moe_w_gate, moe_w_up, moe_w_down))
    router_t = jnp.swapaxes(moe_router, 1, 2)
    lam_re = s5_lambda_re.reshape(depth, 1, S5_N)
    lam_im = s5_lambda_im.reshape(depth, 1, S5_N)
    log_dt = jnp.repeat(s5_log_dt, S5_STATE, axis=-1).reshape(depth, 1, S5_N)
    bre = _block_diag(jnp.swapaxes(s5_b_re, 2, 3)).astype(BF16)
    bim = _block_diag(jnp.swapaxes(s5_b_im, 2, 3)).astype(BF16)
    cre = _block_diag(jnp.swapaxes(s5_c_re, 2, 3)).astype(BF16)
    cim = _block_diag(jnp.swapaxes(s5_c_im, 2, 3)).astype(BF16)
    wglu = s5_w_glu.astype(BF16)
    cos, sin = _rope_tables(s)

    xt = x.reshape(t, d)
    for l in range(depth):
        proj = _inproj(xt, norm_mix[l][None, :], w_in_p, l)
        proj3 = proj.reshape(b, s, IN_COLS)
        o_a = _dilated_mixer(proj3).reshape(t, A_WIDTH)
        o_b = _s5_mixer(proj3, lam_re[l], lam_im[l], log_dt[l], bre[l], bim[l], cre[l], cim[l],
                        s5_d[l][None, :], wglu[l], s5_b_glu[l][None, :]).reshape(t, S5_WIDTH)
        o_c = _swa_mixer(proj3, c_sinks[l], cos, sin).reshape(t, C_WIDTH)
        i = l // 2
        if l % 2 == 0:
            xt, h = _merge(xt, o_a, o_b, o_c, proj, wa[l], wb[l], wc[l], wo[l], norm_ffn[l][None, :])
            xt = _ffn(xt, h, fg, fu, fd, i)
        else:
            xt, h, logits_t = _merge(xt, o_a, o_b, o_c, proj, wa[l], wb[l], wc[l], wo[l],
                                     norm_ffn[l][None, :], router_t[i])
            gate, rank, cnt = _route(logits_t, TM_MOE)
            counts = cnt[:, :, 0].reshape(-1)
            last = l == depth - 1
            xt = _moe(xt, h, counts, rank[:, None, :], gate[:, None, :], mg, mu, md, i, TM_MOE,
                      final_gain=norm_final[None, :] if last else None)
    if depth % 2 == 1:
        xt = _final_norm(xt, norm_final[None, :])
    return xt.reshape(b, s, d)
```

```python
import functools
import math

import jax
import jax.numpy as jnp
import numpy as np
from jax import lax
from jax.experimental import pallas as pl
from jax.experimental.pallas import tpu as pltpu

F32 = jnp.float32
BF16 = jnp.bfloat16

D_MODEL = 1024
HEAD_DIM = 64
BLOCK = 128
LANES = 128
SUBLANES = 8
DILATIONS = (1, 4, 16)
N_DIL = 3
A_HEADS = 4
A_WIDTH = A_HEADS * HEAD_DIM
A_QKV = 3 * N_DIL * A_WIDTH
S5_WIDTH = 256
S5_GROUPS = 16
S5_GROUP_CH = 16
S5_STATE = 64
S5_N = S5_GROUPS * S5_STATE
C_Q_HEADS = 8
C_KV_HEADS = 2
C_WIDTH = C_Q_HEADS * HEAD_DIM
ROPE_THETA = 150000.0
N_GATE = 3 * D_MODEL
IN_COLS = 6400
D_FF = 3584
N_EXPERTS = 8
RMS_EPS = 1e-6
ATT_SCALE = HEAD_DIM ** -0.5

OFF_GATE = 0
OFF_A = N_GATE
OFF_B = OFF_A + A_QKV
OFF_Q = OFF_B + S5_WIDTH
OFF_K = OFF_Q + C_WIDTH
OFF_V = OFF_K + C_KV_HEADS * HEAD_DIM

TM_PROJ = 1024
TN_PROJ = 1280
TM_MERGE = 1024
TM_FFN = 1024
TF_FFN = 512
TS_S5 = 512
TM_MOE = 2048
TF_MOE = 512
CH_MOE = 128
KMIN_MOE = 5
KMAX_MOE = 5
SC_MOE = 512
VMEM_LIMIT = 56 * 1024 * 1024
VMEM_LIMIT_MOE = 60 * 1024 * 1024


def _cparams(sem):
    return pltpu.CompilerParams(dimension_semantics=sem, vmem_limit_bytes=VMEM_LIMIT)


def _dot(a, b):
    return jnp.dot(a, b, preferred_element_type=F32)


def _dot_nt(a, b):
    return lax.dot_general(a, b, (((1,), (1,)), ((), ())), preferred_element_type=F32)


def _rms(x, g):
    return x * lax.rsqrt(jnp.mean(x * x, axis=-1, keepdims=True) + RMS_EPS) * g


def _inproj_kernel(x_ref, g_ref, w_ref, o_ref, h_ref):
    @pl.when(pl.program_id(1) == 0)
    def _():
        h_ref[...] = _rms(x_ref[...], g_ref[...]).astype(BF16)

    o_ref[...] = _dot(h_ref[...], w_ref[...]).astype(o_ref.dtype)


def _inproj(x, g, w, l):
    t, d = x.shape
    n = w.shape[2]
    return pl.pallas_call(
        _inproj_kernel,
        grid=(t // TM_PROJ, n // TN_PROJ),
        in_specs=[
            pl.BlockSpec((TM_PROJ, d), lambda i, j: (i, 0)),
            pl.BlockSpec((1, d), lambda i, j: (0, 0)),
            pl.BlockSpec((None, d, TN_PROJ), lambda i, j: (l, 0, j)),
        ],
        out_specs=pl.BlockSpec((TM_PROJ, TN_PROJ), lambda i, j: (i, j)),
        out_shape=jax.ShapeDtypeStruct((t, n), BF16),
        scratch_shapes=[pltpu.VMEM((TM_PROJ, d), BF16)],
        compiler_params=_cparams(("parallel", "arbitrary")),
        name="inproj",
    )(x, g, w)


def _band_mask(rows, width):
    qi = lax.broadcasted_iota(jnp.int32, (rows, width), 0) % BLOCK
    ki = lax.broadcasted_iota(jnp.int32, (rows, width), 1)
    return (ki >= qi) & (ki <= qi + BLOCK)


def _causal_mask(rows, width):
    qi = lax.broadcasted_iota(jnp.int32, (rows, width), 0) % BLOCK
    ki = lax.broadcasted_iota(jnp.int32, (rows, width), 1)
    return ki <= qi


def _softmax_pv(s, mask, v, sink=None):
    s = jnp.where(mask, s, -jnp.inf)
    m = jnp.max(s, axis=-1, keepdims=True)
    if sink is not None:
        m = jnp.maximum(m, sink)
    p = jnp.exp(s - m)
    den = jnp.sum(p, axis=-1, keepdims=True)
    if sink is not None:
        den = den + jnp.exp(sink - m)
    o = _dot(p.astype(BF16), v)
    return o, m, den


def _dilated_kernel(q0_ref, q1_ref, q2_ref, k0_ref, k1_ref, k2_ref, v0_ref, v1_ref, v2_ref,
                    o_ref, qf_ref, kf_ref, vf_ref, og_ref, lg_ref):
    seq = o_ref.shape[0]
    lane = lax.broadcasted_iota(jnp.int32, (1, LANES), 1)
    lo = lane < HEAD_DIM
    band = _band_mask(2 * BLOCK, 2 * BLOCK)
    causal = _causal_mask(2 * BLOCK, BLOCK)

    def attend(q, k, v, mask):
        zero = jnp.zeros_like(q)
        qs = jnp.concatenate([jnp.where(lo, q, zero), jnp.where(lo, zero, q)], axis=0)
        s = _dot_nt(qs, k)
        o, m, den = _softmax_pv(s, mask, v)
        o = o * (1.0 / den)
        lse = m + jnp.log(den)
        o_pair = jnp.where(lo, o[:BLOCK], o[BLOCK:])
        l_pair = jnp.where(lo, lse[:BLOCK], lse[BLOCK:])
        return o_pair, l_pair

    for idx, (qr, kr, vr) in enumerate(((q1_ref, k1_ref, v1_ref), (q2_ref, k2_ref, v2_ref))):
        qf_ref[idx] = qr[...].astype(F32) * ATT_SCALE
        kf_ref[idx] = kr[...].astype(F32)
        vf_ref[idx] = vr[...].astype(F32)

    o_p, l_p = attend(q0_ref[pl.ds(0, BLOCK), :] * ATT_SCALE, k0_ref[pl.ds(0, BLOCK), :],
                      v0_ref[pl.ds(0, BLOCK), :], causal)
    og_ref[0, pl.ds(0, BLOCK), :] = o_p
    lg_ref[0, pl.ds(0, BLOCK), :] = l_p

    for n in range(1, seq // BLOCK):
        r0, w0 = n * BLOCK, (n - 1) * BLOCK
        o_p, l_p = attend(q0_ref[pl.ds(r0, BLOCK), :] * ATT_SCALE, k0_ref[pl.ds(w0, 2 * BLOCK), :],
                          v0_ref[pl.ds(w0, 2 * BLOCK), :], band)
        og_ref[0, pl.ds(r0, BLOCK), :] = o_p
        lg_ref[0, pl.ds(r0, BLOCK), :] = l_p

    for idx, d in ((0, DILATIONS[1]), (1, DILATIONS[2])):
        g = idx + 1
        nblk = seq // d // BLOCK
        for r in range(d):
            for n in range(nblk):
                q = qf_ref[idx, pl.ds(r + n * BLOCK * d, BLOCK, stride=d), :].astype(BF16)
                if n == 0:
                    k = kf_ref[idx, pl.ds(r, BLOCK, stride=d), :].astype(BF16)
                    v = vf_ref[idx, pl.ds(r, BLOCK, stride=d), :].astype(BF16)
                    o_p, l_p = attend(q, k, v, causal)
                else:
                    w0 = r + (n - 1) * BLOCK * d
                    k = kf_ref[idx, pl.ds(w0, 2 * BLOCK, stride=d), :].astype(BF16)
                    v = vf_ref[idx, pl.ds(w0, 2 * BLOCK, stride=d), :].astype(BF16)
                    o_p, l_p = attend(q, k, v, band)
                og_ref[g, pl.ds(r + n * BLOCK * d, BLOCK, stride=d), :] = o_p
                lg_ref[g, pl.ds(r + n * BLOCK * d, BLOCK, stride=d), :] = l_p

    l0, l1, l2 = lg_ref[0], lg_ref[1], lg_ref[2]
    mx = jnp.maximum(jnp.maximum(l0, l1), l2)
    e0, e1, e2 = jnp.exp(l0 - mx), jnp.exp(l1 - mx), jnp.exp(l2 - mx)
    tot = e0 + e1 + e2
    o_ref[...] = ((e0 * og_ref[0] + e1 * og_ref[1] + e2 * og_ref[2]) / tot).astype(o_ref.dtype)


def _dilated_mixer(proj3):
    b, s, _ = proj3.shape
    base = OFF_A // LANES

    def spec(which, g):
        col = base + which * (N_DIL * A_WIDTH // LANES) + g * (A_WIDTH // LANES)
        return pl.BlockSpec((None, s, LANES), lambda i, hp, col=col: (i, 0, col + hp))

    in_specs = [spec(w, g) for w in range(3) for g in range(N_DIL)]
    return pl.pallas_call(
        _dilated_kernel,
        grid=(b, A_WIDTH // LANES),
        in_specs=in_specs,
        out_specs=pl.BlockSpec((None, s, LANES), lambda i, hp: (i, 0, hp)),
        out_shape=jax.ShapeDtypeStruct((b, s, A_WIDTH), BF16),
        scratch_shapes=[
            pltpu.VMEM((2, s, LANES), F32),
            pltpu.VMEM((2, s, LANES), F32),
            pltpu.VMEM((2, s, LANES), F32),
            pltpu.VMEM((N_DIL, s, LANES), F32),
            pltpu.VMEM((N_DIL, s, LANES), F32),
        ],
        compiler_params=_cparams(("parallel", "parallel")),
        name="dilated_mixer",
    )(*([proj3] * 9))


def _swa_kernel(sink_ref, q_ref, k_ref, v_ref, cos_ref, sin_ref, o_ref, qs_ref, ks_ref, vs_ref):
    seq = o_ref.shape[0]
    g = pl.program_id(1)
    rep = C_Q_HEADS // C_KV_HEADS
    lane = lax.broadcasted_iota(jnp.int32, (1, LANES), 1)
    lo = lane < HEAD_DIM
    cos = cos_ref[...]
    sin = sin_ref[...]

    src = lax.broadcasted_iota(jnp.int32, (LANES, LANES), 0)
    dst = lax.broadcasted_iota(jnp.int32, (LANES, LANES), 1)
    half = HEAD_DIM // 2
    partner = jnp.where((dst % HEAD_DIM) < half, dst + half, dst - half)
    swap_mat = jnp.where(src == partner, 1.0, 0.0).astype(BF16)
    rep_mat = jnp.where(src == g * HEAD_DIM + dst % HEAD_DIM, 1.0, 0.0).astype(BF16)

    def rope(x):
        return x.astype(F32) * cos + _dot(x, swap_mat) * sin

    ks_ref[...] = _dot(rope(k_ref[...]).astype(BF16), rep_mat).astype(BF16)
    vs_ref[...] = _dot(v_ref[...], rep_mat).astype(BF16)
    for c in range(rep // 2):
        sl = slice(c * LANES, (c + 1) * LANES)
        qs_ref[:, sl] = (rope(q_ref[:, sl]) * ATT_SCALE).astype(BF16)

    rows = rep * BLOCK
    hrow = lax.broadcasted_iota(jnp.int32, (rows, 1), 0) // BLOCK
    sink = jnp.zeros((rows, 1), F32)
    for h in range(rep):
        sink = jnp.where(hrow == h, sink_ref[g * rep + h], sink)
    band = _band_mask(rows, 2 * BLOCK)
    causal = _causal_mask(rows, BLOCK)

    def block(r0, k, v, mask):
        parts = []
        for h in range(rep):
            q = qs_ref[pl.ds(r0, BLOCK), (h // 2) * LANES:(h // 2 + 1) * LANES]
            zero = jnp.zeros_like(q)
            parts.append(jnp.where(lo, q, zero) if h % 2 == 0 else jnp.where(lo, zero, q))
        s = _dot_nt(jnp.concatenate(parts, axis=0), k)
        o, _, den = _softmax_pv(s, mask, v, sink)
        o = o * (1.0 / den)
        for c in range(rep // 2):
            pair = jnp.where(lo, o[2 * c * BLOCK:(2 * c + 1) * BLOCK], o[(2 * c + 1) * BLOCK:(2 * c + 2) * BLOCK])
            o_ref[pl.ds(r0, BLOCK), c * LANES:(c + 1) * LANES] = pair.astype(o_ref.dtype)

    block(0, ks_ref[pl.ds(0, BLOCK), :], vs_ref[pl.ds(0, BLOCK), :], causal)

    def body(n, carry):
        r0 = pl.multiple_of(n * BLOCK, BLOCK)
        w0 = pl.multiple_of((n - 1) * BLOCK, BLOCK)
        block(r0, ks_ref[pl.ds(w0, 2 * BLOCK), :], vs_ref[pl.ds(w0, 2 * BLOCK), :], band)
        return carry

    lax.fori_loop(1, seq // BLOCK, body, 0, unroll=3)


def _rope_tables(seq):
    inv = ROPE_THETA ** (-jnp.arange(0, HEAD_DIM, 2, dtype=F32) / HEAD_DIM)
    ang = jnp.arange(seq, dtype=F32)[:, None] * inv[None, :]
    reps = LANES // (HEAD_DIM // 2)
    cos = jnp.tile(jnp.cos(ang), (1, reps))
    sign = jnp.where((jnp.arange(LANES) % HEAD_DIM) < HEAD_DIM // 2, -1.0, 1.0).astype(F32)
    sin = jnp.tile(jnp.sin(ang), (1, reps)) * sign[None, :]
    return cos, sin


def _swa_mixer(proj3, sinks, cos, sin):
    b, s, _ = proj3.shape
    qw = C_WIDTH // C_KV_HEADS
    return pl.pallas_call(
        _swa_kernel,
        grid=(b, C_KV_HEADS),
        in_specs=[
            pl.BlockSpec(memory_space=pltpu.SMEM),
            pl.BlockSpec((None, s, qw), lambda i, g: (i, 0, OFF_Q // qw + g)),
            pl.BlockSpec((None, s, LANES), lambda i, g: (i, 0, OFF_K // LANES)),
            pl.BlockSpec((None, s, LANES), lambda i, g: (i, 0, OFF_V // LANES)),
            pl.BlockSpec((s, LANES), lambda i, g: (0, 0)),
            pl.BlockSpec((s, LANES), lambda i, g: (0, 0)),
        ],
        out_specs=pl.BlockSpec((None, s, qw), lambda i, g: (i, 0, g)),
        out_shape=jax.ShapeDtypeStruct((b, s, C_WIDTH), BF16),
        scratch_shapes=[
            pltpu.VMEM((s, qw), BF16),
            pltpu.VMEM((s, LANES), BF16),
            pltpu.VMEM((s, LANES), BF16),
        ],
        compiler_params=_cparams(("parallel", "parallel")),
        name="swa_mixer",
    )(sinks, proj3, proj3, proj3, cos, sin)


def _cmul(ar, ai, br, bi):
    return ar * br - ai * bi, ar * bi + ai * br


def _s5_kernel(u_ref, lr_ref, li_ref, ldt_ref, bre_ref, bim_ref, cre_ref, cim_ref, d_ref, wg_ref, bg_ref,
               o_ref, hr_ref, hi_ref, pw_ref, car_ref):
    ts = u_ref.shape[0]

    @pl.when(pl.program_id(1) == 0)
    def _():
        lr, li = lr_ref[...], li_ref[...]
        dt = jnp.exp(ldt_ref[...])
        mag = jnp.exp(lr * dt)
        a_re, a_im = mag * jnp.cos(li * dt), mag * jnp.sin(li * dt)
        nr, ni = a_re - 1.0, a_im
        den = lr * lr + li * li
        z_re = (nr * lr + ni * li) / den
        z_im = (ni * lr - nr * li) / den
        row = lax.broadcasted_iota(jnp.int32, (SUBLANES, 1), 0)
        pows = [(a_re, a_im)]
        for _ in range(SUBLANES - 1):
            pows.append(_cmul(pows[-1][0], pows[-1][1], a_re, a_im))
        for j, sft in enumerate((1, 2, 4)):
            pr, pi = pows[sft - 1]
            pw_ref[2 * j] = jnp.where(row >= sft, pr, 0.0)
            pw_ref[2 * j + 1] = jnp.where(row >= sft, pi, 0.0)
        cr = jnp.zeros((SUBLANES, S5_N), F32)
        ci = jnp.zeros((SUBLANES, S5_N), F32)
        for i in range(SUBLANES):
            cr = jnp.where(row == i, pows[i][0], cr)
            ci = jnp.where(row == i, pows[i][1], ci)
        pw_ref[6] = cr
        pw_ref[7] = ci
        pw_ref[8] = jnp.broadcast_to(z_re, (SUBLANES, S5_N))
        pw_ref[9] = jnp.broadcast_to(z_im, (SUBLANES, S5_N))
        car_ref[...] = jnp.zeros_like(car_ref)

    u = u_ref[...]
    bu_re = _dot(u, bre_ref[...])
    bu_im = _dot(u, bim_ref[...])
    z_re = pw_ref[8, 0:1, :]
    z_im = pw_ref[9, 0:1, :]
    hr_ref[...] = z_re * bu_re - z_im * bu_im
    hi_ref[...] = z_re * bu_im + z_im * bu_re

    mults = [(pw_ref[2 * j], pw_ref[2 * j + 1]) for j in range(3)]
    acr, aci = pw_ref[6], pw_ref[7]

    def chunk(k, carry):
        cr, ci = carry
        r0 = pl.multiple_of(k * SUBLANES, SUBLANES)
        xr = hr_ref[pl.ds(r0, SUBLANES), :]
        xi = hi_ref[pl.ds(r0, SUBLANES), :]
        for (mr, mi), sft in zip(mults, (1, 2, 4)):
            sr = pltpu.roll(xr, sft, 0)
            si = pltpu.roll(xi, sft, 0)
            xr, xi = xr + (mr * sr - mi * si), xi + (mr * si + mi * sr)
        xr = xr + (acr * cr - aci * ci)
        xi = xi + (acr * ci + aci * cr)
        hr_ref[pl.ds(r0, SUBLANES), :] = xr
        hi_ref[pl.ds(r0, SUBLANES), :] = xi
        return xr[SUBLANES - 1:SUBLANES, :], xi[SUBLANES - 1:SUBLANES, :]

    cr, ci = lax.fori_loop(0, ts // SUBLANES, chunk, (car_ref[0, 0:1, :], car_ref[1, 0:1, :]), unroll=2)
    car_ref[0] = jnp.broadcast_to(cr, (SUBLANES, S5_N))
    car_ref[1] = jnp.broadcast_to(ci, (SUBLANES, S5_N))

    y = _dot(hr_ref[...].astype(BF16), cre_ref[...]) - _dot(hi_ref[...].astype(BF16), cim_ref[...])
    y = y + d_ref[...] * u.astype(F32)
    gl = jax.nn.gelu(y)
    out = gl * jax.nn.sigmoid(_dot(gl.astype(BF16), wg_ref[...]) + bg_ref[...])
    o_ref[...] = out.astype(o_ref.dtype)


def _s5_mixer(proj3, lr, li, ldt, bre, bim, cre, cim, dsk, wglu, bglu):
    b, s, _ = proj3.shape
    full = lambda shape: pl.BlockSpec(shape, lambda i, t: (0,) * len(shape))
    return pl.pallas_call(
        _s5_kernel,
        grid=(b, s // TS_S5),
        in_specs=[
            pl.BlockSpec((None, TS_S5, S5_WIDTH), lambda i, t: (i, t, OFF_B // S5_WIDTH)),
            full((1, S5_N)), full((1, S5_N)), full((1, S5_N)),
            full((S5_WIDTH, S5_N)), full((S5_WIDTH, S5_N)),
            full((S5_N, S5_WIDTH)), full((S5_N, S5_WIDTH)),
            full((1, S5_WIDTH)), full((S5_WIDTH, S5_WIDTH)), full((1, S5_WIDTH)),
        ],
        out_specs=pl.BlockSpec((None, TS_S5, S5_WIDTH), lambda i, t: (i, t, 0)),
        out_shape=jax.ShapeDtypeStruct((b, s, S5_WIDTH), BF16),
        scratch_shapes=[
            pltpu.VMEM((TS_S5, S5_N), F32),
            pltpu.VMEM((TS_S5, S5_N), F32),
            pltpu.VMEM((10, SUBLANES, S5_N), F32),
            pltpu.VMEM((2, SUBLANES, S5_N), F32),
        ],
        compiler_params=_cparams(("parallel", "arbitrary")),
        name="s5_mixer",
    )(proj3, lr, li, ldt, bre, bim, cre, cim, dsk, wglu, bglu)


def _merge_kernel(x_ref, oa_ref, ob_ref, oc_ref, ga_ref, gb_ref, gc_ref, wa_ref, wb_ref, wc_ref, wo_ref,
                  gn_ref, *rest, with_router):
    if with_router:
        rt_ref, xo_ref, h_ref, lg_ref = rest
    else:
        xo_ref, h_ref = rest

    def sig(r):
        return jax.nn.sigmoid(r[...].astype(F32))

    merged = (sig(ga_ref) * _dot(oa_ref[...], wa_ref[...])
              + sig(gb_ref) * _dot(ob_ref[...], wb_ref[...])
              + sig(gc_ref) * _dot(oc_ref[...], wc_ref[...]))
    xn = x_ref[...] + _dot(merged.astype(BF16), wo_ref[...])
    xo_ref[...] = xn
    h = _rms(xn, gn_ref[...])
    h_hi = h.astype(BF16)
    h_ref[...] = h_hi
    if with_router:
        h_lo = (h - h_hi.astype(F32)).astype(BF16)
        rt = rt_ref[...]
        r_hi = rt.astype(BF16)
        r_lo = (rt - r_hi.astype(F32)).astype(BF16)
        lg_ref[...] = _dot_nt(r_hi, h_hi) + (_dot_nt(r_hi, h_lo) + _dot_nt(r_lo, h_hi))


def _merge(x, oa, ob, oc, proj, wa, wb, wc, wo, gn, router_t=None):
    t, d = x.shape
    tm = TM_MERGE
    with_router = router_t is not None
    row = lambda w: pl.BlockSpec((tm, w), lambda i: (i, 0))
    full = lambda a: pl.BlockSpec(a.shape, lambda i: (0, 0))
    in_specs = [row(d), row(A_WIDTH), row(S5_WIDTH), row(C_WIDTH)]
    in_specs += [pl.BlockSpec((tm, d), lambda i, c=c: (i, c)) for c in range(3)]
    in_specs += [full(wa), full(wb), full(wc), full(wo), full(gn)]
    args = [x, oa, ob, oc, proj, proj, proj, wa, wb, wc, wo, gn]
    out_specs = [row(d), row(d)]
    out_shape = [jax.ShapeDtypeStruct((t, d), F32), jax.ShapeDtypeStruct((t, d), BF16)]
    if with_router:
        in_specs.append(full(router_t))
        args.append(router_t)
        out_specs.append(pl.BlockSpec((N_EXPERTS, tm), lambda i: (0, i)))
        out_shape.append(jax.ShapeDtypeStruct((N_EXPERTS, t), F32))
    return pl.pallas_call(
        functools.partial(_merge_kernel, with_router=with_router),
        grid=(t // tm,),
        in_specs=in_specs,
        out_specs=out_specs,
        out_shape=out_shape,
        compiler_params=_cparams(("parallel",)),
        name="merge_router" if with_router else "merge",
    )(*args)


def _ffn_kernel(x_ref, h_ref, wg_ref, wu_ref, wd_ref, o_ref):
    @pl.when(pl.program_id(1) == 0)
    def _():
        o_ref[...] = x_ref[...]

    h = h_ref[...]
    act = jax.nn.silu(_dot(h, wg_ref[...])) * _dot(h, wu_ref[...])
    o_ref[...] += _dot(act.astype(BF16), wd_ref[...])


def _ffn(x, h, wg, wu, wd, li):
    t, d = x.shape
    ff = wg.shape[2]
    return pl.pallas_call(
        _ffn_kernel,
        grid=(t // TM_FFN, ff // TF_FFN),
        in_specs=[
            pl.BlockSpec((TM_FFN, d), lambda i, f: (i, 0)),
            pl.BlockSpec((TM_FFN, d), lambda i, f: (i, 0)),
            pl.BlockSpec((None, d, TF_FFN), lambda i, f: (li, 0, f)),
            pl.BlockSpec((None, d, TF_FFN), lambda i, f: (li, 0, f)),
            pl.BlockSpec((None, TF_FFN, d), lambda i, f: (li, f, 0)),
        ],
        out_specs=pl.BlockSpec((TM_FFN, d), lambda i, f: (i, 0)),
        out_shape=jax.ShapeDtypeStruct((t, d), F32),
        compiler_params=_cparams(("parallel", "arbitrary")),
        name="dense_ffn",
    )(x, h, wg, wu, wd)


def _route_kernel(lg_ref, gate_ref, rank_ref, cnt_ref):
    lg = lg_ref[...]
    tm = lg.shape[1]
    eidx = lax.broadcasted_iota(jnp.int32, lg.shape, 0)
    m1 = jnp.max(lg, axis=0, keepdims=True)
    i1 = jnp.min(jnp.where(lg == m1, eidx, N_EXPERTS), axis=0, keepdims=True)
    rest = jnp.where(eidx == i1, -jnp.inf, lg)
    m2 = jnp.max(rest, axis=0, keepdims=True)
    i2 = jnp.min(jnp.where(rest == m2, eidx, N_EXPERTS), axis=0, keepdims=True)
    e2 = jnp.exp(m2 - m1)
    tot = 1.0 + e2
    sel1 = eidx == i1
    sel2 = eidx == i2
    gate_ref[...] = jnp.where(sel1, 1.0 / tot, jnp.where(sel2, e2 / tot, 0.0))
    sel = (sel1 | sel2).astype(jnp.int32)
    lane = lax.broadcasted_iota(jnp.int32, lg.shape, 1)
    c = sel
    sft = 1
    while sft < tm:
        c = c + jnp.where(lane >= sft, pltpu.roll(c, sft, 1), 0)
        sft *= 2
    rank_ref[...] = jnp.where(sel > 0, c - 1, -1)
    cnt_ref[...] = jnp.broadcast_to(jnp.sum(sel, axis=1, keepdims=True), cnt_ref.shape)


def _route(logits_t, tm):
    e, t = logits_t.shape
    nt = t // tm
    return pl.pallas_call(
        _route_kernel,
        grid=(nt,),
        in_specs=[pl.BlockSpec((e, tm), lambda i: (0, i))],
        out_specs=[
            pl.BlockSpec((e, tm), lambda i: (0, i)),
            pl.BlockSpec((e, tm), lambda i: (0, i)),
            pl.BlockSpec((None, e, LANES), lambda i: (i, 0, 0)),
        ],
        out_shape=[
            jax.ShapeDtypeStruct((e, t), F32),
            jax.ShapeDtypeStruct((e, t), jnp.int32),
            jax.ShapeDtypeStruct((nt, e, LANES), jnp.int32),
        ],
        compiler_params=_cparams(("parallel",)),
        name="route_top2",
    )(logits_t)


def _moe_kernel(cnt_ref, x_ref, h_ref, rrow_ref, grow_ref, wg_ref, wu_ref, wd_ref, gfin_ref,
                o_ref, hc_ref, y_ref, *, final_norm):
    i, e, f = pl.program_id(0), pl.program_id(1), pl.program_id(2)
    ne, nf = pl.num_programs(1), pl.num_programs(2)
    tm = h_ref.shape[0]
    n_sel = cnt_ref[i * N_EXPERTS + e]
    n_chunks = (n_sel + (CH_MOE - 1)) // CH_MOE
    k_path = jnp.clip(n_chunks, KMIN_MOE, KMAX_MOE)
    n_over = jnp.maximum(n_chunks - KMAX_MOE, 0)

    def expert_part(hc):
        act = jax.nn.silu(_dot(hc, wg_ref[...])) * _dot(hc, wu_ref[...])
        return _dot(act.astype(BF16), wd_ref[...])

    def picks(r0, rows):
        slot = lax.broadcasted_iota(jnp.int32, (rows, tm), 0) + r0
        return slot == rrow_ref[...]

    def compact(r0, rows):
        onehot = jnp.where(picks(r0, rows), 1.0, 0.0).astype(BF16)
        return _dot(onehot, h_ref[...]).astype(BF16)

    def scatter_add(r0, y):
        sel = picks(r0, y.shape[0])
        gate_c = jnp.sum(jnp.where(sel, grow_ref[...], 0.0), axis=1, keepdims=True)
        yb = (y * gate_c).astype(BF16)
        onehot = jnp.where(sel, 1.0, 0.0).astype(BF16)
        for q in range(tm // SC_MOE):
            cols = slice(q * SC_MOE, (q + 1) * SC_MOE)
            o_ref[pl.ds(q * SC_MOE, SC_MOE), :] += lax.dot_general(
                onehot[:, cols], yb, (((0,), (0,)), ((), ())), preferred_element_type=F32)

    @pl.when((e == 0) & (f == 0))
    def _():
        o_ref[...] = x_ref[...]

    def static_path(rows):
        sl = pl.ds(0, rows)

        @pl.when(f == 0)
        def _():
            hc_ref[sl, :] = compact(0, rows)
            y_ref[sl, :] = jnp.zeros((rows, D_MODEL), F32)

        y_ref[sl, :] += expert_part(hc_ref[sl, :])

        @pl.when(f == nf - 1)
        def _():
            scatter_add(0, y_ref[sl, :])

    for k in range(KMIN_MOE, KMAX_MOE + 1):
        pl.when(k_path == k)(functools.partial(static_path, k * CH_MOE))

    def overflow(c, carry):
        r0 = (KMAX_MOE + c) * CH_MOE
        scatter_add(r0, expert_part(compact(r0, CH_MOE)))
        return carry

    lax.fori_loop(0, n_over, overflow, 0)

    if final_norm:
        @pl.when((e == ne - 1) & (f == nf - 1))
        def _():
            o_ref[...] = _rms(o_ref[...], gfin_ref[...])


def _moe(x, h, counts, rank_row, gate_row, wg, wu, wd, li, tm, final_gain=None):
    final_norm = final_gain is not None
    if not final_norm:
        final_gain = jnp.ones((1, x.shape[1]), F32)
    t, d = x.shape
    _, ne, _, ff = wg.shape
    nt = t // tm
    once = pl.Buffered(1)
    grid_spec = pltpu.PrefetchScalarGridSpec(
        num_scalar_prefetch=1,
        grid=(nt, ne, ff // TF_MOE),
        in_specs=[
            pl.BlockSpec((tm, d), lambda i, e, f, c: (i, 0), pipeline_mode=once),
            pl.BlockSpec((tm, d), lambda i, e, f, c: (i, 0), pipeline_mode=once),
            pl.BlockSpec((None, 1, tm), lambda i, e, f, c: (e, 0, i)),
            pl.BlockSpec((None, 1, tm), lambda i, e, f, c: (e, 0, i)),
            pl.BlockSpec((None, None, d, TF_MOE), lambda i, e, f, c: (li, e, 0, f)),
            pl.BlockSpec((None, None, d, TF_MOE), lambda i, e, f, c: (li, e, 0, f)),
            pl.BlockSpec((None, None, TF_MOE, d), lambda i, e, f, c: (li, e, f, 0)),
            pl.BlockSpec((1, d), lambda i, e, f, c: (0, 0)),
        ],
        out_specs=pl.BlockSpec((tm, d), lambda i, e, f, c: (i, 0)),
        scratch_shapes=[pltpu.VMEM((KMAX_MOE * CH_MOE, d), BF16), pltpu.VMEM((KMAX_MOE * CH_MOE, d), F32)],
    )
    return pl.pallas_call(
        functools.partial(_moe_kernel, final_norm=final_norm),
        grid_spec=grid_spec,
        out_shape=jax.ShapeDtypeStruct((t, d), F32),
        compiler_params=pltpu.CompilerParams(dimension_semantics=("parallel", "arbitrary", "arbitrary"),
                                             vmem_limit_bytes=VMEM_LIMIT_MOE),
        name="moe_experts",
    )(counts, x, h, rank_row, gate_row, wg, wu, wd, final_gain)


def _final_norm_kernel(x_ref, g_ref, o_ref):
    o_ref[...] = _rms(x_ref[...], g_ref[...])


def _final_norm(x, g):
    t, d = x.shape
    tm = TM_PROJ
    return pl.pallas_call(
        _final_norm_kernel,
        grid=(t // tm,),
        in_specs=[pl.BlockSpec((tm, d), lambda i: (i, 0)), pl.BlockSpec((1, d), lambda i: (0, 0))],
        out_specs=pl.BlockSpec((tm, d), lambda i: (i, 0)),
        out_shape=jax.ShapeDtypeStruct((t, d), F32),
        compiler_params=_cparams(("parallel",)),
        name="final_norm",
    )(x, g)


def _block_diag(blocks):
    l, g, r, c = blocks.shape
    on_diag = jnp.eye(g, dtype=bool)[None, :, None, :, None]
    out = jnp.where(on_diag, blocks[:, :, :, None, :], jnp.zeros((), blocks.dtype))
    return out.reshape(l, g * r, g * c)


def kernel(x, norm_mix, w_in, s5_lambda_re, s5_lambda_im, s5_log_dt, s5_b_re, s5_b_im, s5_c_re, s5_c_im,
           s5_d, s5_w_glu, s5_b_glu, c_sinks, w_branch_a, w_branch_b, w_branch_c, w_out, norm_ffn,
           ffn_w_gate, ffn_w_up, ffn_w_down, moe_router, moe_w_gate, moe_w_up, moe_w_down, norm_final):
    b, s, d = x.shape
    depth = w_in.shape[0]
    t = b * s
    n_split = A_QKV + S5_WIDTH + C_WIDTH + 2 * C_KV_HEADS * HEAD_DIM

    w_in_p = jnp.concatenate([w_in[:, :, n_split:], w_in[:, :, :n_split]], axis=-1).astype(BF16)
    wa, wb, wc, wo = (w.astype(BF16) for w in (w_branch_a, w_branch_b, w_branch_c, w_out))
    fg, fu, fd = (w.astype(BF16) for w in (ffn_w_gate, ffn_w_up, ffn_w_down))
    mg, mu, md = (w.astype(BF16) for w in (moe_w_gate, moe_w_up, moe_w_down))
    router_t = jnp.swapaxes(moe_router, 1, 2)
    lam_re = s5_lambda_re.reshape(depth, 1, S5_N)
    lam_im = s5_lambda_im.reshape(depth, 1, S5_N)
    log_dt = jnp.repeat(s5_log_dt, S5_STATE, axis=-1).reshape(depth, 1, S5_N)
    bre = _block_diag(jnp.swapaxes(s5_b_re, 2, 3)).astype(BF16)
    bim = _block_diag(jnp.swapaxes(s5_b_im, 2, 3)).astype(BF16)
    cre = _block_diag(jnp.swapaxes(s5_c_re, 2, 3)).astype(BF16)
    cim = _block_diag(jnp.swapaxes(s5_c_im, 2, 3)).astype(BF16)
    wglu = s5_w_glu.astype(BF16)
    cos, sin = _rope_tables(s)

    xt = x.reshape(t, d)
    for l in range(depth):
        proj = _inproj(xt, norm_mix[l][None, :], w_in_p, l)
        proj3 = proj.reshape(b, s, IN_COLS)
        o_a = _dilated_mixer(proj3).reshape(t, A_WIDTH)
        o_b = _s5_mixer(proj3, lam_re[l], lam_im[l], log_dt[l], bre[l], bim[l], cre[l], cim[l],
                        s5_d[l][None, :], wglu[l], s5_b_glu[l][None, :]).reshape(t, S5_WIDTH)
        o_c = _swa_mixer(proj3, c_sinks[l], cos, sin).reshape(t, C_WIDTH)
        i = l // 2
        if l % 2 == 0:
            xt, h = _merge(xt, o_a, o_b, o_c, proj, wa[l], wb[l], wc[l], wo[l], norm_ffn[l][None, :])
            xt = _ffn(xt, h, fg, fu, fd, i)
        else:
            xt, h, logits_t = _merge(xt, o_a, o_b, o_c, proj, wa[l], wb[l], wc[l], wo[l],
                                     norm_ffn[l][None, :], router_t[i])
            gate, rank, cnt = _route(logits_t, TM_MOE)
            counts = cnt[:, :, 0].reshape(-1)
            last = l == depth - 1
            xt = _moe(xt, h, counts, rank[:, None, :], gate[:, None, :], mg, mu, md, i, TM_MOE,
                      final_gain=norm_final[None, :] if last else None)
    if depth % 2 == 1:
        xt = _final_norm(xt, norm_final[None, :])
    return xt.reshape(b, s, d)
```

```python
import functools
import math

import jax
import jax.numpy as jnp
import numpy as np
from jax import lax
from jax.experimental import pallas as pl
from jax.experimental.pallas import tpu as pltpu

F32 = jnp.float32
BF16 = jnp.bfloat16

D_MODEL = 1024
HEAD_DIM = 64
BLOCK = 128
LANES = 128
SUBLANES = 8
DILATIONS = (1, 4, 16)
N_DIL = 3
A_HEADS = 4
A_WIDTH = A_HEADS * HEAD_DIM
A_QKV = 3 * N_DIL * A_WIDTH
S5_WIDTH = 256
S5_GROUPS = 16
S5_GROUP_CH = 16
S5_STATE = 64
S5_N = S5_GROUPS * S5_STATE
C_Q_HEADS = 8
C_KV_HEADS = 2
C_WIDTH = C_Q_HEADS * HEAD_DIM
ROPE_THETA = 150000.0
N_GATE = 3 * D_MODEL
IN_COLS = 6400
D_FF = 3584
N_EXPERTS = 8
RMS_EPS = 1e-6
ATT_SCALE = HEAD_DIM ** -0.5

OFF_GATE = 0
OFF_A = N_GATE
OFF_B = OFF_A + A_QKV
OFF_Q = OFF_B + S5_WIDTH
OFF_K = OFF_Q + C_WIDTH
OFF_V = OFF_K + C_KV_HEADS * HEAD_DIM

TM_PROJ = 2048
TN_PROJ = 1280
TM_MERGE = 1024
TM_FFN = 1024
TF_FFN = 512
TS_S5 = 512
TM_MOE = 2048
TF_MOE = 512
CH_MOE = 64
KMIN_MOE = 6
KMAX_MOE = 12
SC_MOE = 512
VMEM_LIMIT = 56 * 1024 * 1024
VMEM_LIMIT_MOE = 60 * 1024 * 1024


def _cparams(sem):
    return pltpu.CompilerParams(dimension_semantics=sem, vmem_limit_bytes=VMEM_LIMIT)


def _dot(a, b):
    return jnp.dot(a, b, preferred_element_type=F32)


def _dot_nt(a, b):
    return lax.dot_general(a, b, (((1,), (1,)), ((), ())), preferred_element_type=F32)


def _rms(x, g):
    return x * lax.rsqrt(jnp.mean(x * x, axis=-1, keepdims=True) + RMS_EPS) * g


def _inproj_kernel(x_ref, g_ref, w_ref, o_ref, h_ref):
    @pl.when(pl.program_id(1) == 0)
    def _():
        h_ref[...] = _rms(x_ref[...], g_ref[...]).astype(BF16)

    o_ref[...] = _dot(h_ref[...], w_ref[...]).astype(o_ref.dtype)


def _inproj(x, g, w, l):
    t, d = x.shape
    n = w.shape[2]
    return pl.pallas_call(
        _inproj_kernel,
        grid=(t // TM_PROJ, n // TN_PROJ),
        in_specs=[
            pl.BlockSpec((TM_PROJ, d), lambda i, j: (i, 0)),
            pl.BlockSpec((1, d), lambda i, j: (0, 0)),
            pl.BlockSpec((None, d, TN_PROJ), lambda i, j: (l, 0, j)),
        ],
        out_specs=pl.BlockSpec((TM_PROJ, TN_PROJ), lambda i, j: (i, j)),
        out_shape=jax.ShapeDtypeStruct((t, n), BF16),
        scratch_shapes=[pltpu.VMEM((TM_PROJ, d), BF16)],
        compiler_params=_cparams(("parallel", "arbitrary")),
        name="inproj",
    )(x, g, w)


def _band_mask(rows, width):
    qi = lax.broadcasted_iota(jnp.int32, (rows, width), 0) % BLOCK
    ki = lax.broadcasted_iota(jnp.int32, (rows, width), 1)
    return (ki >= qi) & (ki <= qi + BLOCK)


def _causal_mask(rows, width):
    qi = lax.broadcasted_iota(jnp.int32, (rows, width), 0) % BLOCK
    ki = lax.broadcasted_iota(jnp.int32, (rows, width), 1)
    return ki <= qi


def _softmax_pv(s, mask, v, sink=None):
    s = jnp.where(mask, s, -jnp.inf)
    m = jnp.max(s, axis=-1, keepdims=True)
    if sink is not None:
        m = jnp.maximum(m, sink)
    p = jnp.exp(s - m)
    den = jnp.sum(p, axis=-1, keepdims=True)
    if sink is not None:
        den = den + jnp.exp(sink - m)
    o = _dot(p.astype(BF16), v)
    return o, m, den


def _dilated_kernel(q0_ref, q1_ref, q2_ref, k0_ref, k1_ref, k2_ref, v0_ref, v1_ref, v2_ref,
                    o_ref, qf_ref, kf_ref, vf_ref, og_ref, lg_ref):
    seq = o_ref.shape[0]
    lane = lax.broadcasted_iota(jnp.int32, (1, LANES), 1)
    lo = lane < HEAD_DIM
    band = _band_mask(2 * BLOCK, 2 * BLOCK)
    causal = _causal_mask(2 * BLOCK, BLOCK)

    def attend(q, k, v, mask):
        zero = jnp.zeros_like(q)
        qs = jnp.concatenate([jnp.where(lo, q, zero), jnp.where(lo, zero, q)], axis=0)
        s = _dot_nt(qs, k)
        o, m, den = _softmax_pv(s, mask, v)
        o = o * (1.0 / den)
        lse = m + jnp.log(den)
        o_pair = jnp.where(lo, o[:BLOCK], o[BLOCK:])
        l_pair = jnp.where(lo, lse[:BLOCK], lse[BLOCK:])
        return o_pair, l_pair

    for idx, (qr, kr, vr) in enumerate(((q1_ref, k1_ref, v1_ref), (q2_ref, k2_ref, v2_ref))):
        qf_ref[idx] = qr[...].astype(F32) * ATT_SCALE
        kf_ref[idx] = kr[...].astype(F32)
        vf_ref[idx] = vr[...].astype(F32)

    o_p, l_p = attend(q0_ref[pl.ds(0, BLOCK), :] * ATT_SCALE, k0_ref[pl.ds(0, BLOCK), :],
                      v0_ref[pl.ds(0, BLOCK), :], causal)
    og_ref[0, pl.ds(0, BLOCK), :] = o_p
    lg_ref[0, pl.ds(0, BLOCK), :] = l_p

    for n in range(1, seq // BLOCK):
        r0, w0 = n * BLOCK, (n - 1) * BLOCK
        o_p, l_p = attend(q0_ref[pl.ds(r0, BLOCK), :] * ATT_SCALE, k0_ref[pl.ds(w0, 2 * BLOCK), :],
                          v0_ref[pl.ds(w0, 2 * BLOCK), :], band)
        og_ref[0, pl.ds(r0, BLOCK), :] = o_p
        lg_ref[0, pl.ds(r0, BLOCK), :] = l_p

    for idx, d in ((0, DILATIONS[1]), (1, DILATIONS[2])):
        g = idx + 1
        nblk = seq // d // BLOCK
        for r in range(d):
            for n in range(nblk):
                q = qf_ref[idx, pl.ds(r + n * BLOCK * d, BLOCK, stride=d), :].astype(BF16)
                if n == 0:
                    k = kf_ref[idx, pl.ds(r, BLOCK, stride=d), :].astype(BF16)
                    v = vf_ref[idx, pl.ds(r, BLOCK, stride=d), :].astype(BF16)
                    o_p, l_p = attend(q, k, v, causal)
                else:
                    w0 = r + (n - 1) * BLOCK * d
                    k = kf_ref[idx, pl.ds(w0, 2 * BLOCK, stride=d), :].astype(BF16)
                    v = vf_ref[idx, pl.ds(w0, 2 * BLOCK, stride=d), :].astype(BF16)
                    o_p, l_p = attend(q, k, v, band)
                og_ref[g, pl.ds(r + n * BLOCK * d, BLOCK, stride=d), :] = o_p
                lg_ref[g, pl.ds(r + n * BLOCK * d, BLOCK, stride=d), :] = l_p

    l0, l1, l2 = lg_ref[0], lg_ref[1], lg_ref[2]
    mx = jnp.maximum(jnp.maximum(l0, l1), l2)
    e0, e1, e2 = jnp.exp(l0 - mx), jnp.exp(l1 - mx), jnp.exp(l2 - mx)
    tot = e0 + e1 + e2
    o_ref[...] = ((e0 * og_ref[0] + e1 * og_ref[1] + e2 * og_ref[2]) / tot).astype(o_ref.dtype)


def _dilated_mixer(proj3):
    b, s, _ = proj3.shape
    base = OFF_A // LANES

    def spec(which, g):
        col = base + which * (N_DIL * A_WIDTH // LANES) + g * (A_WIDTH // LANES)
        return pl.BlockSpec((None, s, LANES), lambda i, hp, col=col: (i, 0, col + hp))

    in_specs = [spec(w, g) for w in range(3) for g in range(N_DIL)]
    return pl.pallas_call(
        _dilated_kernel,
        grid=(b, A_WIDTH // LANES),
        in_specs=in_specs,
        out_specs=pl.BlockSpec((None, s, LANES), lambda i, hp: (i, 0, hp)),
        out_shape=jax.ShapeDtypeStruct((b, s, A_WIDTH), BF16),
        scratch_shapes=[
            pltpu.VMEM((2, s, LANES), F32),
            pltpu.VMEM((2, s, LANES), F32),
            pltpu.VMEM((2, s, LANES), F32),
            pltpu.VMEM((N_DIL, s, LANES), F32),
            pltpu.VMEM((N_DIL, s, LANES), F32),
        ],
        compiler_params=_cparams(("parallel", "parallel")),
        name="dilated_mixer",
    )(*([proj3] * 9))


def _swa_kernel(sink_ref, q_ref, k_ref, v_ref, cos_ref, sin_ref, o_ref, qs_ref, ks_ref, vs_ref):
    seq = o_ref.shape[0]
    g = pl.program_id(1)
    rep = C_Q_HEADS // C_KV_HEADS
    lane = lax.broadcasted_iota(jnp.int32, (1, LANES), 1)
    lo = lane < HEAD_DIM
    cos = cos_ref[...]
    sin = sin_ref[...]

    src = lax.broadcasted_iota(jnp.int32, (LANES, LANES), 0)
    dst = lax.broadcasted_iota(jnp.int32, (LANES, LANES), 1)
    half = HEAD_DIM // 2
    partner = jnp.where((dst % HEAD_DIM) < half, dst + half, dst - half)
    swap_mat = jnp.where(src == partner, 1.0, 0.0).astype(BF16)
    rep_mat = jnp.where(src == g * HEAD_DIM + dst % HEAD_DIM, 1.0, 0.0).astype(BF16)

    def rope(x):
        return x.astype(F32) * cos + _dot(x, swap_mat) * sin

    ks_ref[...] = _dot(rope(k_ref[...]).astype(BF16), rep_mat).astype(BF16)
    vs_ref[...] = _dot(v_ref[...], rep_mat).astype(BF16)
    for c in range(rep // 2):
        sl = slice(c * LANES, (c + 1) * LANES)
        qs_ref[:, sl] = (rope(q_ref[:, sl]) * ATT_SCALE).astype(BF16)

    rows = rep * BLOCK
    hrow = lax.broadcasted_iota(jnp.int32, (rows, 1), 0) // BLOCK
    sink = jnp.zeros((rows, 1), F32)
    for h in range(rep):
        sink = jnp.where(hrow == h, sink_ref[g * rep + h], sink)
    band = _band_mask(rows, 2 * BLOCK)
    causal = _causal_mask(rows, BLOCK)

    def block(r0, k, v, mask):
        parts = []
        for h in range(rep):
            q = qs_ref[pl.ds(r0, BLOCK), (h // 2) * LANES:(h // 2 + 1) * LANES]
            zero = jnp.zeros_like(q)
            parts.append(jnp.where(lo, q, zero) if h % 2 == 0 else jnp.where(lo, zero, q))
        s = _dot_nt(jnp.concatenate(parts, axis=0), k)
        o, _, den = _softmax_pv(s, mask, v, sink)
        o = o * (1.0 / den)
        for c in range(rep // 2):
            pair = jnp.where(lo, o[2 * c * BLOCK:(2 * c + 1) * BLOCK], o[(2 * c + 1) * BLOCK:(2 * c + 2) * BLOCK])
            o_ref[pl.ds(r0, BLOCK), c * LANES:(c + 1) * LANES] = pair.astype(o_ref.dtype)

    block(0, ks_ref[pl.ds(0, BLOCK), :], vs_ref[pl.ds(0, BLOCK), :], causal)

    def body(n, carry):
        r0 = pl.multiple_of(n * BLOCK, BLOCK)
        w0 = pl.multiple_of((n - 1) * BLOCK, BLOCK)
        block(r0, ks_ref[pl.ds(w0, 2 * BLOCK), :], vs_ref[pl.ds(w0, 2 * BLOCK), :], band)
        return carry

    lax.fori_loop(1, seq // BLOCK, body, 0, unroll=3)


def _rope_tables(seq):
    inv = ROPE_THETA ** (-jnp.arange(0, HEAD_DIM, 2, dtype=F32) / HEAD_DIM)
    ang = jnp.arange(seq, dtype=F32)[:, None] * inv[None, :]
    reps = LANES // (HEAD_DIM // 2)
    cos = jnp.tile(jnp.cos(ang), (1, reps))
    sign = jnp.where((jnp.arange(LANES) % HEAD_DIM) < HEAD_DIM // 2, -1.0, 1.0).astype(F32)
    sin = jnp.tile(jnp.sin(ang), (1, reps)) * sign[None, :]
    return cos, sin


def _swa_mixer(proj3, sinks, cos, sin):
    b, s, _ = proj3.shape
    qw = C_WIDTH // C_KV_HEADS
    return pl.pallas_call(
        _swa_kernel,
        grid=(b, C_KV_HEADS),
        in_specs=[
            pl.BlockSpec(memory_space=pltpu.SMEM),
            pl.BlockSpec((None, s, qw), lambda i, g: (i, 0, OFF_Q // qw + g)),
            pl.BlockSpec((None, s, LANES), lambda i, g: (i, 0, OFF_K // LANES)),
            pl.BlockSpec((None, s, LANES), lambda i, g: (i, 0, OFF_V // LANES)),
            pl.BlockSpec((s, LANES), lambda i, g: (0, 0)),
            pl.BlockSpec((s, LANES), lambda i, g: (0, 0)),
        ],
        out_specs=pl.BlockSpec((None, s, qw), lambda i, g: (i, 0, g)),
        out_shape=jax.ShapeDtypeStruct((b, s, C_WIDTH), BF16),
        scratch_shapes=[
            pltpu.VMEM((s, qw), BF16),
            pltpu.VMEM((s, LANES), BF16),
            pltpu.VMEM((s, LANES), BF16),
        ],
        compiler_params=_cparams(("parallel", "parallel")),
        name="swa_mixer",
    )(sinks, proj3, proj3, proj3, cos, sin)


def _cmul(ar, ai, br, bi):
    return ar * br - ai * bi, ar * bi + ai * br


def _s5_kernel(u_ref, lr_ref, li_ref, ldt_ref, bre_ref, bim_ref, cre_ref, cim_ref, d_ref, wg_ref, bg_ref,
               o_ref, hr_ref, hi_ref, pw_ref, car_ref):
    ts = u_ref.shape[0]

    @pl.when(pl.program_id(1) == 0)
    def _():
        lr, li = lr_ref[...], li_ref[...]
        dt = jnp.exp(ldt_ref[...])
        mag = jnp.exp(lr * dt)
        a_re, a_im = mag * jnp.cos(li * dt), mag * jnp.sin(li * dt)
        nr, ni = a_re - 1.0, a_im
        den = lr * lr + li * li
        z_re = (nr * lr + ni * li) / den
        z_im = (ni * lr - nr * li) / den
        row = lax.broadcasted_iota(jnp.int32, (SUBLANES, 1), 0)
        pows = [(a_re, a_im)]
        for _ in range(SUBLANES - 1):
            pows.append(_cmul(pows[-1][0], pows[-1][1], a_re, a_im))
        for j, sft in enumerate((1, 2, 4)):
            pr, pi = pows[sft - 1]
            pw_ref[2 * j] = jnp.where(row >= sft, pr, 0.0)
            pw_ref[2 * j + 1] = jnp.where(row >= sft, pi, 0.0)
        cr = jnp.zeros((SUBLANES, S5_N), F32)
        ci = jnp.zeros((SUBLANES, S5_N), F32)
        for i in range(SUBLANES):
            cr = jnp.where(row == i, pows[i][0], cr)
            ci = jnp.where(row == i, pows[i][1], ci)
        pw_ref[6] = cr
        pw_ref[7] = ci
        pw_ref[8] = jnp.broadcast_to(z_re, (SUBLANES, S5_N))
        pw_ref[9] = jnp.broadcast_to(z_im, (SUBLANES, S5_N))
        car_ref[...] = jnp.zeros_like(car_ref)

    u = u_ref[...]
    bu_re = _dot(u, bre_ref[...])
    bu_im = _dot(u, bim_ref[...])
    z_re = pw_ref[8, 0:1, :]
    z_im = pw_ref[9, 0:1, :]
    hr_ref[...] = z_re * bu_re - z_im * bu_im
    hi_ref[...] = z_re * bu_im + z_im * bu_re

    mults = [(pw_ref[2 * j], pw_ref[2 * j + 1]) for j in range(3)]
    acr, aci = pw_ref[6], pw_ref[7]

    def chunk(k, carry):
        cr, ci = carry
        r0 = pl.multiple_of(k * SUBLANES, SUBLANES)
        xr = hr_ref[pl.ds(r0, SUBLANES), :]
        xi = hi_ref[pl.ds(r0, SUBLANES), :]
        for (mr, mi), sft in zip(mults, (1, 2, 4)):
            sr = pltpu.roll(xr, sft, 0)
            si = pltpu.roll(xi, sft, 0)
            xr, xi = xr + (mr * sr - mi * si), xi + (mr * si + mi * sr)
        xr = xr + (acr * cr - aci * ci)
        xi = xi + (acr * ci + aci * cr)
        hr_ref[pl.ds(r0, SUBLANES), :] = xr
        hi_ref[pl.ds(r0, SUBLANES), :] = xi
        return xr[SUBLANES - 1:SUBLANES, :], xi[SUBLANES - 1:SUBLANES, :]

    cr, ci = lax.fori_loop(0, ts // SUBLANES, chunk, (car_ref[0, 0:1, :], car_ref[1, 0:1, :]), unroll=2)
    car_ref[0] = jnp.broadcast_to(cr, (SUBLANES, S5_N))
    car_ref[1] = jnp.broadcast_to(ci, (SUBLANES, S5_N))

    y = _dot(hr_ref[...].astype(BF16), cre_ref[...]) - _dot(hi_ref[...].astype(BF16), cim_ref[...])
    y = y + d_ref[...] * u.astype(F32)
    gl = jax.nn.gelu(y)
    out = gl * jax.nn.sigmoid(_dot(gl.astype(BF16), wg_ref[...]) + bg_ref[...])
    o_ref[...] = out.astype(o_ref.dtype)


def _s5_mixer(proj3, lr, li, ldt, bre, bim, cre, cim, dsk, wglu, bglu):
    b, s, _ = proj3.shape
    full = lambda shape: pl.BlockSpec(shape, lambda i, t: (0,) * len(shape))
    return pl.pallas_call(
        _s5_kernel,
        grid=(b, s // TS_S5),
        in_specs=[
            pl.BlockSpec((None, TS_S5, S5_WIDTH), lambda i, t: (i, t, OFF_B // S5_WIDTH)),
            full((1, S5_N)), full((1, S5_N)), full((1, S5_N)),
            full((S5_WIDTH, S5_N)), full((S5_WIDTH, S5_N)),
            full((S5_N, S5_WIDTH)), full((S5_N, S5_WIDTH)),
            full((1, S5_WIDTH)), full((S5_WIDTH, S5_WIDTH)), full((1, S5_WIDTH)),
        ],
        out_specs=pl.BlockSpec((None, TS_S5, S5_WIDTH), lambda i, t: (i, t, 0)),
        out_shape=jax.ShapeDtypeStruct((b, s, S5_WIDTH), BF16),
        scratch_shapes=[
            pltpu.VMEM((TS_S5, S5_N), F32),
            pltpu.VMEM((TS_S5, S5_N), F32),
            pltpu.VMEM((10, SUBLANES, S5_N), F32),
            pltpu.VMEM((2, SUBLANES, S5_N), F32),
        ],
        compiler_params=_cparams(("parallel", "arbitrary")),
        name="s5_mixer",
    )(proj3, lr, li, ldt, bre, bim, cre, cim, dsk, wglu, bglu)


def _merge_kernel(x_ref, oa_ref, ob_ref, oc_ref, ga_ref, gb_ref, gc_ref, wa_ref, wb_ref, wc_ref, wo_ref,
                  gn_ref, *rest, with_router):
    if with_router:
        rt_ref, xo_ref, h_ref, lg_ref = rest
    else:
        xo_ref, h_ref = rest

    def sig(r):
        return jax.nn.sigmoid(r[...].astype(F32))

    merged = (sig(ga_ref) * _dot(oa_ref[...], wa_ref[...])
              + sig(gb_ref) * _dot(ob_ref[...], wb_ref[...])
              + sig(gc_ref) * _dot(oc_ref[...], wc_ref[...]))
    xn = x_ref[...] + _dot(merged.astype(BF16), wo_ref[...])
    xo_ref[...] = xn
    h = _rms(xn, gn_ref[...])
    h_hi = h.astype(BF16)
    h_ref[...] = h_hi
    if with_router:
        h_lo = (h - h_hi.astype(F32)).astype(BF16)
        rt = rt_ref[...]
        r_hi = rt.astype(BF16)
        r_lo = (rt - r_hi.astype(F32)).astype(BF16)
        lg_ref[...] = _dot_nt(r_hi, h_hi) + (_dot_nt(r_hi, h_lo) + _dot_nt(r_lo, h_hi))


def _merge(x, oa, ob, oc, proj, wa, wb, wc, wo, gn, router_t=None):
    t, d = x.shape
    tm = TM_MERGE
    with_router = router_t is not None
    row = lambda w: pl.BlockSpec((tm, w), lambda i: (i, 0))
    full = lambda a: pl.BlockSpec(a.shape, lambda i: (0, 0))
    in_specs = [row(d), row(A_WIDTH), row(S5_WIDTH), row(C_WIDTH)]
    in_specs += [pl.BlockSpec((tm, d), lambda i, c=c: (i, c)) for c in range(3)]
    in_specs += [full(wa), full(wb), full(wc), full(wo), full(gn)]
    args = [x, oa, ob, oc, proj, proj, proj, wa, wb, wc, wo, gn]
    out_specs = [row(d), row(d)]
    out_shape = [jax.ShapeDtypeStruct((t, d), F32), jax.ShapeDtypeStruct((t, d), BF16)]
    if with_router:
        in_specs.append(full(router_t))
        args.append(router_t)
        out_specs.append(pl.BlockSpec((N_EXPERTS, tm), lambda i: (0, i)))
        out_shape.append(jax.ShapeDtypeStruct((N_EXPERTS, t), F32))
    return pl.pallas_call(
        functools.partial(_merge_kernel, with_router=with_router),
        grid=(t // tm,),
        in_specs=in_specs,
        out_specs=out_specs,
        out_shape=out_shape,
        compiler_params=_cparams(("parallel",)),
        name="merge_router" if with_router else "merge",
    )(*args)


def _ffn_kernel(x_ref, h_ref, wg_ref, wu_ref, wd_ref, o_ref):
    @pl.when(pl.program_id(1) == 0)
    def _():
        o_ref[...] = x_ref[...]

    h = h_ref[...]
    act = jax.nn.silu(_dot(h, wg_ref[...])) * _dot(h, wu_ref[...])
    o_ref[...] += _dot(act.astype(BF16), wd_ref[...])


def _ffn(x, h, wg, wu, wd, li):
    t, d = x.shape
    ff = wg.shape[2]
    return pl.pallas_call(
        _ffn_kernel,
        grid=(t // TM_FFN, ff // TF_FFN),
        in_specs=[
            pl.BlockSpec((TM_FFN, d), lambda i, f: (i, 0)),
            pl.BlockSpec((TM_FFN, d), lambda i, f: (i, 0)),
            pl.BlockSpec((None, d, TF_FFN), lambda i, f: (li, 0, f)),
            pl.BlockSpec((None, d, TF_FFN), lambda i, f: (li, 0, f)),
            pl.BlockSpec((None, TF_FFN, d), lambda i, f: (li, f, 0)),
        ],
        out_specs=pl.BlockSpec((TM_FFN, d), lambda i, f: (i, 0)),
        out_shape=jax.ShapeDtypeStruct((t, d), F32),
        compiler_params=_cparams(("parallel", "arbitrary")),
        name="dense_ffn",
    )(x, h, wg, wu, wd)


def _route_kernel(lg_ref, gate_ref, rank_ref, cnt_ref):
    lg = lg_ref[...]
    tm = lg.shape[1]
    eidx = lax.broadcasted_iota(jnp.int32, lg.shape, 0)
    m1 = jnp.max(lg, axis=0, keepdims=True)
    i1 = jnp.min(jnp.where(lg == m1, eidx, N_EXPERTS), axis=0, keepdims=True)
    rest = jnp.where(eidx == i1, -jnp.inf, lg)
    m2 = jnp.max(rest, axis=0, keepdims=True)
    i2 = jnp.min(jnp.where(rest == m2, eidx, N_EXPERTS), axis=0, keepdims=True)
    e2 = jnp.exp(m2 - m1)
    tot = 1.0 + e2
    sel1 = eidx == i1
    sel2 = eidx == i2
    gate_ref[...] = jnp.where(sel1, 1.0 / tot, jnp.where(sel2, e2 / tot, 0.0))
    sel = (sel1 | sel2).astype(jnp.int32)
    lane = lax.broadcasted_iota(jnp.int32, lg.shape, 1)
    c = sel
    sft = 1
    while sft < tm:
        c = c + jnp.where(lane >= sft, pltpu.roll(c, sft, 1), 0)
        sft *= 2
    rank_ref[...] = jnp.where(sel > 0, c - 1, -1)
    cnt_ref[...] = jnp.broadcast_to(jnp.sum(sel, axis=1, keepdims=True), cnt_ref.shape)


def _route(logits_t, tm):
    e, t = logits_t.shape
    nt = t // tm
    return pl.pallas_call(
        _route_kernel,
        grid=(nt,),
        in_specs=[pl.BlockSpec((e, tm), lambda i: (0, i))],
        out_specs=[
            pl.BlockSpec((e, tm), lambda i: (0, i)),
            pl.BlockSpec((e, tm), lambda i: (0, i)),
            pl.BlockSpec((None, e, LANES), lambda i: (i, 0, 0)),
        ],
        out_shape=[
            jax.ShapeDtypeStruct((e, t), F32),
            jax.ShapeDtypeStruct((e, t), jnp.int32),
            jax.ShapeDtypeStruct((nt, e, LANES), jnp.int32),
        ],
        compiler_params=_cparams(("parallel",)),
        name="route_top2",
    )(logits_t)


def _moe_kernel(cnt_ref, x_ref, h_ref, rrow_ref, grow_ref, wg_ref, wu_ref, wd_ref, gfin_ref,
                o_ref, hc_ref, y_ref, *, final_norm):
    i, e, f = pl.program_id(0), pl.program_id(1), pl.program_id(2)
    ne, nf = pl.num_programs(1), pl.num_programs(2)
    tm = h_ref.shape[0]
    n_sel = cnt_ref[i * N_EXPERTS + e]
    n_chunks = (n_sel + (CH_MOE - 1)) // CH_MOE
    k_path = jnp.clip(n_chunks, KMIN_MOE, KMAX_MOE)
    n_over = jnp.maximum(n_chunks - KMAX_MOE, 0)

    def expert_part(hc):
        act = jax.nn.silu(_dot(hc, wg_ref[...])) * _dot(hc, wu_ref[...])
        return _dot(act.astype(BF16), wd_ref[...])

    def picks(r0, rows):
        slot = lax.broadcasted_iota(jnp.int32, (rows, tm), 0) + r0
        return slot == rrow_ref[...]

    def compact(r0, rows):
        onehot = jnp.where(picks(r0, rows), 1.0, 0.0).astype(BF16)
        return _dot(onehot, h_ref[...]).astype(BF16)

    def scatter_add(r0, y):
        sel = picks(r0, y.shape[0])
        gate_c = jnp.sum(jnp.where(sel, grow_ref[...], 0.0), axis=1, keepdims=True)
        yb = (y * gate_c).astype(BF16)
        onehot = jnp.where(sel, 1.0, 0.0).astype(BF16)
        for q in range(tm // SC_MOE):
            cols = slice(q * SC_MOE, (q + 1) * SC_MOE)
            o_ref[pl.ds(q * SC_MOE, SC_MOE), :] += lax.dot_general(
                onehot[:, cols], yb, (((0,), (0,)), ((), ())), preferred_element_type=F32)

    @pl.when((e == 0) & (f == 0))
    def _():
        o_ref[...] = x_ref[...]

    def static_path(rows):
        sl = pl.ds(0, rows)

        @pl.when(f == 0)
        def _():
            hc_ref[sl, :] = compact(0, rows)
            y_ref[sl, :] = jnp.zeros((rows, D_MODEL), F32)

        y_ref[sl, :] += expert_part(hc_ref[sl, :])

        @pl.when(f == nf - 1)
        def _():
            scatter_add(0, y_ref[sl, :])

    for k in range(KMIN_MOE, KMAX_MOE + 1):
        pl.when(k_path == k)(functools.partial(static_path, k * CH_MOE))

    def overflow(c, carry):
        r0 = (KMAX_MOE + c) * CH_MOE
        scatter_add(r0, expert_part(compact(r0, CH_MOE)))
        return carry

    lax.fori_loop(0, n_over, overflow, 0)

    if final_norm:
        @pl.when((e == ne - 1) & (f == nf - 1))
        def _():
            o_ref[...] = _rms(o_ref[...], gfin_ref[...])


def _moe(x, h, counts, rank_row, gate_row, wg, wu, wd, li, tm, final_gain=None):
    final_norm = final_gain is not None
    if not final_norm:
        final_gain = jnp.ones((1, x.shape[1]), F32)
    t, d = x.shape
    _, ne, _, ff = wg.shape
    nt = t // tm
    once = pl.Buffered(1)
    grid_spec = pltpu.PrefetchScalarGridSpec(
        num_scalar_prefetch=1,
        grid=(nt, ne, ff // TF_MOE),
        in_specs=[
            pl.BlockSpec((tm, d), lambda i, e, f, c: (i, 0), pipeline_mode=once),
            pl.BlockSpec((tm, d), lambda i, e, f, c: (i, 0)),
            pl.BlockSpec((None, 1, tm), lambda i, e, f, c: (e, 0, i)),
            pl.BlockSpec((None, 1, tm), lambda i, e, f, c: (e, 0, i)),
            pl.BlockSpec((None, None, d, TF_MOE), lambda i, e, f, c: (li, e, 0, f)),
            pl.BlockSpec((None, None, d, TF_MOE), lambda i, e, f, c: (li, e, 0, f)),
            pl.BlockSpec((None, None, TF_MOE, d), lambda i, e, f, c: (li, e, f, 0)),
            pl.BlockSpec((1, d), lambda i, e, f, c: (0, 0)),
        ],
        out_specs=pl.BlockSpec((tm, d), lambda i, e, f, c: (i, 0)),
        scratch_shapes=[pltpu.VMEM((KMAX_MOE * CH_MOE, d), BF16), pltpu.VMEM((KMAX_MOE * CH_MOE, d), F32)],
    )
    return pl.pallas_call(
        functools.partial(_moe_kernel, final_norm=final_norm),
        grid_spec=grid_spec,
        out_shape=jax.ShapeDtypeStruct((t, d), F32),
        compiler_params=pltpu.CompilerParams(dimension_semantics=("parallel", "arbitrary", "arbitrary"),
                                             vmem_limit_bytes=VMEM_LIMIT_MOE),
        name="moe_experts",
    )(counts, x, h, rank_row, gate_row, wg, wu, wd, final_gain)


def _final_norm_kernel(x_ref, g_ref, o_ref):
    o_ref[...] = _rms(x_ref[...], g_ref[...])


def _final_norm(x, g):
    t, d = x.shape
    tm = TM_PROJ
    return pl.pallas_call(
        _final_norm_kernel,
        grid=(t // tm,),
        in_specs=[pl.BlockSpec((tm, d), lambda i: (i, 0)), pl.BlockSpec((1, d), lambda i: (0, 0))],
        out_specs=pl.BlockSpec((tm, d), lambda i: (i, 0)),
        out_shape=jax.ShapeDtypeStruct((t, d), F32),
        compiler_params=_cparams(("parallel",)),
        name="final_norm",
    )(x, g)


def _block_diag(blocks):
    l, g, r, c = blocks.shape
    on_diag = jnp.eye(g, dtype=bool)[None, :, None, :, None]
    out = jnp.where(on_diag, blocks[:, :, :, None, :], jnp.zeros((), blocks.dtype))
    return out.reshape(l, g * r, g * c)


def kernel(x, norm_mix, w_in, s5_lambda_re, s5_lambda_im, s5_log_dt, s5_b_re, s5_b_im, s5_c_re, s5_c_im,
           s5_d, s5_w_glu, s5_b_glu, c_sinks, w_branch_a, w_branch_b, w_branch_c, w_out, norm_ffn,
           ffn_w_gate, ffn_w_up, ffn_w_down, moe_router, moe_w_gate, moe_w_up, moe_w_down, norm_final):
    b, s, d = x.shape
    depth = w_in.shape[0]
    t = b * s
    n_split = A_QKV + S5_WIDTH + C_WIDTH + 2 * C_KV_HEADS * HEAD_DIM

    w_in_p = jnp.concatenate([w_in[:, :, n_split:], w_in[:, :, :n_split]], axis=-1).astype(BF16)
    wa, wb, wc, wo = (w.astype(BF16) for w in (w_branch_a, w_branch_b, w_branch_c, w_out))
    fg, fu, fd = (w.astype(BF16) for w in (ffn_w_gate, ffn_w_up, ffn_w_down))
    mg, mu, md = (w.astype(BF16) for w in (moe_w_gate, moe_w_up, moe_w_down))
    router_t = jnp.swapaxes(moe_router, 1, 2)
    lam_re = s5_lambda_re.reshape(depth, 1, S5_N)
    lam_im = s5_lambda_im.reshape(depth, 1, S5_N)
    log_dt = jnp.repeat(s5_log_dt, S5_STATE, axis=-1).reshape(depth, 1, S5_N)
    bre = _block_diag(jnp.swapaxes(s5_b_re, 2, 3)).astype(BF16)
    bim = _block_diag(jnp.swapaxes(s5_b_im, 2, 3)).astype(BF16)
    cre = _block_diag(jnp.swapaxes(s5_c_re, 2, 3)).astype(BF16)
    cim = _block_diag(jnp.swapaxes(s5_c_im, 2, 3)).astype(BF16)
    wglu = s5_w_glu.astype(BF16)
    cos, sin = _rope_tables(s)

    xt = x.reshape(t, d)
    for l in range(depth):
        proj = _inproj(xt, norm_mix[l][None, :], w_in_p, l)
        proj3 = proj.reshape(b, s, IN_COLS)
        o_a = _dilated_mixer(proj3).reshape(t, A_WIDTH)
        o_b = _s5_mixer(proj3, lam_re[l], lam_im[l], log_dt[l], bre[l], bim[l], cre[l], cim[l],
                        s5_d[l][None, :], wglu[l], s5_b_glu[l][None, :]).reshape(t, S5_WIDTH)
        o_c = _swa_mixer(proj3, c_sinks[l], cos, sin).reshape(t, C_WIDTH)
        i = l // 2
        if l % 2 == 0:
            xt, h = _merge(xt, o_a, o_b, o_c, proj, wa[l], wb[l], wc[l], wo[l], norm_ffn[l][None, :])
            xt = _ffn(xt, h, fg, fu, fd, i)
        else:
            xt, h, logits_t = _merge(xt, o_a, o_b, o_c, proj, wa[l], wb[l], wc[l], wo[l],
                                     norm_ffn[l][None, :], router_t[i])
            gate, rank, cnt = _route(logits_t, TM_MOE)
            counts = cnt[:, :, 0].reshape(-1)
            last = l == depth - 1
            xt = _moe(xt, h, counts, rank[:, None, :], gate[:, None, :], mg, mu, md, i, TM_MOE,
                      final_gain=norm_final[None, :] if last else None)
    if depth % 2 == 1:
        xt = _final_norm(xt, norm_final[None, :])
    return xt.reshape(b, s, d)
```

```python
import functools
import math

import jax
import jax.numpy as jnp
import numpy as np
from jax import lax
from jax.experimental import pallas as pl
from jax.experimental.pallas import tpu as pltpu

F32 = jnp.float32
BF16 = jnp.bfloat16

D_MODEL = 1024
HEAD_DIM = 64
BLOCK = 128
LANES = 128
SUBLANES = 8
DILATIONS = (1, 4, 16)
N_DIL = 3
A_HEADS = 4
A_WIDTH = A_HEADS * HEAD_DIM
A_QKV = 3 * N_DIL * A_WIDTH
S5_WIDTH = 256
S5_GROUPS = 16
S5_GROUP_CH = 16
S5_STATE = 64
S5_N = S5_GROUPS * S5_STATE
C_Q_HEADS = 8
C_KV_HEADS = 2
C_WIDTH = C_Q_HEADS * HEAD_DIM
ROPE_THETA = 150000.0
N_GATE = 3 * D_MODEL
IN_COLS = 6400
D_FF = 3584
N_EXPERTS = 8
RMS_EPS = 1e-6
ATT_SCALE = HEAD_DIM ** -0.5

OFF_GATE = 0
OFF_A = N_GATE
OFF_B = OFF_A + A_QKV
OFF_Q = OFF_B + S5_WIDTH
OFF_K = OFF_Q + C_WIDTH
OFF_V = OFF_K + C_KV_HEADS * HEAD_DIM

TM_PROJ = 2048
TN_PROJ = 1280
TM_MERGE = 1024
TM_FFN = 1024
TF_FFN = 512
TS_S5 = 512
TM_MOE = 2048
TF_MOE = 512
CH_MOE = 128
KMIN_MOE = 4
KMAX_MOE = 6
SC_MOE = 512
VMEM_LIMIT = 56 * 1024 * 1024
VMEM_LIMIT_MOE = 60 * 1024 * 1024


def _cparams(sem):
    return pltpu.CompilerParams(dimension_semantics=sem, vmem_limit_bytes=VMEM_LIMIT)


def _dot(a, b):
    return jnp.dot(a, b, preferred_element_type=F32)


def _dot_nt(a, b):
    return lax.dot_general(a, b, (((1,), (1,)), ((), ())), preferred_element_type=F32)


def _rms(x, g):
    return x * lax.rsqrt(jnp.mean(x * x, axis=-1, keepdims=True) + RMS_EPS) * g


def _inproj_kernel(x_ref, g_ref, w_ref, o_ref, h_ref):
    @pl.when(pl.program_id(1) == 0)
    def _():
        h_ref[...] = _rms(x_ref[...], g_ref[...]).astype(BF16)

    o_ref[...] = _dot(h_ref[...], w_ref[...]).astype(o_ref.dtype)


def _inproj(x, g, w, l):
    t, d = x.shape
    n = w.shape[2]
    return pl.pallas_call(
        _inproj_kernel,
        grid=(t // TM_PROJ, n // TN_PROJ),
        in_specs=[
            pl.BlockSpec((TM_PROJ, d), lambda i, j: (i, 0)),
            pl.BlockSpec((1, d), lambda i, j: (0, 0)),
            pl.BlockSpec((None, d, TN_PROJ), lambda i, j: (l, 0, j)),
        ],
        out_specs=pl.BlockSpec((TM_PROJ, TN_PROJ), lambda i, j: (i, j)),
        out_shape=jax.ShapeDtypeStruct((t, n), BF16),
        scratch_shapes=[pltpu.VMEM((TM_PROJ, d), BF16)],
        compiler_params=_cparams(("parallel", "arbitrary")),
        name="inproj",
    )(x, g, w)


def _band_mask(rows, width):
    qi = lax.broadcasted_iota(jnp.int32, (rows, width), 0) % BLOCK
    ki = lax.broadcasted_iota(jnp.int32, (rows, width), 1)
    return (ki >= qi) & (ki <= qi + BLOCK)


def _causal_mask(rows, width):
    qi = lax.broadcasted_iota(jnp.int32, (rows, width), 0) % BLOCK
    ki = lax.broadcasted_iota(jnp.int32, (rows, width), 1)
    return ki <= qi


def _softmax_pv(s, mask, v, sink=None):
    s = jnp.where(mask, s, -jnp.inf)
    m = jnp.max(s, axis=-1, keepdims=True)
    if sink is not None:
        m = jnp.maximum(m, sink)
    p = jnp.exp(s - m)
    den = jnp.sum(p, axis=-1, keepdims=True)
    if sink is not None:
        den = den + jnp.exp(sink - m)
    o = _dot(p.astype(BF16), v)
    return o, m, den


def _dilated_kernel(q0_ref, q1_ref, q2_ref, k0_ref, k1_ref, k2_ref, v0_ref, v1_ref, v2_ref,
                    o_ref, qf_ref, kf_ref, vf_ref, og_ref, lg_ref):
    seq = o_ref.shape[0]
    lane = lax.broadcasted_iota(jnp.int32, (1, LANES), 1)
    lo = lane < HEAD_DIM
    band = _band_mask(2 * BLOCK, 2 * BLOCK)
    causal = _causal_mask(2 * BLOCK, BLOCK)

    def attend(q, k, v, mask):
        zero = jnp.zeros_like(q)
        qs = jnp.concatenate([jnp.where(lo, q, zero), jnp.where(lo, zero, q)], axis=0)
        s = _dot_nt(qs, k)
        o, m, den = _softmax_pv(s, mask, v)
        o = o * (1.0 / den)
        lse = m + jnp.log(den)
        o_pair = jnp.where(lo, o[:BLOCK], o[BLOCK:])
        l_pair = jnp.where(lo, lse[:BLOCK], lse[BLOCK:])
        return o_pair, l_pair

    for idx, (qr, kr, vr) in enumerate(((q1_ref, k1_ref, v1_ref), (q2_ref, k2_ref, v2_ref))):
        qf_ref[idx] = qr[...].astype(F32) * ATT_SCALE
        kf_ref[idx] = kr[...].astype(F32)
        vf_ref[idx] = vr[...].astype(F32)

    o_p, l_p = attend(q0_ref[pl.ds(0, BLOCK), :] * ATT_SCALE, k0_ref[pl.ds(0, BLOCK), :],
                      v0_ref[pl.ds(0, BLOCK), :], causal)
    og_ref[0, pl.ds(0, BLOCK), :] = o_p
    lg_ref[0, pl.ds(0, BLOCK), :] = l_p

    for n in range(1, seq // BLOCK):
        r0, w0 = n * BLOCK, (n - 1) * BLOCK
        o_p, l_p = attend(q0_ref[pl.ds(r0, BLOCK), :] * ATT_SCALE, k0_ref[pl.ds(w0, 2 * BLOCK), :],
                          v0_ref[pl.ds(w0, 2 * BLOCK), :], band)
        og_ref[0, pl.ds(r0, BLOCK), :] = o_p
        lg_ref[0, pl.ds(r0, BLOCK), :] = l_p

    for idx, d in ((0, DILATIONS[1]), (1, DILATIONS[2])):
        g = idx + 1
        nblk = seq // d // BLOCK
        for r in range(d):
            for n in range(nblk):
                q = qf_ref[idx, pl.ds(r + n * BLOCK * d, BLOCK, stride=d), :].astype(BF16)
                if n == 0:
                    k = kf_ref[idx, pl.ds(r, BLOCK, stride=d), :].astype(BF16)
                    v = vf_ref[idx, pl.ds(r, BLOCK, stride=d), :].astype(BF16)
                    o_p, l_p = attend(q, k, v, causal)
                else:
                    w0 = r + (n - 1) * BLOCK * d
                    k = kf_ref[idx, pl.ds(w0, 2 * BLOCK, stride=d), :].astype(BF16)
                    v = vf_ref[idx, pl.ds(w0, 2 * BLOCK, stride=d), :].astype(BF16)
                    o_p, l_p = attend(q, k, v, band)
                og_ref[g, pl.ds(r + n * BLOCK * d, BLOCK, stride=d), :] = o_p
                lg_ref[g, pl.ds(r + n * BLOCK * d, BLOCK, stride=d), :] = l_p

    l0, l1, l2 = lg_ref[0], lg_ref[1], lg_ref[2]
    mx = jnp.maximum(jnp.maximum(l0, l1), l2)
    e0, e1, e2 = jnp.exp(l0 - mx), jnp.exp(l1 - mx), jnp.exp(l2 - mx)
    tot = e0 + e1 + e2
    o_ref[...] = ((e0 * og_ref[0] + e1 * og_ref[1] + e2 * og_ref[2]) / tot).astype(o_ref.dtype)


def _dilated_mixer(proj3):
    b, s, _ = proj3.shape
    base = OFF_A // LANES

    def spec(which, g):
        col = base + which * (N_DIL * A_WIDTH // LANES) + g * (A_WIDTH // LANES)
        return pl.BlockSpec((None, s, LANES), lambda i, hp, col=col: (i, 0, col + hp))

    in_specs = [spec(w, g) for w in range(3) for g in range(N_DIL)]
    return pl.pallas_call(
        _dilated_kernel,
        grid=(b, A_WIDTH // LANES),
        in_specs=in_specs,
        out_specs=pl.BlockSpec((None, s, LANES), lambda i, hp: (i, 0, hp)),
        out_shape=jax.ShapeDtypeStruct((b, s, A_WIDTH), BF16),
        scratch_shapes=[
            pltpu.VMEM((2, s, LANES), F32),
            pltpu.VMEM((2, s, LANES), F32),
            pltpu.VMEM((2, s, LANES), F32),
            pltpu.VMEM((N_DIL, s, LANES), F32),
            pltpu.VMEM((N_DIL, s, LANES), F32),
        ],
        compiler_params=_cparams(("parallel", "parallel")),
        name="dilated_mixer",
    )(*([proj3] * 9))


def _swa_kernel(sink_ref, q_ref, k_ref, v_ref, cos_ref, sin_ref, o_ref, qs_ref, ks_ref, vs_ref):
    seq = o_ref.shape[0]
    g = pl.program_id(1)
    rep = C_Q_HEADS // C_KV_HEADS
    lane = lax.broadcasted_iota(jnp.int32, (1, LANES), 1)
    lo = lane < HEAD_DIM
    cos = cos_ref[...]
    sin = sin_ref[...]

    src = lax.broadcasted_iota(jnp.int32, (LANES, LANES), 0)
    dst = lax.broadcasted_iota(jnp.int32, (LANES, LANES), 1)
    half = HEAD_DIM // 2
    partner = jnp.where((dst % HEAD_DIM) < half, dst + half, dst - half)
    swap_mat = jnp.where(src == partner, 1.0, 0.0).astype(BF16)
    rep_mat = jnp.where(src == g * HEAD_DIM + dst % HEAD_DIM, 1.0, 0.0).astype(BF16)

    def rope(x):
        return x.astype(F32) * cos + _dot(x, swap_mat) * sin

    ks_ref[...] = _dot(rope(k_ref[...]).astype(BF16), rep_mat).astype(BF16)
    vs_ref[...] = _dot(v_ref[...], rep_mat).astype(BF16)
    for c in range(rep // 2):
        sl = slice(c * LANES, (c + 1) * LANES)
        qs_ref[:, sl] = (rope(q_ref[:, sl]) * ATT_SCALE).astype(BF16)

    rows = rep * BLOCK
    hrow = lax.broadcasted_iota(jnp.int32, (rows, 1), 0) // BLOCK
    sink = jnp.zeros((rows, 1), F32)
    for h in range(rep):
        sink = jnp.where(hrow == h, sink_ref[g * rep + h], sink)
    band = _band_mask(rows, 2 * BLOCK)
    causal = _causal_mask(rows, BLOCK)

    def block(r0, k, v, mask):
        parts = []
        for h in range(rep):
            q = qs_ref[pl.ds(r0, BLOCK), (h // 2) * LANES:(h // 2 + 1) * LANES]
            zero = jnp.zeros_like(q)
            parts.append(jnp.where(lo, q, zero) if h % 2 == 0 else jnp.where(lo, zero, q))
        s = _dot_nt(jnp.concatenate(parts, axis=0), k)
        o, _, den = _softmax_pv(s, mask, v, sink)
        o = o * (1.0 / den)
        for c in range(rep // 2):
            pair = jnp.where(lo, o[2 * c * BLOCK:(2 * c + 1) * BLOCK], o[(2 * c + 1) * BLOCK:(2 * c + 2) * BLOCK])
            o_ref[pl.ds(r0, BLOCK), c * LANES:(c + 1) * LANES] = pair.astype(o_ref.dtype)

    block(0, ks_ref[pl.ds(0, BLOCK), :], vs_ref[pl.ds(0, BLOCK), :], causal)

    def body(n, carry):
        r0 = pl.multiple_of(n * BLOCK, BLOCK)
        w0 = pl.multiple_of((n - 1) * BLOCK, BLOCK)
        block(r0, ks_ref[pl.ds(w0, 2 * BLOCK), :], vs_ref[pl.ds(w0, 2 * BLOCK), :], band)
        return carry

    lax.fori_loop(1, seq // BLOCK, body, 0, unroll=3)


def _rope_tables(seq):
    inv = ROPE_THETA ** (-jnp.arange(0, HEAD_DIM, 2, dtype=F32) / HEAD_DIM)
    ang = jnp.arange(seq, dtype=F32)[:, None] * inv[None, :]
    reps = LANES // (HEAD_DIM // 2)
    cos = jnp.tile(jnp.cos(ang), (1, reps))
    sign = jnp.where((jnp.arange(LANES) % HEAD_DIM) < HEAD_DIM // 2, -1.0, 1.0).astype(F32)
    sin = jnp.tile(jnp.sin(ang), (1, reps)) * sign[None, :]
    return cos, sin


def _swa_mixer(proj3, sinks, cos, sin):
    b, s, _ = proj3.shape
    qw = C_WIDTH // C_KV_HEADS
    return pl.pallas_call(
        _swa_kernel,
        grid=(b, C_KV_HEADS),
        in_specs=[
            pl.BlockSpec(memory_space=pltpu.SMEM),
            pl.BlockSpec((None, s, qw), lambda i, g: (i, 0, OFF_Q // qw + g)),
            pl.BlockSpec((None, s, LANES), lambda i, g: (i, 0, OFF_K // LANES)),
            pl.BlockSpec((None, s, LANES), lambda i, g: (i, 0, OFF_V // LANES)),
            pl.BlockSpec((s, LANES), lambda i, g: (0, 0)),
            pl.BlockSpec((s, LANES), lambda i, g: (0, 0)),
        ],
        out_specs=pl.BlockSpec((None, s, qw), lambda i, g: (i, 0, g)),
        out_shape=jax.ShapeDtypeStruct((b, s, C_WIDTH), BF16),
        scratch_shapes=[
            pltpu.VMEM((s, qw), BF16),
            pltpu.VMEM((s, LANES), BF16),
            pltpu.VMEM((s, LANES), BF16),
        ],
        compiler_params=_cparams(("parallel", "parallel")),
        name="swa_mixer",
    )(sinks, proj3, proj3, proj3, cos, sin)


def _cmul(ar, ai, br, bi):
    return ar * br - ai * bi, ar * bi + ai * br


def _s5_kernel(u_ref, lr_ref, li_ref, ldt_ref, bre_ref, bim_ref, cre_ref, cim_ref, d_ref, wg_ref, bg_ref,
               o_ref, hr_ref, hi_ref, pw_ref, car_ref):
    ts = u_ref.shape[0]

    @pl.when(pl.program_id(1) == 0)
    def _():
        lr, li = lr_ref[...], li_ref[...]
        dt = jnp.exp(ldt_ref[...])
        mag = jnp.exp(lr * dt)
        a_re, a_im = mag * jnp.cos(li * dt), mag * jnp.sin(li * dt)
        nr, ni = a_re - 1.0, a_im
        den = lr * lr + li * li
        z_re = (nr * lr + ni * li) / den
        z_im = (ni * lr - nr * li) / den
        row = lax.broadcasted_iota(jnp.int32, (SUBLANES, 1), 0)
        pows = [(a_re, a_im)]
        for _ in range(SUBLANES - 1):
            pows.append(_cmul(pows[-1][0], pows[-1][1], a_re, a_im))
        for j, sft in enumerate((1, 2, 4)):
            pr, pi = pows[sft - 1]
            pw_ref[2 * j] = jnp.where(row >= sft, pr, 0.0)
            pw_ref[2 * j + 1] = jnp.where(row >= sft, pi, 0.0)
        cr = jnp.zeros((SUBLANES, S5_N), F32)
        ci = jnp.zeros((SUBLANES, S5_N), F32)
        for i in range(SUBLANES):
            cr = jnp.where(row == i, pows[i][0], cr)
            ci = jnp.where(row == i, pows[i][1], ci)
        pw_ref[6] = cr
        pw_ref[7] = ci
        pw_ref[8] = jnp.broadcast_to(z_re, (SUBLANES, S5_N))
        pw_ref[9] = jnp.broadcast_to(z_im, (SUBLANES, S5_N))
        car_ref[...] = jnp.zeros_like(car_ref)

    u = u_ref[...]
    bu_re = _dot(u, bre_ref[...])
    bu_im = _dot(u, bim_ref[...])
    z_re = pw_ref[8, 0:1, :]
    z_im = pw_ref[9, 0:1, :]
    hr_ref[...] = z_re * bu_re - z_im * bu_im
    hi_ref[...] = z_re * bu_im + z_im * bu_re

    mults = [(pw_ref[2 * j], pw_ref[2 * j + 1]) for j in range(3)]
    acr, aci = pw_ref[6], pw_ref[7]

    def chunk(k, carry):
        cr, ci = carry
        r0 = pl.multiple_of(k * SUBLANES, SUBLANES)
        xr = hr_ref[pl.ds(r0, SUBLANES), :]
        xi = hi_ref[pl.ds(r0, SUBLANES), :]
        for (mr, mi), sft in zip(mults, (1, 2, 4)):
            sr = pltpu.roll(xr, sft, 0)
            si = pltpu.roll(xi, sft, 0)
            xr, xi = xr + (mr * sr - mi * si), xi + (mr * si + mi * sr)
        xr = xr + (acr * cr - aci * ci)
        xi = xi + (acr * ci + aci * cr)
        hr_ref[pl.ds(r0, SUBLANES), :] = xr
        hi_ref[pl.ds(r0, SUBLANES), :] = xi
        return xr[SUBLANES - 1:SUBLANES, :], xi[SUBLANES - 1:SUBLANES, :]

    cr, ci = lax.fori_loop(0, ts // SUBLANES, chunk, (car_ref[0, 0:1, :], car_ref[1, 0:1, :]), unroll=2)
    car_ref[0] = jnp.broadcast_to(cr, (SUBLANES, S5_N))
    car_ref[1] = jnp.broadcast_to(ci, (SUBLANES, S5_N))

    y = _dot(hr_ref[...].astype(BF16), cre_ref[...]) - _dot(hi_ref[...].astype(BF16), cim_ref[...])
    y = y + d_ref[...] * u.astype(F32)
    gl = jax.nn.gelu(y)
    out = gl * jax.nn.sigmoid(_dot(gl.astype(BF16), wg_ref[...]) + bg_ref[...])
    o_ref[...] = out.astype(o_ref.dtype)


def _s5_mixer(proj3, lr, li, ldt, bre, bim, cre, cim, dsk, wglu, bglu):
    b, s, _ = proj3.shape
    full = lambda shape: pl.BlockSpec(shape, lambda i, t: (0,) * len(shape))
    return pl.pallas_call(
        _s5_kernel,
        grid=(b, s // TS_S5),
        in_specs=[
            pl.BlockSpec((None, TS_S5, S5_WIDTH), lambda i, t: (i, t, OFF_B // S5_WIDTH)),
            full((1, S5_N)), full((1, S5_N)), full((1, S5_N)),
            full((S5_WIDTH, S5_N)), full((S5_WIDTH, S5_N)),
            full((S5_N, S5_WIDTH)), full((S5_N, S5_WIDTH)),
            full((1, S5_WIDTH)), full((S5_WIDTH, S5_WIDTH)), full((1, S5_WIDTH)),
        ],
        out_specs=pl.BlockSpec((None, TS_S5, S5_WIDTH), lambda i, t: (i, t, 0)),
        out_shape=jax.ShapeDtypeStruct((b, s, S5_WIDTH), BF16),
        scratch_shapes=[
            pltpu.VMEM((TS_S5, S5_N), F32),
            pltpu.VMEM((TS_S5, S5_N), F32),
            pltpu.VMEM((10, SUBLANES, S5_N), F32),
            pltpu.VMEM((2, SUBLANES, S5_N), F32),
        ],
        compiler_params=_cparams(("parallel", "arbitrary")),
        name="s5_mixer",
    )(proj3, lr, li, ldt, bre, bim, cre, cim, dsk, wglu, bglu)


def _merge_kernel(x_ref, oa_ref, ob_ref, oc_ref, ga_ref, gb_ref, gc_ref, wa_ref, wb_ref, wc_ref, wo_ref,
                  gn_ref, *rest, with_router):
    if with_router:
        rt_ref, xo_ref, h_ref, lg_ref = rest
    else:
        xo_ref, h_ref = rest

    def sig(r):
        return jax.nn.sigmoid(r[...].astype(F32))

    merged = (sig(ga_ref) * _dot(oa_ref[...], wa_ref[...])
              + sig(gb_ref) * _dot(ob_ref[...], wb_ref[...])
              + sig(gc_ref) * _dot(oc_ref[...], wc_ref[...]))
    xn = x_ref[...] + _dot(merged.astype(BF16), wo_ref[...])
    xo_ref[...] = xn
    h = _rms(xn, gn_ref[...])
    h_hi = h.astype(BF16)
    h_ref[...] = h_hi
    if with_router:
        h_lo = (h - h_hi.astype(F32)).astype(BF16)
        rt = rt_ref[...]
        r_hi = rt.astype(BF16)
        r_lo = (rt - r_hi.astype(F32)).astype(BF16)
        lg_ref[...] = _dot_nt(r_hi, h_hi) + (_dot_nt(r_hi, h_lo) + _dot_nt(r_lo, h_hi))


def _merge(x, oa, ob, oc, proj, wa, wb, wc, wo, gn, router_t=None):
    t, d = x.shape
    tm = TM_MERGE
    with_router = router_t is not None
    row = lambda w: pl.BlockSpec((tm, w), lambda i: (i, 0))
    full = lambda a: pl.BlockSpec(a.shape, lambda i: (0, 0))
    in_specs = [row(d), row(A_WIDTH), row(S5_WIDTH), row(C_WIDTH)]
    in_specs += [pl.BlockSpec((tm, d), lambda i, c=c: (i, c)) for c in range(3)]
    in_specs += [full(wa), full(wb), full(wc), full(wo), full(gn)]
    args = [x, oa, ob, oc, proj, proj, proj, wa, wb, wc, wo, gn]
    out_specs = [row(d), row(d)]
    out_shape = [jax.ShapeDtypeStruct((t, d), F32), jax.ShapeDtypeStruct((t, d), BF16)]
    if with_router:
        in_specs.append(full(router_t))
        args.append(router_t)
        out_specs.append(pl.BlockSpec((N_EXPERTS, tm), lambda i: (0, i)))
        out_shape.append(jax.ShapeDtypeStruct((N_EXPERTS, t), F32))
    return pl.pallas_call(
        functools.partial(_merge_kernel, with_router=with_router),
        grid=(t // tm,),
        in_specs=in_specs,
        out_specs=out_specs,
        out_shape=out_shape,
        compiler_params=_cparams(("parallel",)),
        name="merge_router" if with_router else "merge",
    )(*args)


def _ffn_kernel(x_ref, h_ref, wg_ref, wu_ref, wd_ref, o_ref):
    @pl.when(pl.program_id(1) == 0)
    def _():
        o_ref[...] = x_ref[...]

    h = h_ref[...]
    act = jax.nn.silu(_dot(h, wg_ref[...])) * _dot(h, wu_ref[...])
    o_ref[...] += _dot(act.astype(BF16), wd_ref[...])


def _ffn(x, h, wg, wu, wd, li):
    t, d = x.shape
    ff = wg.shape[2]
    return pl.pallas_call(
        _ffn_kernel,
        grid=(t // TM_FFN, ff // TF_FFN),
        in_specs=[
            pl.BlockSpec((TM_FFN, d), lambda i, f: (i, 0)),
            pl.BlockSpec((TM_FFN, d), lambda i, f: (i, 0)),
            pl.BlockSpec((None, d, TF_FFN), lambda i, f: (li, 0, f)),
            pl.BlockSpec((None, d, TF_FFN), lambda i, f: (li, 0, f)),
            pl.BlockSpec((None, TF_FFN, d), lambda i, f: (li, f, 0)),
        ],
        out_specs=pl.BlockSpec((TM_FFN, d), lambda i, f: (i, 0)),
        out_shape=jax.ShapeDtypeStruct((t, d), F32),
        compiler_params=_cparams(("parallel", "arbitrary")),
        name="dense_ffn",
    )(x, h, wg, wu, wd)


def _route_kernel(lg_ref, gate_ref, rank_ref, cnt_ref):
    lg = lg_ref[...]
    tm = lg.shape[1]
    eidx = lax.broadcasted_iota(jnp.int32, lg.shape, 0)
    m1 = jnp.max(lg, axis=0, keepdims=True)
    i1 = jnp.min(jnp.where(lg == m1, eidx, N_EXPERTS), axis=0, keepdims=True)
    rest = jnp.where(eidx == i1, -jnp.inf, lg)
    m2 = jnp.max(rest, axis=0, keepdims=True)
    i2 = jnp.min(jnp.where(rest == m2, eidx, N_EXPERTS), axis=0, keepdims=True)
    e2 = jnp.exp(m2 - m1)
    tot = 1.0 + e2
    sel1 = eidx == i1
    sel2 = eidx == i2
    gate_ref[...] = jnp.where(sel1, 1.0 / tot, jnp.where(sel2, e2 / tot, 0.0))
    sel = (sel1 | sel2).astype(jnp.int32)
    lane = lax.broadcasted_iota(jnp.int32, lg.shape, 1)
    c = sel
    sft = 1
    while sft < tm:
        c = c + jnp.where(lane >= sft, pltpu.roll(c, sft, 1), 0)
        sft *= 2
    rank_ref[...] = jnp.where(sel > 0, c - 1, -1)
    cnt_ref[...] = jnp.broadcast_to(jnp.sum(sel, axis=1, keepdims=True), cnt_ref.shape)


def _route(logits_t, tm):
    e, t = logits_t.shape
    nt = t // tm
    return pl.pallas_call(
        _route_kernel,
        grid=(nt,),
        in_specs=[pl.BlockSpec((e, tm), lambda i: (0, i))],
        out_specs=[
            pl.BlockSpec((e, tm), lambda i: (0, i)),
            pl.BlockSpec((e, tm), lambda i: (0, i)),
            pl.BlockSpec((None, e, LANES), lambda i: (i, 0, 0)),
        ],
        out_shape=[
            jax.ShapeDtypeStruct((e, t), F32),
            jax.ShapeDtypeStruct((e, t), jnp.int32),
            jax.ShapeDtypeStruct((nt, e, LANES), jnp.int32),
        ],
        compiler_params=_cparams(("parallel",)),
        name="route_top2",
    )(logits_t)


def _moe_kernel(cnt_ref, x_ref, h_ref, rrow_ref, grow_ref, wg_ref, wu_ref, wd_ref, gfin_ref,
                o_ref, hc_ref, y_ref, *, final_norm):
    i, e, f = pl.program_id(0), pl.program_id(1), pl.program_id(2)
    ne, nf = pl.num_programs(1), pl.num_programs(2)
    tm = h_ref.shape[0]
    n_sel = cnt_ref[i * N_EXPERTS + e]
    n_chunks = (n_sel + (CH_MOE - 1)) // CH_MOE
    k_path = jnp.clip(n_chunks, KMIN_MOE, KMAX_MOE)
    n_over = jnp.maximum(n_chunks - KMAX_MOE, 0)

    def expert_part(hc):
        act = jax.nn.silu(_dot(hc, wg_ref[...])) * _dot(hc, wu_ref[...])
        return _dot(act.astype(BF16), wd_ref[...])

    def picks(r0, rows):
        slot = lax.broadcasted_iota(jnp.int32, (rows, tm), 0) + r0
        return slot == rrow_ref[...]

    def compact(r0, rows):
        onehot = jnp.where(picks(r0, rows), 1.0, 0.0).astype(BF16)
        return _dot(onehot, h_ref[...]).astype(BF16)

    def scatter_add(r0, y):
        sel = picks(r0, y.shape[0])
        gate_c = jnp.sum(jnp.where(sel, grow_ref[...], 0.0), axis=1, keepdims=True)
        yb = (y * gate_c).astype(BF16)
        onehot = jnp.where(sel, 1.0, 0.0).astype(BF16)
        for q in range(tm // SC_MOE):
            cols = slice(q * SC_MOE, (q + 1) * SC_MOE)
            o_ref[pl.ds(q * SC_MOE, SC_MOE), :] += lax.dot_general(
                onehot[:, cols], yb, (((0,), (0,)), ((), ())), preferred_element_type=F32)

    @pl.when((e == 0) & (f == 0))
    def _():
        o_ref[...] = x_ref[...]

    def static_path(rows):
        sl = pl.ds(0, rows)

        @pl.when(f == 0)
        def _():
            hc_ref[sl, :] = compact(0, rows)
            y_ref[sl, :] = jnp.zeros((rows, D_MODEL), F32)

        y_ref[sl, :] += expert_part(hc_ref[sl, :])

        @pl.when(f == nf - 1)
        def _():
            scatter_add(0, y_ref[sl, :])

    for k in range(KMIN_MOE, KMAX_MOE + 1):
        pl.when(k_path == k)(functools.partial(static_path, k * CH_MOE))

    def overflow(c, carry):
        r0 = (KMAX_MOE + c) * CH_MOE
        scatter_add(r0, expert_part(compact(r0, CH_MOE)))
        return carry

    lax.fori_loop(0, n_over, overflow, 0)

    if final_norm:
        @pl.when((e == ne - 1) & (f == nf - 1))
        def _():
            o_ref[...] = _rms(o_ref[...], gfin_ref[...])


def _moe(x, h, counts, rank_row, gate_row, wg, wu, wd, li, tm, final_gain=None):
    final_norm = final_gain is not None
    if not final_norm:
        final_gain = jnp.ones((1, x.shape[1]), F32)
    t, d = x.shape
    _, ne, _, ff = wg.shape
    nt = t // tm
    once = pl.Buffered(1)
    grid_spec = pltpu.PrefetchScalarGridSpec(
        num_scalar_prefetch=1,
        grid=(nt, ne, ff // TF_MOE),
        in_specs=[
            pl.BlockSpec((tm, d), lambda i, e, f, c: (i, 0), pipeline_mode=once),
            pl.BlockSpec((tm, d), lambda i, e, f, c: (i, 0)),
            pl.BlockSpec((None, 1, tm), lambda i, e, f, c: (e, 0, i)),
            pl.BlockSpec((None, 1, tm), lambda i, e, f, c: (e, 0, i)),
            pl.BlockSpec((None, None, d, TF_MOE), lambda i, e, f, c: (li, e, 0, f)),
            pl.BlockSpec((None, None, d, TF_MOE), lambda i, e, f, c: (li, e, 0, f)),
            pl.BlockSpec((None, None, TF_MOE, d), lambda i, e, f, c: (li, e, f, 0)),
            pl.BlockSpec((1, d), lambda i, e, f, c: (0, 0)),
        ],
        out_specs=pl.BlockSpec((tm, d), lambda i, e, f, c: (i, 0)),
        scratch_shapes=[pltpu.VMEM((KMAX_MOE * CH_MOE, d), BF16), pltpu.VMEM((KMAX_MOE * CH_MOE, d), F32)],
    )
    return pl.pallas_call(
        functools.partial(_moe_kernel, final_norm=final_norm),
        grid_spec=grid_spec,
        out_shape=jax.ShapeDtypeStruct((t, d), F32),
        compiler_params=pltpu.CompilerParams(dimension_semantics=("parallel", "arbitrary", "arbitrary"),
                                             vmem_limit_bytes=VMEM_LIMIT_MOE),
        name="moe_experts",
    )(counts, x, h, rank_row, gate_row, wg, wu, wd, final_gain)


def _final_norm_kernel(x_ref, g_ref, o_ref):
    o_ref[...] = _rms(x_ref[...], g_ref[...])


def _final_norm(x, g):
    t, d = x.shape
    tm = TM_PROJ
    return pl.pallas_call(
        _final_norm_kernel,
        grid=(t // tm,),
        in_specs=[pl.BlockSpec((tm, d), lambda i: (i, 0)), pl.BlockSpec((1, d), lambda i: (0, 0))],
        out_specs=pl.BlockSpec((tm, d), lambda i: (i, 0)),
        out_shape=jax.ShapeDtypeStruct((t, d), F32),
        compiler_params=_cparams(("parallel",)),
        name="final_norm",
    )(x, g)


def _block_diag(blocks):
    l, g, r, c = blocks.shape
    on_diag = jnp.eye(g, dtype=bool)[None, :, None, :, None]
    out = jnp.where(on_diag, blocks[:, :, :, None, :], jnp.zeros((), blocks.dtype))
    return out.reshape(l, g * r, g * c)


def kernel(x, norm_mix, w_in, s5_lambda_re, s5_lambda_im, s5_log_dt, s5_b_re, s5_b_im, s5_c_re, s5_c_im,
           s5_d, s5_w_glu, s5_b_glu, c_sinks, w_branch_a, w_branch_b, w_branch_c, w_out, norm_ffn,
           ffn_w_gate, ffn_w_up, ffn_w_down, moe_router, moe_w_gate, moe_w_up, moe_w_down, norm_final):
    b, s, d = x.shape
    depth = w_in.shape[0]
    t = b * s
    n_split = A_QKV + S5_WIDTH + C_WIDTH + 2 * C_KV_HEADS * HEAD_DIM

    w_in_p = jnp.concatenate([w_in[:, :, n_split:], w_in[:, :, :n_split]], axis=-1).astype(BF16)
    wa, wb, wc, wo = (w.astype(BF16) for w in (w_branch_a, w_branch_b, w_branch_c, w_out))
    fg, fu, fd = (w.astype(BF16) for w in (ffn_w_gate, ffn_w_up, ffn_w_down))
    mg, mu, md = (w.astype(BF16) for w in (moe_w_gate, moe_w_up, moe_w_down))
    router_t = jnp.swapaxes(moe_router, 1, 2)
    lam_re = s5_lambda_re.reshape(depth, 1, S5_N)
    lam_im = s5_lambda_im.reshape(depth, 1, S5_N)
    log_dt = jnp.repeat(s5_log_dt, S5_STATE, axis=-1).reshape(depth, 1, S5_N)
    bre = _block_diag(jnp.swapaxes(s5_b_re, 2, 3)).astype(BF16)
    bim = _block_diag(jnp.swapaxes(s5_b_im, 2, 3)).astype(BF16)
    cre = _block_diag(jnp.swapaxes(s5_c_re, 2, 3)).astype(BF16)
    cim = _block_diag(jnp.swapaxes(s5_c_im, 2, 3)).astype(BF16)
    wglu = s5_w_glu.astype(BF16)
    cos, sin = _rope_tables(s)

    xt = x.reshape(t, d)
    for l in range(depth):
        proj = _inproj(xt, norm_mix[l][None, :], w_in_p, l)
        proj3 = proj.reshape(b, s, IN_COLS)
        o_a = _dilated_mixer(proj3).reshape(t, A_WIDTH)
        o_b = _s5_mixer(proj3, lam_re[l], lam_im[l], log_dt[l], bre[l], bim[l], cre[l], cim[l],
                        s5_d[l][None, :], wglu[l], s5_b_glu[l][None, :]).reshape(t, S5_WIDTH)
        o_c = _swa_mixer(proj3, c_sinks[l], cos, sin).reshape(t, C_WIDTH)
        i = l // 2
        if l % 2 == 0:
            xt, h = _merge(xt, o_a, o_b, o_c, proj, wa[l], wb[l], wc[l], wo[l], norm_ffn[l][None, :])
            xt = _ffn(xt, h, fg, fu, fd, i)
        else:
            xt, h, logits_t = _merge(xt, o_a, o_b, o_c, proj, wa[l], wb[l], wc[l], wo[l],
                                     norm_ffn[l][None, :], router_t[i])
            gate, rank, cnt = _route(logits_t, TM_MOE)
            counts = cnt[:, :, 0].reshape(-1)
            last = l == depth - 1
            xt = _moe(xt, h, counts, rank[:, None, :], gate[:, None, :], mg, mu, md, i, TM_MOE,
                      final_gain=norm_final[None, :] if last else None)
    if depth % 2 == 1:
        xt = _final_norm(xt, norm_final[None, :])
    return xt.reshape(b, s, d)
```

```python
import functools
import math

import jax
import jax.numpy as jnp
import numpy as np
from jax import lax
from jax.experimental import pallas as pl
from jax.experimental.pallas import tpu as pltpu

F32 = jnp.float32
BF16 = jnp.bfloat16

D_MODEL = 1024
HEAD_DIM = 64
BLOCK = 128
LANES = 128
SUBLANES = 8
DILATIONS = (1, 4, 16)
N_DIL = 3
A_HEADS = 4
A_WIDTH = A_HEADS * HEAD_DIM
A_QKV = 3 * N_DIL * A_WIDTH
S5_WIDTH = 256
S5_GROUPS = 16
S5_GROUP_CH = 16
S5_STATE = 64
S5_N = S5_GROUPS * S5_STATE
C_Q_HEADS = 8
C_KV_HEADS = 2
C_WIDTH = C_Q_HEADS * HEAD_DIM
ROPE_THETA = 150000.0
N_GATE = 3 * D_MODEL
IN_COLS = 6400
D_FF = 3584
N_EXPERTS = 8
RMS_EPS = 1e-6
ATT_SCALE = HEAD_DIM ** -0.5

OFF_GATE = 0
OFF_A = N_GATE
OFF_B = OFF_A + A_QKV
OFF_Q = OFF_B + S5_WIDTH
OFF_K = OFF_Q + C_WIDTH
OFF_V = OFF_K + C_KV_HEADS * HEAD_DIM

TM_PROJ = 2048
TN_PROJ = 1280
TM_MERGE = 1024
TM_FFN = 1024
TF_FFN = 512
TS_S5 = 512
TM_MOE = 2048
TF_MOE = 1792
FF_SUB_MOE = ((0, 768), (768, 1024))
CH_MOE = 128
KMIN_MOE = 4
KMAX_MOE = 5
SC_MOE = 512
VMEM_LIMIT = 56 * 1024 * 1024
VMEM_LIMIT_MOE = 60 * 1024 * 1024


def _cparams(sem):
    return pltpu.CompilerParams(dimension_semantics=sem, vmem_limit_bytes=VMEM_LIMIT)


def _dot(a, b):
    return jnp.dot(a, b, preferred_element_type=F32)


def _dot_nt(a, b):
    return lax.dot_general(a, b, (((1,), (1,)), ((), ())), preferred_element_type=F32)


def _rms(x, g):
    return x * lax.rsqrt(jnp.mean(x * x, axis=-1, keepdims=True) + RMS_EPS) * g


def _inproj_kernel(x_ref, g_ref, w_ref, o_ref, h_ref):
    @pl.when(pl.program_id(1) == 0)
    def _():
        h_ref[...] = _rms(x_ref[...], g_ref[...]).astype(BF16)

    o_ref[...] = _dot(h_ref[...], w_ref[...]).astype(o_ref.dtype)


def _inproj(x, g, w, l):
    t, d = x.shape
    n = w.shape[2]
    return pl.pallas_call(
        _inproj_kernel,
        grid=(t // TM_PROJ, n // TN_PROJ),
        in_specs=[
            pl.BlockSpec((TM_PROJ, d), lambda i, j: (i, 0)),
            pl.BlockSpec((1, d), lambda i, j: (0, 0)),
            pl.BlockSpec((None, d, TN_PROJ), lambda i, j: (l, 0, j)),
        ],
        out_specs=pl.BlockSpec((TM_PROJ, TN_PROJ), lambda i, j: (i, j)),
        out_shape=jax.ShapeDtypeStruct((t, n), BF16),
        scratch_shapes=[pltpu.VMEM((TM_PROJ, d), BF16)],
        compiler_params=_cparams(("parallel", "arbitrary")),
        name="inproj",
    )(x, g, w)


def _band_mask(rows, width):
    qi = lax.broadcasted_iota(jnp.int32, (rows, width), 0) % BLOCK
    ki = lax.broadcasted_iota(jnp.int32, (rows, width), 1)
    return (ki >= qi) & (ki <= qi + BLOCK)


def _causal_mask(rows, width):
    qi = lax.broadcasted_iota(jnp.int32, (rows, width), 0) % BLOCK
    ki = lax.broadcasted_iota(jnp.int32, (rows, width), 1)
    return ki <= qi


def _softmax_pv(s, mask, v, sink=None):
    s = jnp.where(mask, s, -jnp.inf)
    m = jnp.max(s, axis=-1, keepdims=True)
    if sink is not None:
        m = jnp.maximum(m, sink)
    p = jnp.exp(s - m)
    den = jnp.sum(p, axis=-1, keepdims=True)
    if sink is not None:
        den = den + jnp.exp(sink - m)
    o = _dot(p.astype(BF16), v)
    return o, m, den


def _dilated_kernel(q0_ref, q1_ref, q2_ref, k0_ref, k1_ref, k2_ref, v0_ref, v1_ref, v2_ref,
                    o_ref, qf_ref, kf_ref, vf_ref, og_ref, lg_ref):
    seq = o_ref.shape[0]
    lane = lax.broadcasted_iota(jnp.int32, (1, LANES), 1)
    lo = lane < HEAD_DIM
    band = _band_mask(2 * BLOCK, 2 * BLOCK)
    causal = _causal_mask(2 * BLOCK, BLOCK)

    def attend(q, k, v, mask):
        zero = jnp.zeros_like(q)
        qs = jnp.concatenate([jnp.where(lo, q, zero), jnp.where(lo, zero, q)], axis=0)
        s = _dot_nt(qs, k)
        o, m, den = _softmax_pv(s, mask, v)
        o = o * (1.0 / den)
        lse = m + jnp.log(den)
        o_pair = jnp.where(lo, o[:BLOCK], o[BLOCK:])
        l_pair = jnp.where(lo, lse[:BLOCK], lse[BLOCK:])
        return o_pair, l_pair

    for idx, (qr, kr, vr) in enumerate(((q1_ref, k1_ref, v1_ref), (q2_ref, k2_ref, v2_ref))):
        qf_ref[idx] = qr[...].astype(F32) * ATT_SCALE
        kf_ref[idx] = kr[...].astype(F32)
        vf_ref[idx] = vr[...].astype(F32)

    o_p, l_p = attend(q0_ref[pl.ds(0, BLOCK), :] * ATT_SCALE, k0_ref[pl.ds(0, BLOCK), :],
                      v0_ref[pl.ds(0, BLOCK), :], causal)
    og_ref[0, pl.ds(0, BLOCK), :] = o_p
    lg_ref[0, pl.ds(0, BLOCK), :] = l_p

    for n in range(1, seq // BLOCK):
        r0, w0 = n * BLOCK, (n - 1) * BLOCK
        o_p, l_p = attend(q0_ref[pl.ds(r0, BLOCK), :] * ATT_SCALE, k0_ref[pl.ds(w0, 2 * BLOCK), :],
                          v0_ref[pl.ds(w0, 2 * BLOCK), :], band)
        og_ref[0, pl.ds(r0, BLOCK), :] = o_p
        lg_ref[0, pl.ds(r0, BLOCK), :] = l_p

    for idx, d in ((0, DILATIONS[1]), (1, DILATIONS[2])):
        g = idx + 1
        nblk = seq // d // BLOCK
        for r in range(d):
            for n in range(nblk):
                q = qf_ref[idx, pl.ds(r + n * BLOCK * d, BLOCK, stride=d), :].astype(BF16)
                if n == 0:
                    k = kf_ref[idx, pl.ds(r, BLOCK, stride=d), :].astype(BF16)
                    v = vf_ref[idx, pl.ds(r, BLOCK, stride=d), :].astype(BF16)
                    o_p, l_p = attend(q, k, v, causal)
                else:
                    w0 = r + (n - 1) * BLOCK * d
                    k = kf_ref[idx, pl.ds(w0, 2 * BLOCK, stride=d), :].astype(BF16)
                    v = vf_ref[idx, pl.ds(w0, 2 * BLOCK, stride=d), :].astype(BF16)
                    o_p, l_p = attend(q, k, v, band)
                og_ref[g, pl.ds(r + n * BLOCK * d, BLOCK, stride=d), :] = o_p
                lg_ref[g, pl.ds(r + n * BLOCK * d, BLOCK, stride=d), :] = l_p

    l0, l1, l2 = lg_ref[0], lg_ref[1], lg_ref[2]
    mx = jnp.maximum(jnp.maximum(l0, l1), l2)
    e0, e1, e2 = jnp.exp(l0 - mx), jnp.exp(l1 - mx), jnp.exp(l2 - mx)
    tot = e0 + e1 + e2
    o_ref[...] = ((e0 * og_ref[0] + e1 * og_ref[1] + e2 * og_ref[2]) / tot).astype(o_ref.dtype)


def _dilated_mixer(proj3):
    b, s, _ = proj3.shape
    base = OFF_A // LANES

    def spec(which, g):
        col = base + which * (N_DIL * A_WIDTH // LANES) + g * (A_WIDTH // LANES)
        return pl.BlockSpec((None, s, LANES), lambda i, hp, col=col: (i, 0, col + hp))

    in_specs = [spec(w, g) for w in range(3) for g in range(N_DIL)]
    return pl.pallas_call(
        _dilated_kernel,
        grid=(b, A_WIDTH // LANES),
        in_specs=in_specs,
        out_specs=pl.BlockSpec((None, s, LANES), lambda i, hp: (i, 0, hp)),
        out_shape=jax.ShapeDtypeStruct((b, s, A_WIDTH), BF16),
        scratch_shapes=[
            pltpu.VMEM((2, s, LANES), F32),
            pltpu.VMEM((2, s, LANES), F32),
            pltpu.VMEM((2, s, LANES), F32),
            pltpu.VMEM((N_DIL, s, LANES), F32),
            pltpu.VMEM((N_DIL, s, LANES), F32),
        ],
        compiler_params=_cparams(("parallel", "parallel")),
        name="dilated_mixer",
    )(*([proj3] * 9))


def _swa_kernel(sink_ref, q_ref, k_ref, v_ref, cos_ref, sin_ref, o_ref, qs_ref, ks_ref, vs_ref):
    seq = o_ref.shape[0]
    g = pl.program_id(1)
    rep = C_Q_HEADS // C_KV_HEADS
    lane = lax.broadcasted_iota(jnp.int32, (1, LANES), 1)
    lo = lane < HEAD_DIM
    cos = cos_ref[...]
    sin = sin_ref[...]

    src = lax.broadcasted_iota(jnp.int32, (LANES, LANES), 0)
    dst = lax.broadcasted_iota(jnp.int32, (LANES, LANES), 1)
    half = HEAD_DIM // 2
    partner = jnp.where((dst % HEAD_DIM) < half, dst + half, dst - half)
    swap_mat = jnp.where(src == partner, 1.0, 0.0).astype(BF16)
    rep_mat = jnp.where(src == g * HEAD_DIM + dst % HEAD_DIM, 1.0, 0.0).astype(BF16)

    def rope(x):
        return x.astype(F32) * cos + _dot(x, swap_mat) * sin

    ks_ref[...] = _dot(rope(k_ref[...]).astype(BF16), rep_mat).astype(BF16)
    vs_ref[...] = _dot(v_ref[...], rep_mat).astype(BF16)
    for c in range(rep // 2):
        sl = slice(c * LANES, (c + 1) * LANES)
        qs_ref[:, sl] = (rope(q_ref[:, sl]) * ATT_SCALE).astype(BF16)

    rows = rep * BLOCK
    hrow = lax.broadcasted_iota(jnp.int32, (rows, 1), 0) // BLOCK
    sink = jnp.zeros((rows, 1), F32)
    for h in range(rep):
        sink = jnp.where(hrow == h, sink_ref[g * rep + h], sink)
    band = _band_mask(rows, 2 * BLOCK)
    causal = _causal_mask(rows, BLOCK)

    def block(r0, k, v, mask):
        parts = []
        for h in range(rep):
            q = qs_ref[pl.ds(r0, BLOCK), (h // 2) * LANES:(h // 2 + 1) * LANES]
            zero = jnp.zeros_like(q)
            parts.append(jnp.where(lo, q, zero) if h % 2 == 0 else jnp.where(lo, zero, q))
        s = _dot_nt(jnp.concatenate(parts, axis=0), k)
        o, _, den = _softmax_pv(s, mask, v, sink)
        o = o * (1.0 / den)
        for c in range(rep // 2):
            pair = jnp.where(lo, o[2 * c * BLOCK:(2 * c + 1) * BLOCK], o[(2 * c + 1) * BLOCK:(2 * c + 2) * BLOCK])
            o_ref[pl.ds(r0, BLOCK), c * LANES:(c + 1) * LANES] = pair.astype(o_ref.dtype)

    block(0, ks_ref[pl.ds(0, BLOCK), :], vs_ref[pl.ds(0, BLOCK), :], causal)

    def body(n, carry):
        r0 = pl.multiple_of(n * BLOCK, BLOCK)
        w0 = pl.multiple_of((n - 1) * BLOCK, BLOCK)
        block(r0, ks_ref[pl.ds(w0, 2 * BLOCK), :], vs_ref[pl.ds(w0, 2 * BLOCK), :], band)
        return carry

    lax.fori_loop(1, seq // BLOCK, body, 0, unroll=3)


def _rope_tables(seq):
    inv = ROPE_THETA ** (-jnp.arange(0, HEAD_DIM, 2, dtype=F32) / HEAD_DIM)
    ang = jnp.arange(seq, dtype=F32)[:, None] * inv[None, :]
    reps = LANES // (HEAD_DIM // 2)
    cos = jnp.tile(jnp.cos(ang), (1, reps))
    sign = jnp.where((jnp.arange(LANES) % HEAD_DIM) < HEAD_DIM // 2, -1.0, 1.0).astype(F32)
    sin = jnp.tile(jnp.sin(ang), (1, reps)) * sign[None, :]
    return cos, sin


def _swa_mixer(proj3, sinks, cos, sin):
    b, s, _ = proj3.shape
    qw = C_WIDTH // C_KV_HEADS
    return pl.pallas_call(
        _swa_kernel,
        grid=(b, C_KV_HEADS),
        in_specs=[
            pl.BlockSpec(memory_space=pltpu.SMEM),
            pl.BlockSpec((None, s, qw), lambda i, g: (i, 0, OFF_Q // qw + g)),
            pl.BlockSpec((None, s, LANES), lambda i, g: (i, 0, OFF_K // LANES)),
            pl.BlockSpec((None, s, LANES), lambda i, g: (i, 0, OFF_V // LANES)),
            pl.BlockSpec((s, LANES), lambda i, g: (0, 0)),
            pl.BlockSpec((s, LANES), lambda i, g: (0, 0)),
        ],
        out_specs=pl.BlockSpec((None, s, qw), lambda i, g: (i, 0, g)),
        out_shape=jax.ShapeDtypeStruct((b, s, C_WIDTH), BF16),
        scratch_shapes=[
            pltpu.VMEM((s, qw), BF16),
            pltpu.VMEM((s, LANES), BF16),
            pltpu.VMEM((s, LANES), BF16),
        ],
        compiler_params=_cparams(("parallel", "parallel")),
        name="swa_mixer",
    )(sinks, proj3, proj3, proj3, cos, sin)


def _cmul(ar, ai, br, bi):
    return ar * br - ai * bi, ar * bi + ai * br


def _s5_kernel(u_ref, lr_ref, li_ref, ldt_ref, bre_ref, bim_ref, cre_ref, cim_ref, d_ref, wg_ref, bg_ref,
               o_ref, hr_ref, hi_ref, pw_ref, car_ref):
    ts = u_ref.shape[0]

    @pl.when(pl.program_id(1) == 0)
    def _():
        lr, li = lr_ref[...], li_ref[...]
        dt = jnp.exp(ldt_ref[...])
        mag = jnp.exp(lr * dt)
        a_re, a_im = mag * jnp.cos(li * dt), mag * jnp.sin(li * dt)
        nr, ni = a_re - 1.0, a_im
        den = lr * lr + li * li
        z_re = (nr * lr + ni * li) / den
        z_im = (ni * lr - nr * li) / den
        row = lax.broadcasted_iota(jnp.int32, (SUBLANES, 1), 0)
        pows = [(a_re, a_im)]
        for _ in range(SUBLANES - 1):
            pows.append(_cmul(pows[-1][0], pows[-1][1], a_re, a_im))
        for j, sft in enumerate((1, 2, 4)):
            pr, pi = pows[sft - 1]
            pw_ref[2 * j] = jnp.where(row >= sft, pr, 0.0)
            pw_ref[2 * j + 1] = jnp.where(row >= sft, pi, 0.0)
        cr = jnp.zeros((SUBLANES, S5_N), F32)
        ci = jnp.zeros((SUBLANES, S5_N), F32)
        for i in range(SUBLANES):
            cr = jnp.where(row == i, pows[i][0], cr)
            ci = jnp.where(row == i, pows[i][1], ci)
        pw_ref[6] = cr
        pw_ref[7] = ci
        pw_ref[8] = jnp.broadcast_to(z_re, (SUBLANES, S5_N))
        pw_ref[9] = jnp.broadcast_to(z_im, (SUBLANES, S5_N))
        car_ref[...] = jnp.zeros_like(car_ref)

    u = u_ref[...]
    bu_re = _dot(u, bre_ref[...])
    bu_im = _dot(u, bim_ref[...])
    z_re = pw_ref[8, 0:1, :]
    z_im = pw_ref[9, 0:1, :]
    hr_ref[...] = z_re * bu_re - z_im * bu_im
    hi_ref[...] = z_re * bu_im + z_im * bu_re

    mults = [(pw_ref[2 * j], pw_ref[2 * j + 1]) for j in range(3)]
    acr, aci = pw_ref[6], pw_ref[7]

    def chunk(k, carry):
        cr, ci = carry
        r0 = pl.multiple_of(k * SUBLANES, SUBLANES)
        xr = hr_ref[pl.ds(r0, SUBLANES), :]
        xi = hi_ref[pl.ds(r0, SUBLANES), :]
        for (mr, mi), sft in zip(mults, (1, 2, 4)):
            sr = pltpu.roll(xr, sft, 0)
            si = pltpu.roll(xi, sft, 0)
            xr, xi = xr + (mr * sr - mi * si), xi + (mr * si + mi * sr)
        xr = xr + (acr * cr - aci * ci)
        xi = xi + (acr * ci + aci * cr)
        hr_ref[pl.ds(r0, SUBLANES), :] = xr
        hi_ref[pl.ds(r0, SUBLANES), :] = xi
        return xr[SUBLANES - 1:SUBLANES, :], xi[SUBLANES - 1:SUBLANES, :]

    cr, ci = lax.fori_loop(0, ts // SUBLANES, chunk, (car_ref[0, 0:1, :], car_ref[1, 0:1, :]), unroll=2)
    car_ref[0] = jnp.broadcast_to(cr, (SUBLANES, S5_N))
    car_ref[1] = jnp.broadcast_to(ci, (SUBLANES, S5_N))

    y = _dot(hr_ref[...].astype(BF16), cre_ref[...]) - _dot(hi_ref[...].astype(BF16), cim_ref[...])
    y = y + d_ref[...] * u.astype(F32)
    gl = jax.nn.gelu(y)
    out = gl * jax.nn.sigmoid(_dot(gl.astype(BF16), wg_ref[...]) + bg_ref[...])
    o_ref[...] = out.astype(o_ref.dtype)


def _s5_mixer(proj3, lr, li, ldt, bre, bim, cre, cim, dsk, wglu, bglu):
    b, s, _ = proj3.shape
    full = lambda shape: pl.BlockSpec(shape, lambda i, t: (0,) * len(shape))
    return pl.pallas_call(
        _s5_kernel,
        grid=(b, s // TS_S5),
        in_specs=[
            pl.BlockSpec((None, TS_S5, S5_WIDTH), lambda i, t: (i, t, OFF_B // S5_WIDTH)),
            full((1, S5_N)), full((1, S5_N)), full((1, S5_N)),
            full((S5_WIDTH, S5_N)), full((S5_WIDTH, S5_N)),
            full((S5_N, S5_WIDTH)), full((S5_N, S5_WIDTH)),
            full((1, S5_WIDTH)), full((S5_WIDTH, S5_WIDTH)), full((1, S5_WIDTH)),
        ],
        out_specs=pl.BlockSpec((None, TS_S5, S5_WIDTH), lambda i, t: (i, t, 0)),
        out_shape=jax.ShapeDtypeStruct((b, s, S5_WIDTH), BF16),
        scratch_shapes=[
            pltpu.VMEM((TS_S5, S5_N), F32),
            pltpu.VMEM((TS_S5, S5_N), F32),
            pltpu.VMEM((10, SUBLANES, S5_N), F32),
            pltpu.VMEM((2, SUBLANES, S5_N), F32),
        ],
        compiler_params=_cparams(("parallel", "arbitrary")),
        name="s5_mixer",
    )(proj3, lr, li, ldt, bre, bim, cre, cim, dsk, wglu, bglu)


def _merge_kernel(x_ref, oa_ref, ob_ref, oc_ref, ga_ref, gb_ref, gc_ref, wa_ref, wb_ref, wc_ref, wo_ref,
                  gn_ref, *rest, with_router):
    if with_router:
        rt_ref, xo_ref, h_ref, lg_ref = rest
    else:
        xo_ref, h_ref = rest

    def sig(r):
        return jax.nn.sigmoid(r[...].astype(F32))

    merged = (sig(ga_ref) * _dot(oa_ref[...], wa_ref[...])
              + sig(gb_ref) * _dot(ob_ref[...], wb_ref[...])
              + sig(gc_ref) * _dot(oc_ref[...], wc_ref[...]))
    xn = x_ref[...] + _dot(merged.astype(BF16), wo_ref[...])
    xo_ref[...] = xn
    h = _rms(xn, gn_ref[...])
    h_hi = h.astype(BF16)
    h_ref[...] = h_hi
    if with_router:
        h_lo = (h - h_hi.astype(F32)).astype(BF16)
        rt = rt_ref[...]
        r_hi = rt.astype(BF16)
        r_lo = (rt - r_hi.astype(F32)).astype(BF16)
        lg_ref[...] = _dot_nt(r_hi, h_hi) + (_dot_nt(r_hi, h_lo) + _dot_nt(r_lo, h_hi))


def _merge(x, oa, ob, oc, proj, wa, wb, wc, wo, gn, router_t=None):
    t, d = x.shape
    tm = TM_MERGE
    with_router = router_t is not None
    row = lambda w: pl.BlockSpec((tm, w), lambda i: (i, 0))
    full = lambda a: pl.BlockSpec(a.shape, lambda i: (0, 0))
    in_specs = [row(d), row(A_WIDTH), row(S5_WIDTH), row(C_WIDTH)]
    in_specs += [pl.BlockSpec((tm, d), lambda i, c=c: (i, c)) for c in range(3)]
    in_specs += [full(wa), full(wb), full(wc), full(wo), full(gn)]
    args = [x, oa, ob, oc, proj, proj, proj, wa, wb, wc, wo, gn]
    out_specs = [row(d), row(d)]
    out_shape = [jax.ShapeDtypeStruct((t, d), F32), jax.ShapeDtypeStruct((t, d), BF16)]
    if with_router:
        in_specs.append(full(router_t))
        args.append(router_t)
        out_specs.append(pl.BlockSpec((N_EXPERTS, tm), lambda i: (0, i)))
        out_shape.append(jax.ShapeDtypeStruct((N_EXPERTS, t), F32))
    return pl.pallas_call(
        functools.partial(_merge_kernel, with_router=with_router),
        grid=(t // tm,),
        in_specs=in_specs,
        out_specs=out_specs,
        out_shape=out_shape,
        compiler_params=_cparams(("parallel",)),
        name="merge_router" if with_router else "merge",
    )(*args)


def _ffn_kernel(x_ref, h_ref, wg_ref, wu_ref, wd_ref, o_ref):
    @pl.when(pl.program_id(1) == 0)
    def _():
        o_ref[...] = x_ref[...]

    h = h_ref[...]
    act = jax.nn.silu(_dot(h, wg_ref[...])) * _dot(h, wu_ref[...])
    o_ref[...] += _dot(act.astype(BF16), wd_ref[...])


def _ffn(x, h, wg, wu, wd, li):
    t, d = x.shape
    ff = wg.shape[2]
    return pl.pallas_call(
        _ffn_kernel,
        grid=(t // TM_FFN, ff // TF_FFN),
        in_specs=[
            pl.BlockSpec((TM_FFN, d), lambda i, f: (i, 0)),
            pl.BlockSpec((TM_FFN, d), lambda i, f: (i, 0)),
            pl.BlockSpec((None, d, TF_FFN), lambda i, f: (li, 0, f)),
            pl.BlockSpec((None, d, TF_FFN), lambda i, f: (li, 0, f)),
            pl.BlockSpec((None, TF_FFN, d), lambda i, f: (li, f, 0)),
        ],
        out_specs=pl.BlockSpec((TM_FFN, d), lambda i, f: (i, 0)),
        out_shape=jax.ShapeDtypeStruct((t, d), F32),
        compiler_params=_cparams(("parallel", "arbitrary")),
        name="dense_ffn",
    )(x, h, wg, wu, wd)


def _route_kernel(lg_ref, gate_ref, rank_ref, cnt_ref):
    lg = lg_ref[...]
    tm = lg.shape[1]
    eidx = lax.broadcasted_iota(jnp.int32, lg.shape, 0)
    m1 = jnp.max(lg, axis=0, keepdims=True)
    i1 = jnp.min(jnp.where(lg == m1, eidx, N_EXPERTS), axis=0, keepdims=True)
    rest = jnp.where(eidx == i1, -jnp.inf, lg)
    m2 = jnp.max(rest, axis=0, keepdims=True)
    i2 = jnp.min(jnp.where(rest == m2, eidx, N_EXPERTS), axis=0, keepdims=True)
    e2 = jnp.exp(m2 - m1)
    tot = 1.0 + e2
    sel1 = eidx == i1
    sel2 = eidx == i2
    gate_ref[...] = jnp.where(sel1, 1.0 / tot, jnp.where(sel2, e2 / tot, 0.0))
    sel = (sel1 | sel2).astype(jnp.int32)
    lane = lax.broadcasted_iota(jnp.int32, lg.shape, 1)
    c = sel
    sft = 1
    while sft < tm:
        c = c + jnp.where(lane >= sft, pltpu.roll(c, sft, 1), 0)
        sft *= 2
    rank_ref[...] = jnp.where(sel > 0, c - 1, -1)
    cnt_ref[...] = jnp.broadcast_to(jnp.sum(sel, axis=1, keepdims=True), cnt_ref.shape)


def _route(logits_t, tm):
    e, t = logits_t.shape
    nt = t // tm
    return pl.pallas_call(
        _route_kernel,
        grid=(nt,),
        in_specs=[pl.BlockSpec((e, tm), lambda i: (0, i))],
        out_specs=[
            pl.BlockSpec((e, tm), lambda i: (0, i)),
            pl.BlockSpec((e, tm), lambda i: (0, i)),
            pl.BlockSpec((None, e, LANES), lambda i: (i, 0, 0)),
        ],
        out_shape=[
            jax.ShapeDtypeStruct((e, t), F32),
            jax.ShapeDtypeStruct((e, t), jnp.int32),
            jax.ShapeDtypeStruct((nt, e, LANES), jnp.int32),
        ],
        compiler_params=_cparams(("parallel",)),
        name="route_top2",
    )(logits_t)


def _moe_kernel(cnt_ref, x_ref, h_ref, rrow_ref, grow_ref, wg_ref, wu_ref, wd_ref, gfin_ref,
                o_ref, hc_ref, y_ref, *, final_norm):
    i, e, f = pl.program_id(0), pl.program_id(1), pl.program_id(2)
    ne, nf = pl.num_programs(1), pl.num_programs(2)
    tm = h_ref.shape[0]
    n_sel = cnt_ref[i * N_EXPERTS + e]
    n_chunks = (n_sel + (CH_MOE - 1)) // CH_MOE
    k_path = jnp.clip(n_chunks, KMIN_MOE, KMAX_MOE)
    n_over = jnp.maximum(n_chunks - KMAX_MOE, 0)

    def expert_part(hc):
        part = None
        for c0, cw in FF_SUB_MOE:
            act = jax.nn.silu(_dot(hc, wg_ref[:, c0:c0 + cw])) * _dot(hc, wu_ref[:, c0:c0 + cw])
            p = _dot(act.astype(BF16), wd_ref[c0:c0 + cw, :])
            part = p if part is None else part + p
        return part

    def picks(r0, rows):
        slot = lax.broadcasted_iota(jnp.int32, (rows, tm), 0) + r0
        return slot == rrow_ref[...]

    def compact(r0, rows):
        onehot = jnp.where(picks(r0, rows), 1.0, 0.0).astype(BF16)
        return _dot(onehot, h_ref[...]).astype(BF16)

    def scatter_add(r0, y):
        sel = picks(r0, y.shape[0])
        gate_c = jnp.sum(jnp.where(sel, grow_ref[...], 0.0), axis=1, keepdims=True)
        yb = (y * gate_c).astype(BF16)
        onehot = jnp.where(sel, 1.0, 0.0).astype(BF16)
        for q in range(tm // SC_MOE):
            cols = slice(q * SC_MOE, (q + 1) * SC_MOE)
            o_ref[pl.ds(q * SC_MOE, SC_MOE), :] += lax.dot_general(
                onehot[:, cols], yb, (((0,), (0,)), ((), ())), preferred_element_type=F32)

    @pl.when((e == 0) & (f == 0))
    def _():
        o_ref[...] = x_ref[...]

    def static_path(rows):
        sl = pl.ds(0, rows)

        @pl.when(f == 0)
        def _():
            hc_ref[sl, :] = compact(0, rows)
            y_ref[sl, :] = jnp.zeros((rows, D_MODEL), F32)

        y_ref[sl, :] += expert_part(hc_ref[sl, :])

        @pl.when(f == nf - 1)
        def _():
            scatter_add(0, y_ref[sl, :])

    for k in range(KMIN_MOE, KMAX_MOE + 1):
        pl.when(k_path == k)(functools.partial(static_path, k * CH_MOE))

    def overflow(c, carry):
        r0 = (KMAX_MOE + c) * CH_MOE
        scatter_add(r0, expert_part(compact(r0, CH_MOE)))
        return carry

    lax.fori_loop(0, n_over, overflow, 0)

    if final_norm:
        @pl.when((e == ne - 1) & (f == nf - 1))
        def _():
            o_ref[...] = _rms(o_ref[...], gfin_ref[...])


def _moe(x, h, counts, rank_row, gate_row, wg, wu, wd, li, tm, final_gain=None):
    final_norm = final_gain is not None
    if not final_norm:
        final_gain = jnp.ones((1, x.shape[1]), F32)
    t, d = x.shape
    _, ne, _, ff = wg.shape
    nt = t // tm
    once = pl.Buffered(1)
    grid_spec = pltpu.PrefetchScalarGridSpec(
        num_scalar_prefetch=1,
        grid=(nt, ne, ff // TF_MOE),
        in_specs=[
            pl.BlockSpec((tm, d), lambda i, e, f, c: (i, 0), pipeline_mode=once),
            pl.BlockSpec((tm, d), lambda i, e, f, c: (i, 0), pipeline_mode=once),
            pl.BlockSpec((None, 1, tm), lambda i, e, f, c: (e, 0, i)),
            pl.BlockSpec((None, 1, tm), lambda i, e, f, c: (e, 0, i)),
            pl.BlockSpec((None, None, d, TF_MOE), lambda i, e, f, c: (li, e, 0, f)),
            pl.BlockSpec((None, None, d, TF_MOE), lambda i, e, f, c: (li, e, 0, f)),
            pl.BlockSpec((None, None, TF_MOE, d), lambda i, e, f, c: (li, e, f, 0)),
            pl.BlockSpec((1, d), lambda i, e, f, c: (0, 0)),
        ],
        out_specs=pl.BlockSpec((tm, d), lambda i, e, f, c: (i, 0), pipeline_mode=once),
        scratch_shapes=[pltpu.VMEM((KMAX_MOE * CH_MOE, d), BF16), pltpu.VMEM((KMAX_MOE * CH_MOE, d), F32)],
    )
    return pl.pallas_call(
        functools.partial(_moe_kernel, final_norm=final_norm),
        grid_spec=grid_spec,
        out_shape=jax.ShapeDtypeStruct((t, d), F32),
        compiler_params=pltpu.CompilerParams(dimension_semantics=("parallel", "arbitrary", "arbitrary"),
                                             vmem_limit_bytes=VMEM_LIMIT_MOE),
        name="moe_experts",
    )(counts, x, h, rank_row, gate_row, wg, wu, wd, final_gain)


def _final_norm_kernel(x_ref, g_ref, o_ref):
    o_ref[...] = _rms(x_ref[...], g_ref[...])


def _final_norm(x, g):
    t, d = x.shape
    tm = TM_PROJ
    return pl.pallas_call(
        _final_norm_kernel,
        grid=(t // tm,),
        in_specs=[pl.BlockSpec((tm, d), lambda i: (i, 0)), pl.BlockSpec((1, d), lambda i: (0, 0))],
        out_specs=pl.BlockSpec((tm, d), lambda i: (i, 0)),
        out_shape=jax.ShapeDtypeStruct((t, d), F32),
        compiler_params=_cparams(("parallel",)),
        name="final_norm",
    )(x, g)


def _block_diag(blocks):
    l, g, r, c = blocks.shape
    on_diag = jnp.eye(g, dtype=bool)[None, :, None, :, None]
    out = jnp.where(on_diag, blocks[:, :, :, None, :], jnp.zeros((), blocks.dtype))
    return out.reshape(l, g * r, g * c)


def kernel(x, norm_mix, w_in, s5_lambda_re, s5_lambda_im, s5_log_dt, s5_b_re, s5_b_im, s5_c_re, s5_c_im,
           s5_d, s5_w_glu, s5_b_glu, c_sinks, w_branch_a, w_branch_b, w_branch_c, w_out, norm_ffn,
           ffn_w_gate, ffn_w_up, ffn_w_down, moe_router, moe_w_gate, moe_w_up, moe_w_down, norm_final):
    b, s, d = x.shape
    depth = w_in.shape[0]
    t = b * s
    n_split = A_QKV + S5_WIDTH + C_WIDTH + 2 * C_KV_HEADS * HEAD_DIM

    w_in_p = jnp.concatenate([w_in[:, :, n_split:], w_in[:, :, :n_split]], axis=-1).astype(BF16)
    wa, wb, wc, wo = (w.astype(BF16) for w in (w_branch_a, w_branch_b, w_branch_c, w_out))
    fg, fu, fd = (w.astype(BF16) for w in (ffn_w_gate, ffn_w_up, ffn_w_down))
    mg, mu, md = (w.astype(BF16) for w in (moe_w_gate, moe_w_up, moe_w_down))
    router_t = jnp.swapaxes(moe_router, 1, 2)
    lam_re = s5_lambda_re.reshape(depth, 1, S5_N)
    lam_im = s5_lambda_im.reshape(depth, 1, S5_N)
    log_dt = jnp.repeat(s5_log_dt, S5_STATE, axis=-1).reshape(depth, 1, S5_N)
    bre = _block_diag(jnp.swapaxes(s5_b_re, 2, 3)).astype(BF16)
    bim = _block_diag(jnp.swapaxes(s5_b_im, 2, 3)).astype(BF16)
    cre = _block_diag(jnp.swapaxes(s5_c_re, 2, 3)).astype(BF16)
    cim = _block_diag(jnp.swapaxes(s5_c_im, 2, 3)).astype(BF16)
    wglu = s5_w_glu.astype(BF16)
    cos, sin = _rope_tables(s)

    xt = x.reshape(t, d)
    for l in range(depth):
        proj = _inproj(xt, norm_mix[l][None, :], w_in_p, l)
        proj3 = proj.reshape(b, s, IN_COLS)
        o_a = _dilated_mixer(proj3).reshape(t, A_WIDTH)
        o_b = _s5_mixer(proj3, lam_re[l], lam_im[l], log_dt[l], bre[l], bim[l], cre[l], cim[l],
                        s5_d[l][None, :], wglu[l], s5_b_glu[l][None, :]).reshape(t, S5_WIDTH)
        o_c = _swa_mixer(proj3, c_sinks[l], cos, sin).reshape(t, C_WIDTH)
        i = l // 2
        if l % 2 == 0:
            xt, h = _merge(xt, o_a, o_b, o_c, proj, wa[l], wb[l], wc[l], wo[l], norm_ffn[l][None, :])
            xt = _ffn(xt, h, fg, fu, fd, i)
        else:
            xt, h, logits_t = _merge(xt, o_a, o_b, o_c, proj, wa[l], wb[l], wc[l], wo[l],
                                     norm_ffn[l][None, :], router_t[i])
            gate, rank, cnt = _route(logits_t, TM_MOE)
            counts = cnt[:, :, 0].reshape(-1)
            last = l == depth - 1
            xt = _moe(xt, h, counts, rank[:, None, :], gate[:, None, :], mg, mu, md, i, TM_MOE,
                      final_gain=norm_final[None, :] if last else None)
    if depth % 2 == 1:
        xt = _final_norm(xt, norm_final[None, :])
    return xt.reshape(b, s, d)
```

```python
import functools
import math

import jax
import jax.numpy as jnp
import numpy as np
from jax import lax
from jax.experimental import pallas as pl
from jax.experimental.pallas import tpu as pltpu

F32 = jnp.float32
BF16 = jnp.bfloat16

D_MODEL = 1024
HEAD_DIM = 64
BLOCK = 128
LANES = 128
SUBLANES = 8
DILATIONS = (1, 4, 16)
N_DIL = 3
A_HEADS = 4
A_WIDTH = A_HEADS * HEAD_DIM
A_QKV = 3 * N_DIL * A_WIDTH
S5_WIDTH = 256
S5_GROUPS = 16
S5_GROUP_CH = 16
S5_STATE = 64
S5_N = S5_GROUPS * S5_STATE
C_Q_HEADS = 8
C_KV_HEADS = 2
C_WIDTH = C_Q_HEADS * HEAD_DIM
ROPE_THETA = 150000.0
N_GATE = 3 * D_MODEL
IN_COLS = 6400
D_FF = 3584
N_EXPERTS = 8
RMS_EPS = 1e-6
ATT_SCALE = HEAD_DIM ** -0.5

OFF_GATE = 0
OFF_A = N_GATE
OFF_B = OFF_A + A_QKV
OFF_Q = OFF_B + S5_WIDTH
OFF_K = OFF_Q + C_WIDTH
OFF_V = OFF_K + C_KV_HEADS * HEAD_DIM

TM_PROJ = 2048
TN_PROJ = 1280
TM_MERGE = 1024
TM_FFN = 1024
TF_FFN = 1792
TS_S5 = 1024
TM_MOE = 2048
TF_MOE = 1792
FF_SUB_MOE = ((0, 768), (768, 1024))
CH_MOE = 128
KMIN_MOE = 4
KMAX_MOE = 5
SC_MOE = 512
VMEM_LIMIT = 56 * 1024 * 1024
VMEM_LIMIT_MOE = 60 * 1024 * 1024


def _cparams(sem):
    return pltpu.CompilerParams(dimension_semantics=sem, vmem_limit_bytes=VMEM_LIMIT)


def _dot(a, b):
    return jnp.dot(a, b, preferred_element_type=F32)


def _dot_nt(a, b):
    return lax.dot_general(a, b, (((1,), (1,)), ((), ())), preferred_element_type=F32)


def _rms(x, g):
    return x * lax.rsqrt(jnp.mean(x * x, axis=-1, keepdims=True) + RMS_EPS) * g


def _inproj_kernel(x_ref, g_ref, w_ref, o_ref, h_ref):
    @pl.when(pl.program_id(1) == 0)
    def _():
        h_ref[...] = _rms(x_ref[...], g_ref[...]).astype(BF16)

    o_ref[...] = _dot(h_ref[...], w_ref[...]).astype(o_ref.dtype)


def _inproj(x, g, w, l):
    t, d = x.shape
    n = w.shape[2]
    return pl.pallas_call(
        _inproj_kernel,
        grid=(t // TM_PROJ, n // TN_PROJ),
        in_specs=[
            pl.BlockSpec((TM_PROJ, d), lambda i, j: (i, 0)),
            pl.BlockSpec((1, d), lambda i, j: (0, 0)),
            pl.BlockSpec((None, d, TN_PROJ), lambda i, j: (l, 0, j)),
        ],
        out_specs=pl.BlockSpec((TM_PROJ, TN_PROJ), lambda i, j: (i, j)),
        out_shape=jax.ShapeDtypeStruct((t, n), BF16),
        scratch_shapes=[pltpu.VMEM((TM_PROJ, d), BF16)],
        compiler_params=_cparams(("parallel", "arbitrary")),
        name="inproj",
    )(x, g, w)


def _band_mask(rows, width):
    qi = lax.broadcasted_iota(jnp.int32, (rows, width), 0) % BLOCK
    ki = lax.broadcasted_iota(jnp.int32, (rows, width), 1)
    return (ki >= qi) & (ki <= qi + BLOCK)


def _causal_mask(rows, width):
    qi = lax.broadcasted_iota(jnp.int32, (rows, width), 0) % BLOCK
    ki = lax.broadcasted_iota(jnp.int32, (rows, width), 1)
    return ki <= qi


def _softmax_pv(s, mask, v, sink=None):
    s = jnp.where(mask, s, -jnp.inf)
    m = jnp.max(s, axis=-1, keepdims=True)
    if sink is not None:
        m = jnp.maximum(m, sink)
    p = jnp.exp(s - m)
    den = jnp.sum(p, axis=-1, keepdims=True)
    if sink is not None:
        den = den + jnp.exp(sink - m)
    o = _dot(p.astype(BF16), v)
    return o, m, den


def _dilated_kernel(q0_ref, q1_ref, q2_ref, k0_ref, k1_ref, k2_ref, v0_ref, v1_ref, v2_ref,
                    o_ref, qf_ref, kf_ref, vf_ref, og_ref, lg_ref):
    seq = o_ref.shape[0]
    lane = lax.broadcasted_iota(jnp.int32, (1, LANES), 1)
    lo = lane < HEAD_DIM
    band = _band_mask(2 * BLOCK, 2 * BLOCK)
    causal = _causal_mask(2 * BLOCK, BLOCK)

    def attend(q, k, v, mask):
        zero = jnp.zeros_like(q)
        qs = jnp.concatenate([jnp.where(lo, q, zero), jnp.where(lo, zero, q)], axis=0)
        s = _dot_nt(qs, k)
        o, m, den = _softmax_pv(s, mask, v)
        o = o * (1.0 / den)
        lse = m + jnp.log(den)
        o_pair = jnp.where(lo, o[:BLOCK], o[BLOCK:])
        l_pair = jnp.where(lo, lse[:BLOCK], lse[BLOCK:])
        return o_pair, l_pair

    for idx, (qr, kr, vr) in enumerate(((q1_ref, k1_ref, v1_ref), (q2_ref, k2_ref, v2_ref))):
        qf_ref[idx] = qr[...].astype(F32) * ATT_SCALE
        kf_ref[idx] = kr[...].astype(F32)
        vf_ref[idx] = vr[...].astype(F32)

    o_p, l_p = attend(q0_ref[pl.ds(0, BLOCK), :] * ATT_SCALE, k0_ref[pl.ds(0, BLOCK), :],
                      v0_ref[pl.ds(0, BLOCK), :], causal)
    og_ref[0, pl.ds(0, BLOCK), :] = o_p
    lg_ref[0, pl.ds(0, BLOCK), :] = l_p

    for n in range(1, seq // BLOCK):
        r0, w0 = n * BLOCK, (n - 1) * BLOCK
        o_p, l_p = attend(q0_ref[pl.ds(r0, BLOCK), :] * ATT_SCALE, k0_ref[pl.ds(w0, 2 * BLOCK), :],
                          v0_ref[pl.ds(w0, 2 * BLOCK), :], band)
        og_ref[0, pl.ds(r0, BLOCK), :] = o_p
        lg_ref[0, pl.ds(r0, BLOCK), :] = l_p

    for idx, d in ((0, DILATIONS[1]), (1, DILATIONS[2])):
        g = idx + 1
        nblk = seq // d // BLOCK
        for r in range(d):
            for n in range(nblk):
                q = qf_ref[idx, pl.ds(r + n * BLOCK * d, BLOCK, stride=d), :].astype(BF16)
                if n == 0:
                    k = kf_ref[idx, pl.ds(r, BLOCK, stride=d), :].astype(BF16)
                    v = vf_ref[idx, pl.ds(r, BLOCK, stride=d), :].astype(BF16)
                    o_p, l_p = attend(q, k, v, causal)
                else:
                    w0 = r + (n - 1) * BLOCK * d
                    k = kf_ref[idx, pl.ds(w0, 2 * BLOCK, stride=d), :].astype(BF16)
                    v = vf_ref[idx, pl.ds(w0, 2 * BLOCK, stride=d), :].astype(BF16)
                    o_p, l_p = attend(q, k, v, band)
                og_ref[g, pl.ds(r + n * BLOCK * d, BLOCK, stride=d), :] = o_p
                lg_ref[g, pl.ds(r + n * BLOCK * d, BLOCK, stride=d), :] = l_p

    l0, l1, l2 = lg_ref[0], lg_ref[1], lg_ref[2]
    mx = jnp.maximum(jnp.maximum(l0, l1), l2)
    e0, e1, e2 = jnp.exp(l0 - mx), jnp.exp(l1 - mx), jnp.exp(l2 - mx)
    tot = e0 + e1 + e2
    o_ref[...] = ((e0 * og_ref[0] + e1 * og_ref[1] + e2 * og_ref[2]) / tot).astype(o_ref.dtype)


def _dilated_mixer(proj3):
    b, s, _ = proj3.shape
    base = OFF_A // LANES

    def spec(which, g):
        col = base + which * (N_DIL * A_WIDTH // LANES) + g * (A_WIDTH // LANES)
        return pl.BlockSpec((None, s, LANES), lambda i, hp, col=col: (i, 0, col + hp))

    in_specs = [spec(w, g) for w in range(3) for g in range(N_DIL)]
    return pl.pallas_call(
        _dilated_kernel,
        grid=(b, A_WIDTH // LANES),
        in_specs=in_specs,
        out_specs=pl.BlockSpec((None, s, LANES), lambda i, hp: (i, 0, hp)),
        out_shape=jax.ShapeDtypeStruct((b, s, A_WIDTH), BF16),
        scratch_shapes=[
            pltpu.VMEM((2, s, LANES), F32),
            pltpu.VMEM((2, s, LANES), F32),
            pltpu.VMEM((2, s, LANES), F32),
            pltpu.VMEM((N_DIL, s, LANES), F32),
            pltpu.VMEM((N_DIL, s, LANES), F32),
        ],
        compiler_params=_cparams(("parallel", "parallel")),
        name="dilated_mixer",
    )(*([proj3] * 9))


def _swa_kernel(sink_ref, q_ref, k_ref, v_ref, cos_ref, sin_ref, o_ref, qs_ref, ks_ref, vs_ref):
    seq = o_ref.shape[0]
    g = pl.program_id(1)
    rep = C_Q_HEADS // C_KV_HEADS
    lane = lax.broadcasted_iota(jnp.int32, (1, LANES), 1)
    lo = lane < HEAD_DIM
    cos = cos_ref[...]
    sin = sin_ref[...]

    src = lax.broadcasted_iota(jnp.int32, (LANES, LANES), 0)
    dst = lax.broadcasted_iota(jnp.int32, (LANES, LANES), 1)
    half = HEAD_DIM // 2
    partner = jnp.where((dst % HEAD_DIM) < half, dst + half, dst - half)
    swap_mat = jnp.where(src == partner, 1.0, 0.0).astype(BF16)
    rep_mat = jnp.where(src == g * HEAD_DIM + dst % HEAD_DIM, 1.0, 0.0).astype(BF16)

    def rope(x):
        return x.astype(F32) * cos + _dot(x, swap_mat) * sin

    ks_ref[...] = _dot(rope(k_ref[...]).astype(BF16), rep_mat).astype(BF16)
    vs_ref[...] = _dot(v_ref[...], rep_mat).astype(BF16)
    for c in range(rep // 2):
        sl = slice(c * LANES, (c + 1) * LANES)
        qs_ref[:, sl] = (rope(q_ref[:, sl]) * ATT_SCALE).astype(BF16)

    rows = rep * BLOCK
    hrow = lax.broadcasted_iota(jnp.int32, (rows, 1), 0) // BLOCK
    sink = jnp.zeros((rows, 1), F32)
    for h in range(rep):
        sink = jnp.where(hrow == h, sink_ref[g * rep + h], sink)
    band = _band_mask(rows, 2 * BLOCK)
    causal = _causal_mask(rows, BLOCK)

    def block(r0, k, v, mask):
        parts = []
        for h in range(rep):
            q = qs_ref[pl.ds(r0, BLOCK), (h // 2) * LANES:(h // 2 + 1) * LANES]
            zero = jnp.zeros_like(q)
            parts.append(jnp.where(lo, q, zero) if h % 2 == 0 else jnp.where(lo, zero, q))
        s = _dot_nt(jnp.concatenate(parts, axis=0), k)
        o, _, den = _softmax_pv(s, mask, v, sink)
        o = o * (1.0 / den)
        for c in range(rep // 2):
            pair = jnp.where(lo, o[2 * c * BLOCK:(2 * c + 1) * BLOCK], o[(2 * c + 1) * BLOCK:(2 * c + 2) * BLOCK])
            o_ref[pl.ds(r0, BLOCK), c * LANES:(c + 1) * LANES] = pair.astype(o_ref.dtype)

    block(0, ks_ref[pl.ds(0, BLOCK), :], vs_ref[pl.ds(0, BLOCK), :], causal)

    def body(n, carry):
        r0 = pl.multiple_of(n * BLOCK, BLOCK)
        w0 = pl.multiple_of((n - 1) * BLOCK, BLOCK)
        block(r0, ks_ref[pl.ds(w0, 2 * BLOCK), :], vs_ref[pl.ds(w0, 2 * BLOCK), :], band)
        return carry

    lax.fori_loop(1, seq // BLOCK, body, 0, unroll=3)


def _rope_tables(seq):
    inv = ROPE_THETA ** (-jnp.arange(0, HEAD_DIM, 2, dtype=F32) / HEAD_DIM)
    ang = jnp.arange(seq, dtype=F32)[:, None] * inv[None, :]
    reps = LANES // (HEAD_DIM // 2)
    cos = jnp.tile(jnp.cos(ang), (1, reps))
    sign = jnp.where((jnp.arange(LANES) % HEAD_DIM) < HEAD_DIM // 2, -1.0, 1.0).astype(F32)
    sin = jnp.tile(jnp.sin(ang), (1, reps)) * sign[None, :]
    return cos, sin


def _swa_mixer(proj3, sinks, cos, sin):
    b, s, _ = proj3.shape
    qw = C_WIDTH // C_KV_HEADS
    return pl.pallas_call(
        _swa_kernel,
        grid=(b, C_KV_HEADS),
        in_specs=[
            pl.BlockSpec(memory_space=pltpu.SMEM),
            pl.BlockSpec((None, s, qw), lambda i, g: (i, 0, OFF_Q // qw + g)),
            pl.BlockSpec((None, s, LANES), lambda i, g: (i, 0, OFF_K // LANES)),
            pl.BlockSpec((None, s, LANES), lambda i, g: (i, 0, OFF_V // LANES)),
            pl.BlockSpec((s, LANES), lambda i, g: (0, 0)),
            pl.BlockSpec((s, LANES), lambda i, g: (0, 0)),
        ],
        out_specs=pl.BlockSpec((None, s, qw), lambda i, g: (i, 0, g)),
        out_shape=jax.ShapeDtypeStruct((b, s, C_WIDTH), BF16),
        scratch_shapes=[
            pltpu.VMEM((s, qw), BF16),
            pltpu.VMEM((s, LANES), BF16),
            pltpu.VMEM((s, LANES), BF16),
        ],
        compiler_params=_cparams(("parallel", "parallel")),
        name="swa_mixer",
    )(sinks, proj3, proj3, proj3, cos, sin)


def _cmul(ar, ai, br, bi):
    return ar * br - ai * bi, ar * bi + ai * br


def _s5_kernel(u_ref, lr_ref, li_ref, ldt_ref, bre_ref, bim_ref, cre_ref, cim_ref, d_ref, wg_ref, bg_ref,
               o_ref, hr_ref, hi_ref, pw_ref, car_ref):
    ts = u_ref.shape[0]

    @pl.when(pl.program_id(1) == 0)
    def _():
        lr, li = lr_ref[...], li_ref[...]
        dt = jnp.exp(ldt_ref[...])
        mag = jnp.exp(lr * dt)
        a_re, a_im = mag * jnp.cos(li * dt), mag * jnp.sin(li * dt)
        nr, ni = a_re - 1.0, a_im
        den = lr * lr + li * li
        z_re = (nr * lr + ni * li) / den
        z_im = (ni * lr - nr * li) / den
        row = lax.broadcasted_iota(jnp.int32, (SUBLANES, 1), 0)
        pows = [(a_re, a_im)]
        for _ in range(SUBLANES - 1):
            pows.append(_cmul(pows[-1][0], pows[-1][1], a_re, a_im))
        for j, sft in enumerate((1, 2, 4)):
            pr, pi = pows[sft - 1]
            pw_ref[2 * j] = jnp.where(row >= sft, pr, 0.0)
            pw_ref[2 * j + 1] = jnp.where(row >= sft, pi, 0.0)
        cr = jnp.zeros((SUBLANES, S5_N), F32)
        ci = jnp.zeros((SUBLANES, S5_N), F32)
        for i in range(SUBLANES):
            cr = jnp.where(row == i, pows[i][0], cr)
            ci = jnp.where(row == i, pows[i][1], ci)
        pw_ref[6] = cr
        pw_ref[7] = ci
        pw_ref[8] = jnp.broadcast_to(z_re, (SUBLANES, S5_N))
        pw_ref[9] = jnp.broadcast_to(z_im, (SUBLANES, S5_N))
        car_ref[...] = jnp.zeros_like(car_ref)

    u = u_ref[...]
    bu_re = _dot(u, bre_ref[...])
    bu_im = _dot(u, bim_ref[...])
    z_re = pw_ref[8, 0:1, :]
    z_im = pw_ref[9, 0:1, :]
    hr_ref[...] = z_re * bu_re - z_im * bu_im
    hi_ref[...] = z_re * bu_im + z_im * bu_re

    mults = [(pw_ref[2 * j], pw_ref[2 * j + 1]) for j in range(3)]
    acr, aci = pw_ref[6], pw_ref[7]

    def chunk(k, carry):
        cr, ci = carry
        r0 = pl.multiple_of(k * SUBLANES, SUBLANES)
        xr = hr_ref[pl.ds(r0, SUBLANES), :]
        xi = hi_ref[pl.ds(r0, SUBLANES), :]
        for (mr, mi), sft in zip(mults, (1, 2, 4)):
            sr = pltpu.roll(xr, sft, 0)
            si = pltpu.roll(xi, sft, 0)
            xr, xi = xr + (mr * sr - mi * si), xi + (mr * si + mi * sr)
        xr = xr + (acr * cr - aci * ci)
        xi = xi + (acr * ci + aci * cr)
        hr_ref[pl.ds(r0, SUBLANES), :] = xr
        hi_ref[pl.ds(r0, SUBLANES), :] = xi
        return xr[SUBLANES - 1:SUBLANES, :], xi[SUBLANES - 1:SUBLANES, :]

    cr, ci = lax.fori_loop(0, ts // SUBLANES, chunk, (car_ref[0, 0:1, :], car_ref[1, 0:1, :]), unroll=2)
    car_ref[0] = jnp.broadcast_to(cr, (SUBLANES, S5_N))
    car_ref[1] = jnp.broadcast_to(ci, (SUBLANES, S5_N))

    y = _dot(hr_ref[...].astype(BF16), cre_ref[...]) - _dot(hi_ref[...].astype(BF16), cim_ref[...])
    y = y + d_ref[...] * u.astype(F32)
    gl = jax.nn.gelu(y)
    out = gl * jax.nn.sigmoid(_dot(gl.astype(BF16), wg_ref[...]) + bg_ref[...])
    o_ref[...] = out.astype(o_ref.dtype)


def _s5_mixer(proj3, lr, li, ldt, bre, bim, cre, cim, dsk, wglu, bglu):
    b, s, _ = proj3.shape
    full = lambda shape: pl.BlockSpec(shape, lambda i, t: (0,) * len(shape))
    return pl.pallas_call(
        _s5_kernel,
        grid=(b, s // TS_S5),
        in_specs=[
            pl.BlockSpec((None, TS_S5, S5_WIDTH), lambda i, t: (i, t, OFF_B // S5_WIDTH)),
            full((1, S5_N)), full((1, S5_N)), full((1, S5_N)),
            full((S5_WIDTH, S5_N)), full((S5_WIDTH, S5_N)),
            full((S5_N, S5_WIDTH)), full((S5_N, S5_WIDTH)),
            full((1, S5_WIDTH)), full((S5_WIDTH, S5_WIDTH)), full((1, S5_WIDTH)),
        ],
        out_specs=pl.BlockSpec((None, TS_S5, S5_WIDTH), lambda i, t: (i, t, 0)),
        out_shape=jax.ShapeDtypeStruct((b, s, S5_WIDTH), BF16),
        scratch_shapes=[
            pltpu.VMEM((TS_S5, S5_N), F32),
            pltpu.VMEM((TS_S5, S5_N), F32),
            pltpu.VMEM((10, SUBLANES, S5_N), F32),
            pltpu.VMEM((2, SUBLANES, S5_N), F32),
        ],
        compiler_params=_cparams(("parallel", "arbitrary")),
        name="s5_mixer",
    )(proj3, lr, li, ldt, bre, bim, cre, cim, dsk, wglu, bglu)


def _merge_kernel(x_ref, oa_ref, ob_ref, oc_ref, ga_ref, gb_ref, gc_ref, wa_ref, wb_ref, wc_ref, wo_ref,
                  gn_ref, *rest, with_router):
    if with_router:
        rt_ref, xo_ref, h_ref, lg_ref = rest
    else:
        xo_ref, h_ref = rest

    def sig(r):
        return jax.nn.sigmoid(r[...].astype(F32))

    merged = (sig(ga_ref) * _dot(oa_ref[...], wa_ref[...])
              + sig(gb_ref) * _dot(ob_ref[...], wb_ref[...])
              + sig(gc_ref) * _dot(oc_ref[...], wc_ref[...]))
    xn = x_ref[...] + _dot(merged.astype(BF16), wo_ref[...])
    xo_ref[...] = xn
    h = _rms(xn, gn_ref[...])
    h_hi = h.astype(BF16)
    h_ref[...] = h_hi
    if with_router:
        h_lo = (h - h_hi.astype(F32)).astype(BF16)
        rt = rt_ref[...]
        r_hi = rt.astype(BF16)
        r_lo = (rt - r_hi.astype(F32)).astype(BF16)
        lg_ref[...] = _dot_nt(r_hi, h_hi) + (_dot_nt(r_hi, h_lo) + _dot_nt(r_lo, h_hi))


def _merge(x, oa, ob, oc, proj, wa, wb, wc, wo, gn, router_t=None):
    t, d = x.shape
    tm = TM_MERGE
    with_router = router_t is not None
    row = lambda w: pl.BlockSpec((tm, w), lambda i: (i, 0))
    full = lambda a: pl.BlockSpec(a.shape, lambda i: (0, 0))
    in_specs = [row(d), row(A_WIDTH), row(S5_WIDTH), row(C_WIDTH)]
    in_specs += [pl.BlockSpec((tm, d), lambda i, c=c: (i, c)) for c in range(3)]
    in_specs += [full(wa), full(wb), full(wc), full(wo), full(gn)]
    args = [x, oa, ob, oc, proj, proj, proj, wa, wb, wc, wo, gn]
    out_specs = [row(d), row(d)]
    out_shape = [jax.ShapeDtypeStruct((t, d), F32), jax.ShapeDtypeStruct((t, d), BF16)]
    if with_router:
        in_specs.append(full(router_t))
        args.append(router_t)
        out_specs.append(pl.BlockSpec((N_EXPERTS, tm), lambda i: (0, i)))
        out_shape.append(jax.ShapeDtypeStruct((N_EXPERTS, t), F32))
    return pl.pallas_call(
        functools.partial(_merge_kernel, with_router=with_router),
        grid=(t // tm,),
        in_specs=in_specs,
        out_specs=out_specs,
        out_shape=out_shape,
        compiler_params=_cparams(("parallel",)),
        name="merge_router" if with_router else "merge",
    )(*args)


def _ffn_kernel(x_ref, h_ref, wg_ref, wu_ref, wd_ref, o_ref):
    @pl.when(pl.program_id(1) == 0)
    def _():
        o_ref[...] = x_ref[...]

    h = h_ref[...]
    for c0, cw in FF_SUB_MOE:
        act = jax.nn.silu(_dot(h, wg_ref[:, c0:c0 + cw])) * _dot(h, wu_ref[:, c0:c0 + cw])
        o_ref[...] += _dot(act.astype(BF16), wd_ref[c0:c0 + cw, :])


def _ffn(x, h, wg, wu, wd, li):
    t, d = x.shape
    ff = wg.shape[2]
    return pl.pallas_call(
        _ffn_kernel,
        grid=(t // TM_FFN, ff // TF_FFN),
        in_specs=[
            pl.BlockSpec((TM_FFN, d), lambda i, f: (i, 0)),
            pl.BlockSpec((TM_FFN, d), lambda i, f: (i, 0)),
            pl.BlockSpec((None, d, TF_FFN), lambda i, f: (li, 0, f)),
            pl.BlockSpec((None, d, TF_FFN), lambda i, f: (li, 0, f)),
            pl.BlockSpec((None, TF_FFN, d), lambda i, f: (li, f, 0)),
        ],
        out_specs=pl.BlockSpec((TM_FFN, d), lambda i, f: (i, 0)),
        out_shape=jax.ShapeDtypeStruct((t, d), F32),
        compiler_params=_cparams(("parallel", "arbitrary")),
        name="dense_ffn",
    )(x, h, wg, wu, wd)


def _route_kernel(lg_ref, gate_ref, rank_ref, cnt_ref):
    lg = lg_ref[...]
    tm = lg.shape[1]
    eidx = lax.broadcasted_iota(jnp.int32, lg.shape, 0)
    m1 = jnp.max(lg, axis=0, keepdims=True)
    i1 = jnp.min(jnp.where(lg == m1, eidx, N_EXPERTS), axis=0, keepdims=True)
    rest = jnp.where(eidx == i1, -jnp.inf, lg)
    m2 = jnp.max(rest, axis=0, keepdims=True)
    i2 = jnp.min(jnp.where(rest == m2, eidx, N_EXPERTS), axis=0, keepdims=True)
    e2 = jnp.exp(m2 - m1)
    tot = 1.0 + e2
    sel1 = eidx == i1
    sel2 = eidx == i2
    gate_ref[...] = jnp.where(sel1, 1.0 / tot, jnp.where(sel2, e2 / tot, 0.0))
    sel = (sel1 | sel2).astype(jnp.int32)
    lane = lax.broadcasted_iota(jnp.int32, lg.shape, 1)
    c = sel
    sft = 1
    while sft < tm:
        c = c + jnp.where(lane >= sft, pltpu.roll(c, sft, 1), 0)
        sft *= 2
    rank_ref[...] = jnp.where(sel > 0, c - 1, -1)
    cnt_ref[...] = jnp.broadcast_to(jnp.sum(sel, axis=1, keepdims=True), cnt_ref.shape)


def _route(logits_t, tm):
    e, t = logits_t.shape
    nt = t // tm
    return pl.pallas_call(
        _route_kernel,
        grid=(nt,),
        in_specs=[pl.BlockSpec((e, tm), lambda i: (0, i))],
        out_specs=[
            pl.BlockSpec((e, tm), lambda i: (0, i)),
            pl.BlockSpec((e, tm), lambda i: (0, i)),
            pl.BlockSpec((None, e, LANES), lambda i: (i, 0, 0)),
        ],
        out_shape=[
            jax.ShapeDtypeStruct((e, t), F32),
            jax.ShapeDtypeStruct((e, t), jnp.int32),
            jax.ShapeDtypeStruct((nt, e, LANES), jnp.int32),
        ],
        compiler_params=_cparams(("parallel",)),
        name="route_top2",
    )(logits_t)


def _moe_kernel(cnt_ref, x_ref, h_ref, rrow_ref, grow_ref, wg_ref, wu_ref, wd_ref, gfin_ref,
                o_ref, hc_ref, y_ref, *, final_norm):
    i, e, f = pl.program_id(0), pl.program_id(1), pl.program_id(2)
    ne, nf = pl.num_programs(1), pl.num_programs(2)
    tm = h_ref.shape[0]
    n_sel = cnt_ref[i * N_EXPERTS + e]
    n_chunks = (n_sel + (CH_MOE - 1)) // CH_MOE
    k_path = jnp.clip(n_chunks, KMIN_MOE, KMAX_MOE)
    n_over = jnp.maximum(n_chunks - KMAX_MOE, 0)

    def expert_part(hc):
        part = None
        for c0, cw in FF_SUB_MOE:
            act = jax.nn.silu(_dot(hc, wg_ref[:, c0:c0 + cw])) * _dot(hc, wu_ref[:, c0:c0 + cw])
            p = _dot(act.astype(BF16), wd_ref[c0:c0 + cw, :])
            part = p if part is None else part + p
        return part

    def picks(r0, rows):
        slot = lax.broadcasted_iota(jnp.int32, (rows, tm), 0) + r0
        return slot == rrow_ref[...]

    def compact(r0, rows):
        onehot = jnp.where(picks(r0, rows), 1.0, 0.0).astype(BF16)
        return _dot(onehot, h_ref[...]).astype(BF16)

    def scatter_add(r0, y):
        sel = picks(r0, y.shape[0])
        gate_c = jnp.sum(jnp.where(sel, grow_ref[...], 0.0), axis=1, keepdims=True)
        yb = (y * gate_c).astype(BF16)
        onehot = jnp.where(sel, 1.0, 0.0).astype(BF16)
        for q in range(tm // SC_MOE):
            cols = slice(q * SC_MOE, (q + 1) * SC_MOE)
            o_ref[pl.ds(q * SC_MOE, SC_MOE), :] += lax.dot_general(
                onehot[:, cols], yb, (((0,), (0,)), ((), ())), preferred_element_type=F32)

    @pl.when((e == 0) & (f == 0))
    def _():
        o_ref[...] = x_ref[...]

    def static_path(rows):
        sl = pl.ds(0, rows)

        @pl.when(f == 0)
        def _():
            hc_ref[sl, :] = compact(0, rows)
            y_ref[sl, :] = jnp.zeros((rows, D_MODEL), F32)

        y_ref[sl, :] += expert_part(hc_ref[sl, :])

        @pl.when(f == nf - 1)
        def _():
            scatter_add(0, y_ref[sl, :])

    for k in range(KMIN_MOE, KMAX_MOE + 1):
        pl.when(k_path == k)(functools.partial(static_path, k * CH_MOE))

    def overflow(c, carry):
        r0 = (KMAX_MOE + c) * CH_MOE
        scatter_add(r0, expert_part(compact(r0, CH_MOE)))
        return carry

    lax.fori_loop(0, n_over, overflow, 0)

    if final_norm:
        @pl.when((e == ne - 1) & (f == nf - 1))
        def _():
            o_ref[...] = _rms(o_ref[...], gfin_ref[...])


def _moe(x, h, counts, rank_row, gate_row, wg, wu, wd, li, tm, final_gain=None):
    final_norm = final_gain is not None
    if not final_norm:
        final_gain = jnp.ones((1, x.shape[1]), F32)
    t, d = x.shape
    _, ne, _, ff = wg.shape
    nt = t // tm
    once = pl.Buffered(1)
    grid_spec = pltpu.PrefetchScalarGridSpec(
        num_scalar_prefetch=1,
        grid=(nt, ne, ff // TF_MOE),
        in_specs=[
            pl.BlockSpec((tm, d), lambda i, e, f, c: (i, 0), pipeline_mode=once),
            pl.BlockSpec((tm, d), lambda i, e, f, c: (i, 0), pipeline_mode=once),
            pl.BlockSpec((None, 1, tm), lambda i, e, f, c: (e, 0, i)),
            pl.BlockSpec((None, 1, tm), lambda i, e, f, c: (e, 0, i)),
            pl.BlockSpec((None, None, d, TF_MOE), lambda i, e, f, c: (li, e, 0, f)),
            pl.BlockSpec((None, None, d, TF_MOE), lambda i, e, f, c: (li, e, 0, f)),
            pl.BlockSpec((None, None, TF_MOE, d), lambda i, e, f, c: (li, e, f, 0)),
            pl.BlockSpec((1, d), lambda i, e, f, c: (0, 0)),
        ],
        out_specs=pl.BlockSpec((tm, d), lambda i, e, f, c: (i, 0), pipeline_mode=once),
        scratch_shapes=[pltpu.VMEM((KMAX_MOE * CH_MOE, d), BF16), pltpu.VMEM((KMAX_MOE * CH_MOE, d), F32)],
    )
    return pl.pallas_call(
        functools.partial(_moe_kernel, final_norm=final_norm),
        grid_spec=grid_spec,
        out_shape=jax.ShapeDtypeStruct((t, d), F32),
        compiler_params=pltpu.CompilerParams(dimension_semantics=("parallel", "arbitrary", "arbitrary"),
                                             vmem_limit_bytes=VMEM_LIMIT_MOE),
        name="moe_experts",
    )(counts, x, h, rank_row, gate_row, wg, wu, wd, final_gain)


def _final_norm_kernel(x_ref, g_ref, o_ref):
    o_ref[...] = _rms(x_ref[...], g_ref[...])


def _final_norm(x, g):
    t, d = x.shape
    tm = TM_PROJ
    return pl.pallas_call(
        _final_norm_kernel,
        grid=(t // tm,),
        in_specs=[pl.BlockSpec((tm, d), lambda i: (i, 0)), pl.BlockSpec((1, d), lambda i: (0, 0))],
        out_specs=pl.BlockSpec((tm, d), lambda i: (i, 0)),
        out_shape=jax.ShapeDtypeStruct((t, d), F32),
        compiler_params=_cparams(("parallel",)),
        name="final_norm",
    )(x, g)


def _block_diag(blocks):
    l, g, r, c = blocks.shape
    on_diag = jnp.eye(g, dtype=bool)[None, :, None, :, None]
    out = jnp.where(on_diag, blocks[:, :, :, None, :], jnp.zeros((), blocks.dtype))
    return out.reshape(l, g * r, g * c)


def kernel(x, norm_mix, w_in, s5_lambda_re, s5_lambda_im, s5_log_dt, s5_b_re, s5_b_im, s5_c_re, s5_c_im,
           s5_d, s5_w_glu, s5_b_glu, c_sinks, w_branch_a, w_branch_b, w_branch_c, w_out, norm_ffn,
           ffn_w_gate, ffn_w_up, ffn_w_down, moe_router, moe_w_gate, moe_w_up, moe_w_down, norm_final):
    b, s, d = x.shape
    depth = w_in.shape[0]
    t = b * s
    n_split = A_QKV + S5_WIDTH + C_WIDTH + 2 * C_KV_HEADS * HEAD_DIM

    w_in_p = jnp.concatenate([w_in[:, :, n_split:], w_in[:, :, :n_split]], axis=-1).astype(BF16)
    wa, wb, wc, wo = (w.astype(BF16) for w in (w_branch_a, w_branch_b, w_branch_c, w_out))
    fg, fu, fd = (w.astype(BF16) for w in (ffn_w_gate, ffn_w_up, ffn_w_down))
    mg, mu, md = (w.astype(BF16) for w in (moe_w_gate, moe_w_up, moe_w_down))
    router_t = jnp.swapaxes(moe_router, 1, 2)
    lam_re = s5_lambda_re.reshape(depth, 1, S5_N)
    lam_im = s5_lambda_im.reshape(depth, 1, S5_N)
    log_dt = jnp.repeat(s5_log_dt, S5_STATE, axis=-1).reshape(depth, 1, S5_N)
    bre = _block_diag(jnp.swapaxes(s5_b_re, 2, 3)).astype(BF16)
    bim = _block_diag(jnp.swapaxes(s5_b_im, 2, 3)).astype(BF16)
    cre = _block_diag(jnp.swapaxes(s5_c_re, 2, 3)).astype(BF16)
    cim = _block_diag(jnp.swapaxes(s5_c_im, 2, 3)).astype(BF16)
    wglu = s5_w_glu.astype(BF16)
    cos, sin = _rope_tables(s)

    xt = x.reshape(t, d)
    for l in range(depth):
        proj = _inproj(xt, norm_mix[l][None, :], w_in_p, l)
        proj3 = proj.reshape(b, s, IN_COLS)
        o_a = _dilated_mixer(proj3).reshape(t, A_WIDTH)
        o_b = _s5_mixer(proj3, lam_re[l], lam_im[l], log_dt[l], bre[l], bim[l], cre[l], cim[l],
                        s5_d[l][None, :], wglu[l], s5_b_glu[l][None, :]).reshape(t, S5_WIDTH)
        o_c = _swa_mixer(proj3, c_sinks[l], cos, sin).reshape(t, C_WIDTH)
        i = l // 2
        if l % 2 == 0:
            xt, h = _merge(xt, o_a, o_b, o_c, proj, wa[l], wb[l], wc[l], wo[l], norm_ffn[l][None, :])
            xt = _ffn(xt, h, fg, fu, fd, i)
        else:
            xt, h, logits_t = _merge(xt, o_a, o_b, o_c, proj, wa[l], wb[l], wc[l], wo[l],
                                     norm_ffn[l][None, :], router_t[i])
            gate, rank, cnt = _route(logits_t, TM_MOE)
            counts = cnt[:, :, 0].reshape(-1)
            last = l == depth - 1
            xt = _moe(xt, h, counts, rank[:, None, :], gate[:, None, :], mg, mu, md, i, TM_MOE,
                      final_gain=norm_final[None, :] if last else None)
    if depth % 2 == 1:
        xt = _final_norm(xt, norm_final[None, :])
    return xt.reshape(b, s, d)
```

```python
import functools
import math

import jax
import jax.numpy as jnp
import numpy as np
from jax import lax
from jax.experimental import pallas as pl
from jax.experimental.pallas import tpu as pltpu

F32 = jnp.float32
BF16 = jnp.bfloat16

D_MODEL = 1024
HEAD_DIM = 64
BLOCK = 128
LANES = 128
SUBLANES = 8
DILATIONS = (1, 4, 16)
N_DIL = 3
A_HEADS = 4
A_WIDTH = A_HEADS * HEAD_DIM
A_QKV = 3 * N_DIL * A_WIDTH
S5_WIDTH = 256
S5_GROUPS = 16
S5_GROUP_CH = 16
S5_STATE = 64
S5_N = S5_GROUPS * S5_STATE
C_Q_HEADS = 8
C_KV_HEADS = 2
C_WIDTH = C_Q_HEADS * HEAD_DIM
ROPE_THETA = 150000.0
N_GATE = 3 * D_MODEL
IN_COLS = 6400
D_FF = 3584
N_EXPERTS = 8
RMS_EPS = 1e-6
ATT_SCALE = HEAD_DIM ** -0.5

OFF_GATE = 0
OFF_A = N_GATE
OFF_B = OFF_A + A_QKV
OFF_Q = OFF_B + S5_WIDTH
OFF_K = OFF_Q + C_WIDTH
OFF_V = OFF_K + C_KV_HEADS * HEAD_DIM

TM_PROJ = 2048
TN_PROJ = 1280
TM_MERGE = 1024
TM_FFN = 1024
TF_FFN = 1792
TS_S5 = 1024
TM_MOE = 2048
TF_MOE = 1792
FF_SUB_MOE = ((0, 768), (768, 1024))
CH_MOE = 128
KMIN_MOE = 4
KMAX_MOE = 5
SC_MOE = 512
VMEM_LIMIT = 56 * 1024 * 1024
VMEM_LIMIT_MOE = 60 * 1024 * 1024


def _cparams(sem):
    return pltpu.CompilerParams(dimension_semantics=sem, vmem_limit_bytes=VMEM_LIMIT)


def _dot(a, b):
    return jnp.dot(a, b, preferred_element_type=F32)


def _dot_nt(a, b):
    return lax.dot_general(a, b, (((1,), (1,)), ((), ())), preferred_element_type=F32)


def _rms(x, g):
    return x * lax.rsqrt(jnp.mean(x * x, axis=-1, keepdims=True) + RMS_EPS) * g


def _inproj_kernel(x_ref, g_ref, w_ref, o_ref, h_ref):
    @pl.when(pl.program_id(1) == 0)
    def _():
        h_ref[...] = _rms(x_ref[...], g_ref[...]).astype(BF16)

    o_ref[...] = _dot(h_ref[...], w_ref[...]).astype(o_ref.dtype)


def _inproj(x, g, w, l):
    t, d = x.shape
    n = w.shape[2]
    return pl.pallas_call(
        _inproj_kernel,
        grid=(t // TM_PROJ, n // TN_PROJ),
        in_specs=[
            pl.BlockSpec((TM_PROJ, d), lambda i, j: (i, 0)),
            pl.BlockSpec((1, d), lambda i, j: (0, 0)),
            pl.BlockSpec((None, d, TN_PROJ), lambda i, j: (l, 0, j)),
        ],
        out_specs=pl.BlockSpec((TM_PROJ, TN_PROJ), lambda i, j: (i, j)),
        out_shape=jax.ShapeDtypeStruct((t, n), BF16),
        scratch_shapes=[pltpu.VMEM((TM_PROJ, d), BF16)],
        compiler_params=_cparams(("parallel", "arbitrary")),
        name="inproj",
    )(x, g, w)


def _band_mask(rows, width):
    qi = lax.broadcasted_iota(jnp.int32, (rows, width), 0) % BLOCK
    ki = lax.broadcasted_iota(jnp.int32, (rows, width), 1)
    return (ki >= qi) & (ki <= qi + BLOCK)


def _causal_mask(rows, width):
    qi = lax.broadcasted_iota(jnp.int32, (rows, width), 0) % BLOCK
    ki = lax.broadcasted_iota(jnp.int32, (rows, width), 1)
    return ki <= qi


def _softmax_pv(s, mask, v, sink=None):
    s = jnp.where(mask, s, -jnp.inf)
    m = jnp.max(s, axis=-1, keepdims=True)
    if sink is not None:
        m = jnp.maximum(m, sink)
    p = jnp.exp(s - m)
    den = jnp.sum(p, axis=-1, keepdims=True)
    if sink is not None:
        den = den + jnp.exp(sink - m)
    o = _dot(p.astype(BF16), v)
    return o, m, den


def _dilated_kernel(q0_ref, q1_ref, q2_ref, k0_ref, k1_ref, k2_ref, v0_ref, v1_ref, v2_ref,
                    o_ref, qf_ref, kf_ref, vf_ref, og_ref, lg_ref):
    seq = o_ref.shape[0]
    lane = lax.broadcasted_iota(jnp.int32, (1, LANES), 1)
    lo = lane < HEAD_DIM
    band = _band_mask(2 * BLOCK, 2 * BLOCK)
    causal = _causal_mask(2 * BLOCK, BLOCK)

    def attend(q, k, v, mask):
        zero = jnp.zeros_like(q)
        qs = jnp.concatenate([jnp.where(lo, q, zero), jnp.where(lo, zero, q)], axis=0)
        s = _dot_nt(qs, k)
        o, m, den = _softmax_pv(s, mask, v)
        o = o * (1.0 / den)
        lse = m + jnp.log(den)
        o_pair = jnp.where(lo, o[:BLOCK], o[BLOCK:])
        l_pair = jnp.where(lo, lse[:BLOCK], lse[BLOCK:])
        return o_pair, l_pair

    for idx, (qr, kr, vr) in enumerate(((q1_ref, k1_ref, v1_ref), (q2_ref, k2_ref, v2_ref))):
        qf_ref[idx] = qr[...].astype(F32) * ATT_SCALE
        kf_ref[idx] = kr[...].astype(F32)
        vf_ref[idx] = vr[...].astype(F32)

    o_p, l_p = attend(q0_ref[pl.ds(0, BLOCK), :] * ATT_SCALE, k0_ref[pl.ds(0, BLOCK), :],
                      v0_ref[pl.ds(0, BLOCK), :], causal)
    og_ref[0, pl.ds(0, BLOCK), :] = o_p
    lg_ref[0, pl.ds(0, BLOCK), :] = l_p

    for n in range(1, seq // BLOCK):
        r0, w0 = n * BLOCK, (n - 1) * BLOCK
        o_p, l_p = attend(q0_ref[pl.ds(r0, BLOCK), :] * ATT_SCALE, k0_ref[pl.ds(w0, 2 * BLOCK), :],
                          v0_ref[pl.ds(w0, 2 * BLOCK), :], band)
        og_ref[0, pl.ds(r0, BLOCK), :] = o_p
        lg_ref[0, pl.ds(r0, BLOCK), :] = l_p

    for idx, d in ((0, DILATIONS[1]), (1, DILATIONS[2])):
        g = idx + 1
        nblk = seq // d // BLOCK
        for r in range(d):
            for n in range(nblk):
                q = qf_ref[idx, pl.ds(r + n * BLOCK * d, BLOCK, stride=d), :].astype(BF16)
                if n == 0:
                    k = kf_ref[idx, pl.ds(r, BLOCK, stride=d), :].astype(BF16)
                    v = vf_ref[idx, pl.ds(r, BLOCK, stride=d), :].astype(BF16)
                    o_p, l_p = attend(q, k, v, causal)
                else:
                    w0 = r + (n - 1) * BLOCK * d
                    k = kf_ref[idx, pl.ds(w0, 2 * BLOCK, stride=d), :].astype(BF16)
                    v = vf_ref[idx, pl.ds(w0, 2 * BLOCK, stride=d), :].astype(BF16)
                    o_p, l_p = attend(q, k, v, band)
                og_ref[g, pl.ds(r + n * BLOCK * d, BLOCK, stride=d), :] = o_p
                lg_ref[g, pl.ds(r + n * BLOCK * d, BLOCK, stride=d), :] = l_p

    l0, l1, l2 = lg_ref[0], lg_ref[1], lg_ref[2]
    mx = jnp.maximum(jnp.maximum(l0, l1), l2)
    e0, e1, e2 = jnp.exp(l0 - mx), jnp.exp(l1 - mx), jnp.exp(l2 - mx)
    tot = e0 + e1 + e2
    o_ref[...] = ((e0 * og_ref[0] + e1 * og_ref[1] + e2 * og_ref[2]) / tot).astype(o_ref.dtype)


def _dilated_mixer(proj3):
    b, s, _ = proj3.shape
    base = OFF_A // LANES

    def spec(which, g):
        col = base + which * (N_DIL * A_WIDTH // LANES) + g * (A_WIDTH // LANES)
        return pl.BlockSpec((None, s, LANES), lambda i, hp, col=col: (i, 0, col + hp))

    in_specs = [spec(w, g) for w in range(3) for g in range(N_DIL)]
    return pl.pallas_call(
        _dilated_kernel,
        grid=(b, A_WIDTH // LANES),
        in_specs=in_specs,
        out_specs=pl.BlockSpec((None, s, LANES), lambda i, hp: (i, 0, hp)),
        out_shape=jax.ShapeDtypeStruct((b, s, A_WIDTH), BF16),
        scratch_shapes=[
            pltpu.VMEM((2, s, LANES), F32),
            pltpu.VMEM((2, s, LANES), F32),
            pltpu.VMEM((2, s, LANES), F32),
            pltpu.VMEM((N_DIL, s, LANES), F32),
            pltpu.VMEM((N_DIL, s, LANES), F32),
        ],
        compiler_params=_cparams(("parallel", "parallel")),
        name="dilated_mixer",
    )(*([proj3] * 9))


def _swa_kernel(sink_ref, q_ref, k_ref, v_ref, cos_ref, sin_ref, o_ref, qs_ref, ks_ref, vs_ref):
    seq = o_ref.shape[0]
    g = pl.program_id(1)
    rep = C_Q_HEADS // C_KV_HEADS
    lane = lax.broadcasted_iota(jnp.int32, (1, LANES), 1)
    lo = lane < HEAD_DIM
    cos = cos_ref[...]
    sin = sin_ref[...]

    src = lax.broadcasted_iota(jnp.int32, (LANES, LANES), 0)
    dst = lax.broadcasted_iota(jnp.int32, (LANES, LANES), 1)
    half = HEAD_DIM // 2
    partner = jnp.where((dst % HEAD_DIM) < half, dst + half, dst - half)
    swap_mat = jnp.where(src == partner, 1.0, 0.0).astype(BF16)
    rep_mat = jnp.where(src == g * HEAD_DIM + dst % HEAD_DIM, 1.0, 0.0).astype(BF16)

    def rope(x):
        return x.astype(F32) * cos + _dot(x, swap_mat) * sin

    ks_ref[...] = _dot(rope(k_ref[...]).astype(BF16), rep_mat).astype(BF16)
    vs_ref[...] = _dot(v_ref[...], rep_mat).astype(BF16)
    for c in range(rep // 2):
        sl = slice(c * LANES, (c + 1) * LANES)
        qs_ref[:, sl] = (rope(q_ref[:, sl]) * ATT_SCALE).astype(BF16)

    rows = rep * BLOCK
    hrow = lax.broadcasted_iota(jnp.int32, (rows, 1), 0) // BLOCK
    sink = jnp.zeros((rows, 1), F32)
    for h in range(rep):
        sink = jnp.where(hrow == h, sink_ref[g * rep + h], sink)
    band = _band_mask(rows, 2 * BLOCK)
    causal = _causal_mask(rows, BLOCK)

    def block(r0, k, v, mask):
        parts = []
        for h in range(rep):
            q = qs_ref[pl.ds(r0, BLOCK), (h // 2) * LANES:(h // 2 + 1) * LANES]
            zero = jnp.zeros_like(q)
            parts.append(jnp.where(lo, q, zero) if h % 2 == 0 else jnp.where(lo, zero, q))
        s = _dot_nt(jnp.concatenate(parts, axis=0), k)
        o, _, den = _softmax_pv(s, mask, v, sink)
        o = o * (1.0 / den)
        for c in range(rep // 2):
            pair = jnp.where(lo, o[2 * c * BLOCK:(2 * c + 1) * BLOCK], o[(2 * c + 1) * BLOCK:(2 * c + 2) * BLOCK])
            o_ref[pl.ds(r0, BLOCK), c * LANES:(c + 1) * LANES] = pair.astype(o_ref.dtype)

    block(0, ks_ref[pl.ds(0, BLOCK), :], vs_ref[pl.ds(0, BLOCK), :], causal)

    def body(n, carry):
        r0 = pl.multiple_of(n * BLOCK, BLOCK)
        w0 = pl.multiple_of((n - 1) * BLOCK, BLOCK)
        block(r0, ks_ref[pl.ds(w0, 2 * BLOCK), :], vs_ref[pl.ds(w0, 2 * BLOCK), :], band)
        return carry

    lax.fori_loop(1, seq // BLOCK, body, 0, unroll=3)


def _rope_tables(seq):
    inv = ROPE_THETA ** (-jnp.arange(0, HEAD_DIM, 2, dtype=F32) / HEAD_DIM)
    ang = jnp.arange(seq, dtype=F32)[:, None] * inv[None, :]
    reps = LANES // (HEAD_DIM // 2)
    cos = jnp.tile(jnp.cos(ang), (1, reps))
    sign = jnp.where((jnp.arange(LANES) % HEAD_DIM) < HEAD_DIM // 2, -1.0, 1.0).astype(F32)
    sin = jnp.tile(jnp.sin(ang), (1, reps)) * sign[None, :]
    return cos, sin


def _swa_mixer(proj3, sinks, cos, sin):
    b, s, _ = proj3.shape
    qw = C_WIDTH // C_KV_HEADS
    return pl.pallas_call(
        _swa_kernel,
        grid=(b, C_KV_HEADS),
        in_specs=[
            pl.BlockSpec(memory_space=pltpu.SMEM),
            pl.BlockSpec((None, s, qw), lambda i, g: (i, 0, OFF_Q // qw + g)),
            pl.BlockSpec((None, s, LANES), lambda i, g: (i, 0, OFF_K // LANES)),
            pl.BlockSpec((None, s, LANES), lambda i, g: (i, 0, OFF_V // LANES)),
            pl.BlockSpec((s, LANES), lambda i, g: (0, 0)),
            pl.BlockSpec((s, LANES), lambda i, g: (0, 0)),
        ],
        out_specs=pl.BlockSpec((None, s, qw), lambda i, g: (i, 0, g)),
        out_shape=jax.ShapeDtypeStruct((b, s, C_WIDTH), BF16),
        scratch_shapes=[
            pltpu.VMEM((s, qw), BF16),
            pltpu.VMEM((s, LANES), BF16),
            pltpu.VMEM((s, LANES), BF16),
        ],
        compiler_params=_cparams(("parallel", "parallel")),
        name="swa_mixer",
    )(sinks, proj3, proj3, proj3, cos, sin)


def _cmul(ar, ai, br, bi):
    return ar * br - ai * bi, ar * bi + ai * br


def _s5_kernel(u_ref, lr_ref, li_ref, ldt_ref, bre_ref, bim_ref, cre_ref, cim_ref, d_ref, wg_ref, bg_ref,
               o_ref, hr_ref, hi_ref, pw_ref, car_ref):
    ts = u_ref.shape[0]

    @pl.when(pl.program_id(1) == 0)
    def _():
        lr, li = lr_ref[...], li_ref[...]
        dt = jnp.exp(ldt_ref[...])
        mag = jnp.exp(lr * dt)
        a_re, a_im = mag * jnp.cos(li * dt), mag * jnp.sin(li * dt)
        nr, ni = a_re - 1.0, a_im
        den = lr * lr + li * li
        z_re = (nr * lr + ni * li) / den
        z_im = (ni * lr - nr * li) / den
        row = lax.broadcasted_iota(jnp.int32, (SUBLANES, 1), 0)
        pows = [(a_re, a_im)]
        for _ in range(SUBLANES - 1):
            pows.append(_cmul(pows[-1][0], pows[-1][1], a_re, a_im))
        for j, sft in enumerate((1, 2, 4)):
            pr, pi = pows[sft - 1]
            pw_ref[2 * j] = jnp.where(row >= sft, pr, 0.0)
            pw_ref[2 * j + 1] = jnp.where(row >= sft, pi, 0.0)
        cr = jnp.zeros((SUBLANES, S5_N), F32)
        ci = jnp.zeros((SUBLANES, S5_N), F32)
        for i in range(SUBLANES):
            cr = jnp.where(row == i, pows[i][0], cr)
            ci = jnp.where(row == i, pows[i][1], ci)
        pw_ref[6] = cr
        pw_ref[7] = ci
        pw_ref[8] = jnp.broadcast_to(z_re, (SUBLANES, S5_N))
        pw_ref[9] = jnp.broadcast_to(z_im, (SUBLANES, S5_N))
        car_ref[...] = jnp.zeros_like(car_ref)

    u = u_ref[...]
    bu_re = _dot(u, bre_ref[...])
    bu_im = _dot(u, bim_ref[...])
    z_re = pw_ref[8, 0:1, :]
    z_im = pw_ref[9, 0:1, :]
    hr_ref[...] = z_re * bu_re - z_im * bu_im
    hi_ref[...] = z_re * bu_im + z_im * bu_re

    mults = [(pw_ref[2 * j], pw_ref[2 * j + 1]) for j in range(3)]
    acr, aci = pw_ref[6], pw_ref[7]

    def chunk(k, carry):
        cr, ci = carry
        r0 = pl.multiple_of(k * SUBLANES, SUBLANES)
        xr = hr_ref[pl.ds(r0, SUBLANES), :]
        xi = hi_ref[pl.ds(r0, SUBLANES), :]
        for (mr, mi), sft in zip(mults, (1, 2, 4)):
            sr = pltpu.roll(xr, sft, 0)
            si = pltpu.roll(xi, sft, 0)
            xr, xi = xr + (mr * sr - mi * si), xi + (mr * si + mi * sr)
        xr = xr + (acr * cr - aci * ci)
        xi = xi + (acr * ci + aci * cr)
        hr_ref[pl.ds(r0, SUBLANES), :] = xr
        hi_ref[pl.ds(r0, SUBLANES), :] = xi
        return xr[SUBLANES - 1:SUBLANES, :], xi[SUBLANES - 1:SUBLANES, :]

    cr, ci = lax.fori_loop(0, ts // SUBLANES, chunk, (car_ref[0, 0:1, :], car_ref[1, 0:1, :]), unroll=2)
    car_ref[0] = jnp.broadcast_to(cr, (SUBLANES, S5_N))
    car_ref[1] = jnp.broadcast_to(ci, (SUBLANES, S5_N))

    y = _dot(hr_ref[...].astype(BF16), cre_ref[...]) - _dot(hi_ref[...].astype(BF16), cim_ref[...])
    y = y + d_ref[...] * u.astype(F32)
    gl = jax.nn.gelu(y)
    out = gl * jax.nn.sigmoid(_dot(gl.astype(BF16), wg_ref[...]) + bg_ref[...])
    o_ref[...] = out.astype(o_ref.dtype)


def _s5_mixer(proj3, lr, li, ldt, bre, bim, cre, cim, dsk, wglu, bglu):
    b, s, _ = proj3.shape
    full = lambda shape: pl.BlockSpec(shape, lambda i, t: (0,) * len(shape))
    return pl.pallas_call(
        _s5_kernel,
        grid=(b, s // TS_S5),
        in_specs=[
            pl.BlockSpec((None, TS_S5, S5_WIDTH), lambda i, t: (i, t, OFF_B // S5_WIDTH)),
            full((1, S5_N)), full((1, S5_N)), full((1, S5_N)),
            full((S5_WIDTH, S5_N)), full((S5_WIDTH, S5_N)),
            full((S5_N, S5_WIDTH)), full((S5_N, S5_WIDTH)),
            full((1, S5_WIDTH)), full((S5_WIDTH, S5_WIDTH)), full((1, S5_WIDTH)),
        ],
        out_specs=pl.BlockSpec((None, TS_S5, S5_WIDTH), lambda i, t: (i, t, 0)),
        out_shape=jax.ShapeDtypeStruct((b, s, S5_WIDTH), BF16),
        scratch_shapes=[
            pltpu.VMEM((TS_S5, S5_N), F32),
            pltpu.VMEM((TS_S5, S5_N), F32),
            pltpu.VMEM((10, SUBLANES, S5_N), F32),
            pltpu.VMEM((2, SUBLANES, S5_N), F32),
        ],
        compiler_params=_cparams(("parallel", "arbitrary")),
        name="s5_mixer",
    )(proj3, lr, li, ldt, bre, bim, cre, cim, dsk, wglu, bglu)


def _merge_kernel(x_ref, oa_ref, ob_ref, oc_ref, ga_ref, gb_ref, gc_ref, wa_ref, wb_ref, wc_ref, wo_ref,
                  gn_ref, *rest, with_router):
    if with_router:
        rt_ref, xo_ref, h_ref, lg_ref = rest
    else:
        xo_ref, h_ref = rest

    def sig(r):
        return jax.nn.sigmoid(r[...].astype(F32))

    merged = (sig(ga_ref) * _dot(oa_ref[...], wa_ref[...])
              + sig(gb_ref) * _dot(ob_ref[...], wb_ref[...])
              + sig(gc_ref) * _dot(oc_ref[...], wc_ref[...]))
    xn = x_ref[...] + _dot(merged.astype(BF16), wo_ref[...])
    xo_ref[...] = xn
    h = _rms(xn, gn_ref[...])
    h_hi = h.astype(BF16)
    h_ref[...] = h_hi
    if with_router:
        h_lo = (h - h_hi.astype(F32)).astype(BF16)
        rt = rt_ref[...]
        r_hi = rt.astype(BF16)
        r_lo = (rt - r_hi.astype(F32)).astype(BF16)
        lg_ref[...] = _dot_nt(r_hi, h_hi) + (_dot_nt(r_hi, h_lo) + _dot_nt(r_lo, h_hi))


def _merge(x, oa, ob, oc, proj, wa, wb, wc, wo, gn, router_t=None):
    t, d = x.shape
    tm = TM_MERGE
    with_router = router_t is not None
    row = lambda w: pl.BlockSpec((tm, w), lambda i: (i, 0))
    full = lambda a: pl.BlockSpec(a.shape, lambda i: (0, 0))
    in_specs = [row(d), row(A_WIDTH), row(S5_WIDTH), row(C_WIDTH)]
    in_specs += [pl.BlockSpec((tm, d), lambda i, c=c: (i, c)) for c in range(3)]
    in_specs += [full(wa), full(wb), full(wc), full(wo), full(gn)]
    args = [x, oa, ob, oc, proj, proj, proj, wa, wb, wc, wo, gn]
    out_specs = [row(d), row(d)]
    out_shape = [jax.ShapeDtypeStruct((t, d), F32), jax.ShapeDtypeStruct((t, d), BF16)]
    if with_router:
        in_specs.append(full(router_t))
        args.append(router_t)
        out_specs.append(pl.BlockSpec((N_EXPERTS, tm), lambda i: (0, i)))
        out_shape.append(jax.ShapeDtypeStruct((N_EXPERTS, t), F32))
    return pl.pallas_call(
        functools.partial(_merge_kernel, with_router=with_router),
        grid=(t // tm,),
        in_specs=in_specs,
        out_specs=out_specs,
        out_shape=out_shape,
        compiler_params=_cparams(("parallel",)),
        name="merge_router" if with_router else "merge",
    )(*args)


def _ffn_kernel(x_ref, h_ref, wg_ref, wu_ref, wd_ref, o_ref):
    @pl.when(pl.program_id(1) == 0)
    def _():
        o_ref[...] = x_ref[...]

    h = h_ref[...]
    for c0, cw in FF_SUB_MOE:
        act = jax.nn.silu(_dot(h, wg_ref[:, c0:c0 + cw])) * _dot(h, wu_ref[:, c0:c0 + cw])
        o_ref[...] += _dot(act.astype(BF16), wd_ref[c0:c0 + cw, :])


def _ffn(x, h, wg, wu, wd, li):
    t, d = x.shape
    ff = wg.shape[2]
    return pl.pallas_call(
        _ffn_kernel,
        grid=(t // TM_FFN, ff // TF_FFN),
        in_specs=[
            pl.BlockSpec((TM_FFN, d), lambda i, f: (i, 0)),
            pl.BlockSpec((TM_FFN, d), lambda i, f: (i, 0)),
            pl.BlockSpec((None, d, TF_FFN), lambda i, f: (li, 0, f)),
            pl.BlockSpec((None, d, TF_FFN), lambda i, f: (li, 0, f)),
            pl.BlockSpec((None, TF_FFN, d), lambda i, f: (li, f, 0)),
        ],
        out_specs=pl.BlockSpec((TM_FFN, d), lambda i, f: (i, 0)),
        out_shape=jax.ShapeDtypeStruct((t, d), F32),
        compiler_params=_cparams(("parallel", "arbitrary")),
        name="dense_ffn",
    )(x, h, wg, wu, wd)


def _route_kernel(lg_ref, gate_ref, rank_ref, cnt_ref):
    lg = lg_ref[...]
    tm = lg.shape[1]
    eidx = lax.broadcasted_iota(jnp.int32, lg.shape, 0)
    m1 = jnp.max(lg, axis=0, keepdims=True)
    i1 = jnp.min(jnp.where(lg == m1, eidx, N_EXPERTS), axis=0, keepdims=True)
    rest = jnp.where(eidx == i1, -jnp.inf, lg)
    m2 = jnp.max(rest, axis=0, keepdims=True)
    i2 = jnp.min(jnp.where(rest == m2, eidx, N_EXPERTS), axis=0, keepdims=True)
    e2 = jnp.exp(m2 - m1)
    tot = 1.0 + e2
    sel1 = eidx == i1
    sel2 = eidx == i2
    gate_ref[...] = jnp.where(sel1, 1.0 / tot, jnp.where(sel2, e2 / tot, 0.0))
    sel = (sel1 | sel2).astype(jnp.int32)
    lane = lax.broadcasted_iota(jnp.int32, lg.shape, 1)
    c = sel
    sft = 1
    while sft < tm:
        c = c + jnp.where(lane >= sft, pltpu.roll(c, sft, 1), 0)
        sft *= 2
    rank_ref[...] = jnp.where(sel > 0, c - 1, -1)
    cnt_ref[...] = jnp.broadcast_to(jnp.sum(sel, axis=1, keepdims=True), cnt_ref.shape)


def _route(logits_t, tm):
    e, t = logits_t.shape
    nt = t // tm
    return pl.pallas_call(
        _route_kernel,
        grid=(nt,),
        in_specs=[pl.BlockSpec((e, tm), lambda i: (0, i))],
        out_specs=[
            pl.BlockSpec((e, tm), lambda i: (0, i)),
            pl.BlockSpec((e, tm), lambda i: (0, i)),
            pl.BlockSpec((None, e, LANES), lambda i: (i, 0, 0)),
        ],
        out_shape=[
            jax.ShapeDtypeStruct((e, t), F32),
            jax.ShapeDtypeStruct((e, t), jnp.int32),
            jax.ShapeDtypeStruct((nt, e, LANES), jnp.int32),
        ],
        compiler_params=_cparams(("parallel",)),
        name="route_top2",
    )(logits_t)


def _moe_kernel(cnt_ref, x_ref, h_ref, rrow_ref, grow_ref, wg_ref, wu_ref, wd_ref, gfin_ref,
                o_ref, hc_ref, y_ref, *, final_norm):
    i, e, f = pl.program_id(0), pl.program_id(1), pl.program_id(2)
    ne, nf = pl.num_programs(1), pl.num_programs(2)
    tm = h_ref.shape[0]
    n_sel = cnt_ref[i * N_EXPERTS + e]
    n_chunks = (n_sel + (CH_MOE - 1)) // CH_MOE
    k_path = jnp.clip(n_chunks, KMIN_MOE, KMAX_MOE)
    n_over = jnp.maximum(n_chunks - KMAX_MOE, 0)

    def expert_part(hc):
        part = None
        for c0, cw in FF_SUB_MOE:
            act = jax.nn.silu(_dot(hc, wg_ref[:, c0:c0 + cw])) * _dot(hc, wu_ref[:, c0:c0 + cw])
            p = _dot(act.astype(BF16), wd_ref[c0:c0 + cw, :])
            part = p if part is None else part + p
        return part

    def picks(r0, rows):
        slot = lax.broadcasted_iota(jnp.int32, (rows, tm), 0) + r0
        return slot == rrow_ref[...]

    def compact(r0, rows):
        onehot = jnp.where(picks(r0, rows), 1.0, 0.0).astype(BF16)
        return _dot(onehot, h_ref[...]).astype(BF16)

    def scatter_add(r0, y):
        sel = picks(r0, y.shape[0])
        gate_c = jnp.sum(jnp.where(sel, grow_ref[...], 0.0), axis=1, keepdims=True)
        yb = (y * gate_c).astype(BF16)
        onehot = jnp.where(sel, 1.0, 0.0).astype(BF16)
        for q in range(tm // SC_MOE):
            cols = slice(q * SC_MOE, (q + 1) * SC_MOE)
            o_ref[pl.ds(q * SC_MOE, SC_MOE), :] += lax.dot_general(
                onehot[:, cols], yb, (((0,), (0,)), ((), ())), preferred_element_type=F32)

    @pl.when((e == 0) & (f == 0))
    def _():
        o_ref[...] = x_ref[...]

    def static_path(rows):
        sl = pl.ds(0, rows)

        @pl.when(f == 0)
        def _():
            hc_ref[sl, :] = compact(0, rows)
            y_ref[sl, :] = jnp.zeros((rows, D_MODEL), F32)

        y_ref[sl, :] += expert_part(hc_ref[sl, :])

        @pl.when(f == nf - 1)
        def _():
            scatter_add(0, y_ref[sl, :])

    for k in range(KMIN_MOE, KMAX_MOE + 1):
        pl.when(k_path == k)(functools.partial(static_path, k * CH_MOE))

    def overflow(c, carry):
        r0 = (KMAX_MOE + c) * CH_MOE
        scatter_add(r0, expert_part(compact(r0, CH_MOE)))
        return carry

    lax.fori_loop(0, n_over, overflow, 0)

    if final_norm:
        @pl.when((e == ne - 1) & (f == nf - 1))
        def _():
            o_ref[...] = _rms(o_ref[...], gfin_ref[...])


def _moe(x, h, counts, rank_row, gate_row, wg, wu, wd, li, tm, final_gain=None):
    final_norm = final_gain is not None
    if not final_norm:
        final_gain = jnp.ones((1, x.shape[1]), F32)
    t, d = x.shape
    _, ne, _, ff = wg.shape
    nt = t // tm
    once = pl.Buffered(1)
    grid_spec = pltpu.PrefetchScalarGridSpec(
        num_scalar_prefetch=1,
        grid=(nt, ne, ff // TF_MOE),
        in_specs=[
            pl.BlockSpec((tm, d), lambda i, e, f, c: (i, 0), pipeline_mode=once),
            pl.BlockSpec((tm, d), lambda i, e, f, c: (i, 0)),
            pl.BlockSpec((None, 1, tm), lambda i, e, f, c: (e, 0, i)),
            pl.BlockSpec((None, 1, tm), lambda i, e, f, c: (e, 0, i)),
            pl.BlockSpec((None, None, d, TF_MOE), lambda i, e, f, c: (li, e, 0, f)),
            pl.BlockSpec((None, None, d, TF_MOE), lambda i, e, f, c: (li, e, 0, f)),
            pl.BlockSpec((None, None, TF_MOE, d), lambda i, e, f, c: (li, e, f, 0)),
            pl.BlockSpec((1, d), lambda i, e, f, c: (0, 0)),
        ],
        out_specs=pl.BlockSpec((tm, d), lambda i, e, f, c: (i, 0), pipeline_mode=once),
        scratch_shapes=[pltpu.VMEM((KMAX_MOE * CH_MOE, d), BF16), pltpu.VMEM((KMAX_MOE * CH_MOE, d), F32)],
    )
    return pl.pallas_call(
        functools.partial(_moe_kernel, final_norm=final_norm),
        grid_spec=grid_spec,
        out_shape=jax.ShapeDtypeStruct((t, d), F32),
        compiler_params=pltpu.CompilerParams(dimension_semantics=("parallel", "arbitrary", "arbitrary"),
                                             vmem_limit_bytes=VMEM_LIMIT_MOE),
        name="moe_experts",
    )(counts, x, h, rank_row, gate_row, wg, wu, wd, final_gain)


def _final_norm_kernel(x_ref, g_ref, o_ref):
    o_ref[...] = _rms(x_ref[...], g_ref[...])


def _final_norm(x, g):
    t, d = x.shape
    tm = TM_PROJ
    return pl.pallas_call(
        _final_norm_kernel,
        grid=(t // tm,),
        in_specs=[pl.BlockSpec((tm, d), lambda i: (i, 0)), pl.BlockSpec((1, d), lambda i: (0, 0))],
        out_specs=pl.BlockSpec((tm, d), lambda i: (i, 0)),
        out_shape=jax.ShapeDtypeStruct((t, d), F32),
        compiler_params=_cparams(("parallel",)),
        name="final_norm",
    )(x, g)


def _block_diag(blocks):
    l, g, r, c = blocks.shape
    on_diag = jnp.eye(g, dtype=bool)[None, :, None, :, None]
    out = jnp.where(on_diag, blocks[:, :, :, None, :], jnp.zeros((), blocks.dtype))
    return out.reshape(l, g * r, g * c)


def kernel(x, norm_mix, w_in, s5_lambda_re, s5_lambda_im, s5_log_dt, s5_b_re, s5_b_im, s5_c_re, s5_c_im,
           s5_d, s5_w_glu, s5_b_glu, c_sinks, w_branch_a, w_branch_b, w_branch_c, w_out, norm_ffn,
           ffn_w_gate, ffn_w_up, ffn_w_down, moe_router, moe_w_gate, moe_w_up, moe_w_down, norm_final):
    b, s, d = x.shape
    depth = w_in.shape[0]
    t = b * s
    n_split = A_QKV + S5_WIDTH + C_WIDTH + 2 * C_KV_HEADS * HEAD_DIM

    w_in_p = jnp.concatenate([w_in[:, :, n_split:], w_in[:, :, :n_split]], axis=-1).astype(BF16)
    wa, wb, wc, wo = (w.astype(BF16) for w in (w_branch_a, w_branch_b, w_branch_c, w_out))
    fg, fu, fd = (w.astype(BF16) for w in (ffn_w_gate, ffn_w_up, ffn_w_down))
    mg, mu, md = (w.astype(BF16) for w in (moe_w_gate, moe_w_up, moe_w_down))
    router_t = jnp.swapaxes(moe_router, 1, 2)
    lam_re = s5_lambda_re.reshape(depth, 1, S5_N)
    lam_im = s5_lambda_im.reshape(depth, 1, S5_N)
    log_dt = jnp.repeat(s5_log_dt, S5_STATE, axis=-1).reshape(depth, 1, S5_N)
    bre = _block_diag(jnp.swapaxes(s5_b_re, 2, 3)).astype(BF16)
    bim = _block_diag(jnp.swapaxes(s5_b_im, 2, 3)).astype(BF16)
    cre = _block_diag(jnp.swapaxes(s5_c_re, 2, 3)).astype(BF16)
    cim = _block_diag(jnp.swapaxes(s5_c_im, 2, 3)).astype(BF16)
    wglu = s5_w_glu.astype(BF16)
    cos, sin = _rope_tables(s)

    xt = x.reshape(t, d)
    for l in range(depth):
        proj = _inproj(xt, norm_mix[l][None, :], w_in_p, l)
        proj3 = proj.reshape(b, s, IN_COLS)
        o_a = _dilated_mixer(proj3).reshape(t, A_WIDTH)
        o_b = _s5_mixer(proj3, lam_re[l], lam_im[l], log_dt[l], bre[l], bim[l], cre[l], cim[l],
                        s5_d[l][None, :], wglu[l], s5_b_glu[l][None, :]).reshape(t, S5_WIDTH)
        o_c = _swa_mixer(proj3, c_sinks[l], cos, sin).reshape(t, C_WIDTH)
        i = l // 2
        if l % 2 == 0:
            xt, h = _merge(xt, o_a, o_b, o_c, proj, wa[l], wb[l], wc[l], wo[l], norm_ffn[l][None, :])
            xt = _ffn(xt, h, fg, fu, fd, i)
        else:
            xt, h, logits_t = _merge(xt, o_a, o_b, o_c, proj, wa[l], wb[l], wc[l], wo[l],
                                     norm_ffn[l][None, :], router_t[i])
            gate, rank, cnt = _route(logits_t, TM_MOE)
            counts = cnt[:, :, 0].reshape(-1)
            last = l == depth - 1
            xt = _moe(xt, h, counts, rank[:, None, :], gate[:, None, :], mg, mu, md, i, TM_MOE,
                      final_gain=norm_final[None, :] if last else None)
    if depth % 2 == 1:
        xt = _final_norm(xt, norm_final[None, :])
    return xt.reshape(b, s, d)
```

```python
import functools
import math

import jax
import jax.numpy as jnp
import numpy as np
from jax import lax
from jax.experimental import pallas as pl
from jax.experimental.pallas import tpu as pltpu

F32 = jnp.float32
BF16 = jnp.bfloat16

D_MODEL = 1024
HEAD_DIM = 64
BLOCK = 128
LANES = 128
SUBLANES = 8
DILATIONS = (1, 4, 16)
N_DIL = 3
A_HEADS = 4
A_WIDTH = A_HEADS * HEAD_DIM
A_QKV = 3 * N_DIL * A_WIDTH
S5_WIDTH = 256
S5_GROUPS = 16
S5_GROUP_CH = 16
S5_STATE = 64
S5_N = S5_GROUPS * S5_STATE
C_Q_HEADS = 8
C_KV_HEADS = 2
C_WIDTH = C_Q_HEADS * HEAD_DIM
ROPE_THETA = 150000.0
N_GATE = 3 * D_MODEL
IN_COLS = 6400
D_FF = 3584
N_EXPERTS = 8
RMS_EPS = 1e-6
ATT_SCALE = HEAD_DIM ** -0.5

OFF_GATE = 0
OFF_A = N_GATE
OFF_B = OFF_A + A_QKV
OFF_Q = OFF_B + S5_WIDTH
OFF_K = OFF_Q + C_WIDTH
OFF_V = OFF_K + C_KV_HEADS * HEAD_DIM

TM_PROJ = 2048
TN_PROJ = 1280
TM_MERGE = 1024
TM_FFN = 1024
TF_FFN = 1792
TS_S5 = 1024
TM_MOE = 2048
TF_MOE = 1792
FF_SUB_MOE = ((0, 768), (768, 1024))
CH_MOE = 128
KMIN_MOE = 4
KMAX_MOE = 5
SC_MOE = 512
VMEM_LIMIT = 56 * 1024 * 1024
VMEM_LIMIT_MOE = 60 * 1024 * 1024


def _cparams(sem):
    return pltpu.CompilerParams(dimension_semantics=sem, vmem_limit_bytes=VMEM_LIMIT)


def _dot(a, b):
    return jnp.dot(a, b, preferred_element_type=F32)


def _dot_nt(a, b):
    return lax.dot_general(a, b, (((1,), (1,)), ((), ())), preferred_element_type=F32)


def _rms(x, g):
    return x * lax.rsqrt(jnp.mean(x * x, axis=-1, keepdims=True) + RMS_EPS) * g


def _inproj_kernel(x_ref, g_ref, w_ref, o_ref, h_ref):
    @pl.when(pl.program_id(1) == 0)
    def _():
        h_ref[...] = _rms(x_ref[...], g_ref[...]).astype(BF16)

    o_ref[...] = _dot(h_ref[...], w_ref[...]).astype(o_ref.dtype)


def _inproj(x, g, w, l):
    t, d = x.shape
    n = w.shape[2]
    return pl.pallas_call(
        _inproj_kernel,
        grid=(t // TM_PROJ, n // TN_PROJ),
        in_specs=[
            pl.BlockSpec((TM_PROJ, d), lambda i, j: (i, 0)),
            pl.BlockSpec((1, d), lambda i, j: (0, 0)),
            pl.BlockSpec((None, d, TN_PROJ), lambda i, j: (l, 0, j)),
        ],
        out_specs=pl.BlockSpec((TM_PROJ, TN_PROJ), lambda i, j: (i, j)),
        out_shape=jax.ShapeDtypeStruct((t, n), BF16),
        scratch_shapes=[pltpu.VMEM((TM_PROJ, d), BF16)],
        compiler_params=_cparams(("parallel", "arbitrary")),
        name="inproj",
    )(x, g, w)


def _band_mask(rows, width):
    qi = lax.broadcasted_iota(jnp.int32, (rows, width), 0) % BLOCK
    ki = lax.broadcasted_iota(jnp.int32, (rows, width), 1)
    return (ki >= qi) & (ki <= qi + BLOCK)


def _causal_mask(rows, width):
    qi = lax.broadcasted_iota(jnp.int32, (rows, width), 0) % BLOCK
    ki = lax.broadcasted_iota(jnp.int32, (rows, width), 1)
    return ki <= qi


def _softmax_pv(s, mask, v, sink=None):
    s = jnp.where(mask, s, -jnp.inf)
    m = jnp.max(s, axis=-1, keepdims=True)
    if sink is not None:
        m = jnp.maximum(m, sink)
    p = jnp.exp(s - m)
    den = jnp.sum(p, axis=-1, keepdims=True)
    if sink is not None:
        den = den + jnp.exp(sink - m)
    o = _dot(p.astype(BF16), v)
    return o, m, den


def _dilated_kernel(q0_ref, q1_ref, q2_ref, k0_ref, k1_ref, k2_ref, v0_ref, v1_ref, v2_ref,
                    o_ref, qf_ref, kf_ref, vf_ref, og_ref, lg_ref):
    seq = o_ref.shape[0]
    lane = lax.broadcasted_iota(jnp.int32, (1, LANES), 1)
    lo = lane < HEAD_DIM
    band = _band_mask(2 * BLOCK, 2 * BLOCK)
    causal = _causal_mask(2 * BLOCK, BLOCK)

    def attend(q, k, v, mask):
        zero = jnp.zeros_like(q)
        qs = jnp.concatenate([jnp.where(lo, q, zero), jnp.where(lo, zero, q)], axis=0)
        s = _dot_nt(qs, k)
        o, m, den = _softmax_pv(s, mask, v)
        o = o * (1.0 / den)
        lse = m + jnp.log(den)
        o_pair = jnp.where(lo, o[:BLOCK], o[BLOCK:])
        l_pair = jnp.where(lo, lse[:BLOCK], lse[BLOCK:])
        return o_pair, l_pair

    for idx, (qr, kr, vr) in enumerate(((q1_ref, k1_ref, v1_ref), (q2_ref, k2_ref, v2_ref))):
        qf_ref[idx] = qr[...].astype(F32) * ATT_SCALE
        kf_ref[idx] = kr[...].astype(F32)
        vf_ref[idx] = vr[...].astype(F32)

    o_p, l_p = attend(q0_ref[pl.ds(0, BLOCK), :] * ATT_SCALE, k0_ref[pl.ds(0, BLOCK), :],
                      v0_ref[pl.ds(0, BLOCK), :], causal)
    og_ref[0, pl.ds(0, BLOCK), :] = o_p
    lg_ref[0, pl.ds(0, BLOCK), :] = l_p

    for n in range(1, seq // BLOCK):
        r0, w0 = n * BLOCK, (n - 1) * BLOCK
        o_p, l_p = attend(q0_ref[pl.ds(r0, BLOCK), :] * ATT_SCALE, k0_ref[pl.ds(w0, 2 * BLOCK), :],
                          v0_ref[pl.ds(w0, 2 * BLOCK), :], band)
        og_ref[0, pl.ds(r0, BLOCK), :] = o_p
        lg_ref[0, pl.ds(r0, BLOCK), :] = l_p

    for idx, d in ((0, DILATIONS[1]), (1, DILATIONS[2])):
        g = idx + 1
        nblk = seq // d // BLOCK
        for r in range(d):
            for n in range(nblk):
                q = qf_ref[idx, pl.ds(r + n * BLOCK * d, BLOCK, stride=d), :].astype(BF16)
                if n == 0:
                    k = kf_ref[idx, pl.ds(r, BLOCK, stride=d), :].astype(BF16)
                    v = vf_ref[idx, pl.ds(r, BLOCK, stride=d), :].astype(BF16)
                    o_p, l_p = attend(q, k, v, causal)
                else:
                    w0 = r + (n - 1) * BLOCK * d
                    k = kf_ref[idx, pl.ds(w0, 2 * BLOCK, stride=d), :].astype(BF16)
                    v = vf_ref[idx, pl.ds(w0, 2 * BLOCK, stride=d), :].astype(BF16)
                    o_p, l_p = attend(q, k, v, band)
                og_ref[g, pl.ds(r + n * BLOCK * d, BLOCK, stride=d), :] = o_p
                lg_ref[g, pl.ds(r + n * BLOCK * d, BLOCK, stride=d), :] = l_p

    l0, l1, l2 = lg_ref[0], lg_ref[1], lg_ref[2]
    mx = jnp.maximum(jnp.maximum(l0, l1), l2)
    e0, e1, e2 = jnp.exp(l0 - mx), jnp.exp(l1 - mx), jnp.exp(l2 - mx)
    tot = e0 + e1 + e2
    o_ref[...] = ((e0 * og_ref[0] + e1 * og_ref[1] + e2 * og_ref[2]) / tot).astype(o_ref.dtype)


def _dilated_mixer(proj3):
    b, s, _ = proj3.shape
    base = OFF_A // LANES

    def spec(which, g):
        col = base + which * (N_DIL * A_WIDTH // LANES) + g * (A_WIDTH // LANES)
        return pl.BlockSpec((None, s, LANES), lambda i, hp, col=col: (i, 0, col + hp))

    in_specs = [spec(w, g) for w in range(3) for g in range(N_DIL)]
    return pl.pallas_call(
        _dilated_kernel,
        grid=(b, A_WIDTH // LANES),
        in_specs=in_specs,
        out_specs=pl.BlockSpec((None, s, LANES), lambda i, hp: (i, 0, hp)),
        out_shape=jax.ShapeDtypeStruct((b, s, A_WIDTH), BF16),
        scratch_shapes=[
            pltpu.VMEM((2, s, LANES), F32),
            pltpu.VMEM((2, s, LANES), F32),
            pltpu.VMEM((2, s, LANES), F32),
            pltpu.VMEM((N_DIL, s, LANES), F32),
            pltpu.VMEM((N_DIL, s, LANES), F32),
        ],
        compiler_params=_cparams(("parallel", "parallel")),
        name="dilated_mixer",
    )(*([proj3] * 9))


def _swa_kernel(sink_ref, q_ref, k_ref, v_ref, cos_ref, sin_ref, o_ref, qs_ref, ks_ref, vs_ref):
    seq = o_ref.shape[0]
    g = pl.program_id(1)
    rep = C_Q_HEADS // C_KV_HEADS
    lane = lax.broadcasted_iota(jnp.int32, (1, LANES), 1)
    lo = lane < HEAD_DIM
    cos = cos_ref[...]
    sin = sin_ref[...]

    src = lax.broadcasted_iota(jnp.int32, (LANES, LANES), 0)
    dst = lax.broadcasted_iota(jnp.int32, (LANES, LANES), 1)
    half = HEAD_DIM // 2
    partner = jnp.where((dst % HEAD_DIM) < half, dst + half, dst - half)
    swap_mat = jnp.where(src == partner, 1.0, 0.0).astype(BF16)
    rep_mat = jnp.where(src == g * HEAD_DIM + dst % HEAD_DIM, 1.0, 0.0).astype(BF16)

    def rope(x):
        return x.astype(F32) * cos + _dot(x, swap_mat) * sin

    ks_ref[...] = _dot(rope(k_ref[...]).astype(BF16), rep_mat).astype(BF16)
    vs_ref[...] = _dot(v_ref[...], rep_mat).astype(BF16)
    for c in range(rep // 2):
        sl = slice(c * LANES, (c + 1) * LANES)
        qs_ref[:, sl] = (rope(q_ref[:, sl]) * ATT_SCALE).astype(BF16)

    rows = rep * BLOCK
    hrow = lax.broadcasted_iota(jnp.int32, (rows, 1), 0) // BLOCK
    sink = jnp.zeros((rows, 1), F32)
    for h in range(rep):
        sink = jnp.where(hrow == h, sink_ref[g * rep + h], sink)
    band = _band_mask(rows, 2 * BLOCK)
    causal = _causal_mask(rows, BLOCK)

    def block(r0, k, v, mask):
        parts = []
        for h in range(rep):
            q = qs_ref[pl.ds(r0, BLOCK), (h // 2) * LANES:(h // 2 + 1) * LANES]
            zero = jnp.zeros_like(q)
            parts.append(jnp.where(lo, q, zero) if h % 2 == 0 else jnp.where(lo, zero, q))
        s = _dot_nt(jnp.concatenate(parts, axis=0), k)
        o, _, den = _softmax_pv(s, mask, v, sink)
        o = o * (1.0 / den)
        for c in range(rep // 2):
            pair = jnp.where(lo, o[2 * c * BLOCK:(2 * c + 1) * BLOCK], o[(2 * c + 1) * BLOCK:(2 * c + 2) * BLOCK])
            o_ref[pl.ds(r0, BLOCK), c * LANES:(c + 1) * LANES] = pair.astype(o_ref.dtype)

    block(0, ks_ref[pl.ds(0, BLOCK), :], vs_ref[pl.ds(0, BLOCK), :], causal)

    def body(n, carry):
        r0 = pl.multiple_of(n * BLOCK, BLOCK)
        w0 = pl.multiple_of((n - 1) * BLOCK, BLOCK)
        block(r0, ks_ref[pl.ds(w0, 2 * BLOCK), :], vs_ref[pl.ds(w0, 2 * BLOCK), :], band)
        return carry

    lax.fori_loop(1, seq // BLOCK, body, 0, unroll=15)


def _rope_tables(seq):
    inv = ROPE_THETA ** (-jnp.arange(0, HEAD_DIM, 2, dtype=F32) / HEAD_DIM)
    ang = jnp.arange(seq, dtype=F32)[:, None] * inv[None, :]
    reps = LANES // (HEAD_DIM // 2)
    cos = jnp.tile(jnp.cos(ang), (1, reps))
    sign = jnp.where((jnp.arange(LANES) % HEAD_DIM) < HEAD_DIM // 2, -1.0, 1.0).astype(F32)
    sin = jnp.tile(jnp.sin(ang), (1, reps)) * sign[None, :]
    return cos, sin


def _swa_mixer(proj3, sinks, cos, sin):
    b, s, _ = proj3.shape
    qw = C_WIDTH // C_KV_HEADS
    return pl.pallas_call(
        _swa_kernel,
        grid=(b, C_KV_HEADS),
        in_specs=[
            pl.BlockSpec(memory_space=pltpu.SMEM),
            pl.BlockSpec((None, s, qw), lambda i, g: (i, 0, OFF_Q // qw + g)),
            pl.BlockSpec((None, s, LANES), lambda i, g: (i, 0, OFF_K // LANES)),
            pl.BlockSpec((None, s, LANES), lambda i, g: (i, 0, OFF_V // LANES)),
            pl.BlockSpec((s, LANES), lambda i, g: (0, 0)),
            pl.BlockSpec((s, LANES), lambda i, g: (0, 0)),
        ],
        out_specs=pl.BlockSpec((None, s, qw), lambda i, g: (i, 0, g)),
        out_shape=jax.ShapeDtypeStruct((b, s, C_WIDTH), BF16),
        scratch_shapes=[
            pltpu.VMEM((s, qw), BF16),
            pltpu.VMEM((s, LANES), BF16),
            pltpu.VMEM((s, LANES), BF16),
        ],
        compiler_params=_cparams(("parallel", "parallel")),
        name="swa_mixer",
    )(sinks, proj3, proj3, proj3, cos, sin)


def _cmul(ar, ai, br, bi):
    return ar * br - ai * bi, ar * bi + ai * br


def _s5_kernel(u_ref, lr_ref, li_ref, ldt_ref, bre_ref, bim_ref, cre_ref, cim_ref, d_ref, wg_ref, bg_ref,
               o_ref, hr_ref, hi_ref, pw_ref, car_ref):
    ts = u_ref.shape[0]

    @pl.when(pl.program_id(1) == 0)
    def _():
        lr, li = lr_ref[...], li_ref[...]
        dt = jnp.exp(ldt_ref[...])
        mag = jnp.exp(lr * dt)
        a_re, a_im = mag * jnp.cos(li * dt), mag * jnp.sin(li * dt)
        nr, ni = a_re - 1.0, a_im
        den = lr * lr + li * li
        z_re = (nr * lr + ni * li) / den
        z_im = (ni * lr - nr * li) / den
        row = lax.broadcasted_iota(jnp.int32, (SUBLANES, 1), 0)
        pows = [(a_re, a_im)]
        for _ in range(SUBLANES - 1):
            pows.append(_cmul(pows[-1][0], pows[-1][1], a_re, a_im))
        for j, sft in enumerate((1, 2, 4)):
            pr, pi = pows[sft - 1]
            pw_ref[2 * j] = jnp.where(row >= sft, pr, 0.0)
            pw_ref[2 * j + 1] = jnp.where(row >= sft, pi, 0.0)
        cr = jnp.zeros((SUBLANES, S5_N), F32)
        ci = jnp.zeros((SUBLANES, S5_N), F32)
        for i in range(SUBLANES):
            cr = jnp.where(row == i, pows[i][0], cr)
            ci = jnp.where(row == i, pows[i][1], ci)
        pw_ref[6] = cr
        pw_ref[7] = ci
        pw_ref[8] = jnp.broadcast_to(z_re, (SUBLANES, S5_N))
        pw_ref[9] = jnp.broadcast_to(z_im, (SUBLANES, S5_N))
        car_ref[...] = jnp.zeros_like(car_ref)

    u = u_ref[...]
    bu_re = _dot(u, bre_ref[...])
    bu_im = _dot(u, bim_ref[...])
    z_re = pw_ref[8, 0:1, :]
    z_im = pw_ref[9, 0:1, :]
    hr_ref[...] = z_re * bu_re - z_im * bu_im
    hi_ref[...] = z_re * bu_im + z_im * bu_re

    mults = [(pw_ref[2 * j], pw_ref[2 * j + 1]) for j in range(3)]
    acr, aci = pw_ref[6], pw_ref[7]

    def chunk(k, carry):
        cr, ci = carry
        r0 = pl.multiple_of(k * SUBLANES, SUBLANES)
        xr = hr_ref[pl.ds(r0, SUBLANES), :]
        xi = hi_ref[pl.ds(r0, SUBLANES), :]
        for (mr, mi), sft in zip(mults, (1, 2, 4)):
            sr = pltpu.roll(xr, sft, 0)
            si = pltpu.roll(xi, sft, 0)
            xr, xi = xr + (mr * sr - mi * si), xi + (mr * si + mi * sr)
        xr = xr + (acr * cr - aci * ci)
        xi = xi + (acr * ci + aci * cr)
        hr_ref[pl.ds(r0, SUBLANES), :] = xr
        hi_ref[pl.ds(r0, SUBLANES), :] = xi
        return xr[SUBLANES - 1:SUBLANES, :], xi[SUBLANES - 1:SUBLANES, :]

    cr, ci = lax.fori_loop(0, ts // SUBLANES, chunk, (car_ref[0, 0:1, :], car_ref[1, 0:1, :]), unroll=2)
    car_ref[0] = jnp.broadcast_to(cr, (SUBLANES, S5_N))
    car_ref[1] = jnp.broadcast_to(ci, (SUBLANES, S5_N))

    y = _dot(hr_ref[...].astype(BF16), cre_ref[...]) - _dot(hi_ref[...].astype(BF16), cim_ref[...])
    y = y + d_ref[...] * u.astype(F32)
    gl = jax.nn.gelu(y)
    out = gl * jax.nn.sigmoid(_dot(gl.astype(BF16), wg_ref[...]) + bg_ref[...])
    o_ref[...] = out.astype(o_ref.dtype)


def _s5_mixer(proj3, lr, li, ldt, bre, bim, cre, cim, dsk, wglu, bglu):
    b, s, _ = proj3.shape
    full = lambda shape: pl.BlockSpec(shape, lambda i, t: (0,) * len(shape))
    return pl.pallas_call(
        _s5_kernel,
        grid=(b, s // TS_S5),
        in_specs=[
            pl.BlockSpec((None, TS_S5, S5_WIDTH), lambda i, t: (i, t, OFF_B // S5_WIDTH)),
            full((1, S5_N)), full((1, S5_N)), full((1, S5_N)),
            full((S5_WIDTH, S5_N)), full((S5_WIDTH, S5_N)),
            full((S5_N, S5_WIDTH)), full((S5_N, S5_WIDTH)),
            full((1, S5_WIDTH)), full((S5_WIDTH, S5_WIDTH)), full((1, S5_WIDTH)),
        ],
        out_specs=pl.BlockSpec((None, TS_S5, S5_WIDTH), lambda i, t: (i, t, 0)),
        out_shape=jax.ShapeDtypeStruct((b, s, S5_WIDTH), BF16),
        scratch_shapes=[
            pltpu.VMEM((TS_S5, S5_N), F32),
            pltpu.VMEM((TS_S5, S5_N), F32),
            pltpu.VMEM((10, SUBLANES, S5_N), F32),
            pltpu.VMEM((2, SUBLANES, S5_N), F32),
        ],
        compiler_params=_cparams(("parallel", "arbitrary")),
        name="s5_mixer",
    )(proj3, lr, li, ldt, bre, bim, cre, cim, dsk, wglu, bglu)


def _merge_kernel(x_ref, oa_ref, ob_ref, oc_ref, ga_ref, gb_ref, gc_ref, wa_ref, wb_ref, wc_ref, wo_ref,
                  gn_ref, *rest, with_router):
    if with_router:
        rt_ref, xo_ref, h_ref, lg_ref = rest
    else:
        xo_ref, h_ref = rest

    def sig(r):
        return 0.5 * jnp.tanh(0.5 * r[...].astype(F32)) + 0.5

    merged = (sig(ga_ref) * _dot(oa_ref[...], wa_ref[...])
              + sig(gb_ref) * _dot(ob_ref[...], wb_ref[...])
              + sig(gc_ref) * _dot(oc_ref[...], wc_ref[...]))
    xn = x_ref[...] + _dot(merged.astype(BF16), wo_ref[...])
    xo_ref[...] = xn
    h = _rms(xn, gn_ref[...])
    h_hi = h.astype(BF16)
    h_ref[...] = h_hi
    if with_router:
        h_lo = (h - h_hi.astype(F32)).astype(BF16)
        rt = rt_ref[...]
        r_hi = rt.astype(BF16)
        r_lo = (rt - r_hi.astype(F32)).astype(BF16)
        lg_ref[...] = _dot_nt(r_hi, h_hi) + (_dot_nt(r_hi, h_lo) + _dot_nt(r_lo, h_hi))


def _merge(x, oa, ob, oc, proj, wa, wb, wc, wo, gn, l, router_t=None):
    t, d = x.shape
    tm = TM_MERGE
    with_router = router_t is not None
    row = lambda w: pl.BlockSpec((tm, w), lambda i: (i, 0))
    full = lambda a: pl.BlockSpec(a.shape, lambda i: (0, 0))
    layer = lambda a: pl.BlockSpec((None,) + a.shape[1:], lambda i: (l, 0, 0))
    in_specs = [row(d), row(A_WIDTH), row(S5_WIDTH), row(C_WIDTH)]
    in_specs += [pl.BlockSpec((tm, d), lambda i, c=c: (i, c)) for c in range(3)]
    in_specs += [layer(wa), layer(wb), layer(wc), layer(wo), full(gn)]
    args = [x, oa, ob, oc, proj, proj, proj, wa, wb, wc, wo, gn]
    out_specs = [row(d), row(d)]
    out_shape = [jax.ShapeDtypeStruct((t, d), F32), jax.ShapeDtypeStruct((t, d), BF16)]
    if with_router:
        in_specs.append(full(router_t))
        args.append(router_t)
        out_specs.append(pl.BlockSpec((N_EXPERTS, tm), lambda i: (0, i)))
        out_shape.append(jax.ShapeDtypeStruct((N_EXPERTS, t), F32))
    return pl.pallas_call(
        functools.partial(_merge_kernel, with_router=with_router),
        grid=(t // tm,),
        in_specs=in_specs,
        out_specs=out_specs,
        out_shape=out_shape,
        compiler_params=_cparams(("parallel",)),
        name="merge_router" if with_router else "merge",
    )(*args)


def _ffn_kernel(x_ref, h_ref, wg_ref, wu_ref, wd_ref, o_ref):
    @pl.when(pl.program_id(1) == 0)
    def _():
        o_ref[...] = x_ref[...]

    h = h_ref[...]
    for c0, cw in FF_SUB_MOE:
        act = jax.nn.silu(_dot(h, wg_ref[:, c0:c0 + cw])) * _dot(h, wu_ref[:, c0:c0 + cw])
        o_ref[...] += _dot(act.astype(BF16), wd_ref[c0:c0 + cw, :])


def _ffn(x, h, wg, wu, wd, li):
    t, d = x.shape
    ff = wg.shape[2]
    return pl.pallas_call(
        _ffn_kernel,
        grid=(t // TM_FFN, ff // TF_FFN),
        in_specs=[
            pl.BlockSpec((TM_FFN, d), lambda i, f: (i, 0)),
            pl.BlockSpec((TM_FFN, d), lambda i, f: (i, 0)),
            pl.BlockSpec((None, d, TF_FFN), lambda i, f: (li, 0, f)),
            pl.BlockSpec((None, d, TF_FFN), lambda i, f: (li, 0, f)),
            pl.BlockSpec((None, TF_FFN, d), lambda i, f: (li, f, 0)),
        ],
        out_specs=pl.BlockSpec((TM_FFN, d), lambda i, f: (i, 0)),
        out_shape=jax.ShapeDtypeStruct((t, d), F32),
        compiler_params=_cparams(("parallel", "arbitrary")),
        name="dense_ffn",
    )(x, h, wg, wu, wd)


def _route_kernel(lg_ref, gate_ref, rank_ref, cnt_ref):
    lg = lg_ref[...]
    tm = lg.shape[1]
    eidx = lax.broadcasted_iota(jnp.int32, lg.shape, 0)
    m1 = jnp.max(lg, axis=0, keepdims=True)
    i1 = jnp.min(jnp.where(lg == m1, eidx, N_EXPERTS), axis=0, keepdims=True)
    rest = jnp.where(eidx == i1, -jnp.inf, lg)
    m2 = jnp.max(rest, axis=0, keepdims=True)
    i2 = jnp.min(jnp.where(rest == m2, eidx, N_EXPERTS), axis=0, keepdims=True)
    e2 = jnp.exp(m2 - m1)
    tot = 1.0 + e2
    sel1 = eidx == i1
    sel2 = eidx == i2
    gate_ref[...] = jnp.where(sel1, 1.0 / tot, jnp.where(sel2, e2 / tot, 0.0))
    sel = (sel1 | sel2).astype(jnp.int32)
    lane = lax.broadcasted_iota(jnp.int32, lg.shape, 1)
    c = sel
    sft = 1
    while sft < tm:
        c = c + jnp.where(lane >= sft, pltpu.roll(c, sft, 1), 0)
        sft *= 2
    rank_ref[...] = jnp.where(sel > 0, c - 1, -1)
    cnt_ref[...] = jnp.broadcast_to(jnp.sum(sel, axis=1, keepdims=True), cnt_ref.shape)


def _route(logits_t, tm):
    e, t = logits_t.shape
    nt = t // tm
    return pl.pallas_call(
        _route_kernel,
        grid=(nt,),
        in_specs=[pl.BlockSpec((e, tm), lambda i: (0, i))],
        out_specs=[
            pl.BlockSpec((e, tm), lambda i: (0, i)),
            pl.BlockSpec((e, tm), lambda i: (0, i)),
            pl.BlockSpec((None, e, LANES), lambda i: (i, 0, 0)),
        ],
        out_shape=[
            jax.ShapeDtypeStruct((e, t), F32),
            jax.ShapeDtypeStruct((e, t), jnp.int32),
            jax.ShapeDtypeStruct((nt, e, LANES), jnp.int32),
        ],
        compiler_params=_cparams(("parallel",)),
        name="route_top2",
    )(logits_t)


def _moe_kernel(cnt_ref, x_ref, h_ref, rrow_ref, grow_ref, wg_ref, wu_ref, wd_ref, gfin_ref,
                o_ref, hc_ref, y_ref, *, final_norm):
    i, e, f = pl.program_id(0), pl.program_id(1), pl.program_id(2)
    ne, nf = pl.num_programs(1), pl.num_programs(2)
    tm = h_ref.shape[0]
    n_sel = cnt_ref[i * N_EXPERTS + e]
    n_chunks = (n_sel + (CH_MOE - 1)) // CH_MOE
    k_path = jnp.clip(n_chunks, KMIN_MOE, KMAX_MOE)
    n_over = jnp.maximum(n_chunks - KMAX_MOE, 0)

    def expert_part(hc):
        part = None
        for c0, cw in FF_SUB_MOE:
            act = jax.nn.silu(_dot(hc, wg_ref[:, c0:c0 + cw])) * _dot(hc, wu_ref[:, c0:c0 + cw])
            p = _dot(act.astype(BF16), wd_ref[c0:c0 + cw, :])
            part = p if part is None else part + p
        return part

    def picks(r0, rows):
        slot = lax.broadcasted_iota(jnp.int32, (rows, tm), 0) + r0
        return slot == rrow_ref[...]

    def compact(r0, rows):
        onehot = jnp.where(picks(r0, rows), 1.0, 0.0).astype(BF16)
        return _dot(onehot, h_ref[...]).astype(BF16)

    def scatter_add(r0, y):
        sel = picks(r0, y.shape[0])
        gate_c = jnp.sum(jnp.where(sel, grow_ref[...], 0.0), axis=1, keepdims=True)
        yb = (y * gate_c).astype(BF16)
        onehot = jnp.where(sel, 1.0, 0.0).astype(BF16)
        for q in range(tm // SC_MOE):
            cols = slice(q * SC_MOE, (q + 1) * SC_MOE)
            o_ref[pl.ds(q * SC_MOE, SC_MOE), :] += lax.dot_general(
                onehot[:, cols], yb, (((0,), (0,)), ((), ())), preferred_element_type=F32)

    @pl.when((e == 0) & (f == 0))
    def _():
        o_ref[...] = x_ref[...]

    def static_path(rows):
        sl = pl.ds(0, rows)

        @pl.when(f == 0)
        def _():
            hc_ref[sl, :] = compact(0, rows)
            y_ref[sl, :] = jnp.zeros((rows, D_MODEL), F32)

        y_ref[sl, :] += expert_part(hc_ref[sl, :])

        @pl.when(f == nf - 1)
        def _():
            scatter_add(0, y_ref[sl, :])

    for k in range(KMIN_MOE, KMAX_MOE + 1):
        pl.when(k_path == k)(functools.partial(static_path, k * CH_MOE))

    def overflow(c, carry):
        r0 = (KMAX_MOE + c) * CH_MOE
        scatter_add(r0, expert_part(compact(r0, CH_MOE)))
        return carry

    lax.fori_loop(0, n_over, overflow, 0)

    if final_norm:
        @pl.when((e == ne - 1) & (f == nf - 1))
        def _():
            o_ref[...] = _rms(o_ref[...], gfin_ref[...])


def _moe(x, h, counts, rank_row, gate_row, wg, wu, wd, li, tm, final_gain=None):
    final_norm = final_gain is not None
    if not final_norm:
        final_gain = jnp.ones((1, x.shape[1]), F32)
    t, d = x.shape
    _, ne, _, ff = wg.shape
    nt = t // tm
    once = pl.Buffered(1)
    grid_spec = pltpu.PrefetchScalarGridSpec(
        num_scalar_prefetch=1,
        grid=(nt, ne, ff // TF_MOE),
        in_specs=[
            pl.BlockSpec((tm, d), lambda i, e, f, c: (i, 0), pipeline_mode=once),
            pl.BlockSpec((tm, d), lambda i, e, f, c: (i, 0)),
            pl.BlockSpec((None, 1, tm), lambda i, e, f, c: (e, 0, i)),
            pl.BlockSpec((None, 1, tm), lambda i, e, f, c: (e, 0, i)),
            pl.BlockSpec((None, None, d, TF_MOE), lambda i, e, f, c: (li, e, 0, f)),
            pl.BlockSpec((None, None, d, TF_MOE), lambda i, e, f, c: (li, e, 0, f)),
            pl.BlockSpec((None, None, TF_MOE, d), lambda i, e, f, c: (li, e, f, 0)),
            pl.BlockSpec((1, d), lambda i, e, f, c: (0, 0)),
        ],
        out_specs=pl.BlockSpec((tm, d), lambda i, e, f, c: (i, 0), pipeline_mode=once),
        scratch_shapes=[pltpu.VMEM((KMAX_MOE * CH_MOE, d), BF16), pltpu.VMEM((KMAX_MOE * CH_MOE, d), F32)],
    )
    return pl.pallas_call(
        functools.partial(_moe_kernel, final_norm=final_norm),
        grid_spec=grid_spec,
        out_shape=jax.ShapeDtypeStruct((t, d), F32),
        compiler_params=pltpu.CompilerParams(dimension_semantics=("parallel", "arbitrary", "arbitrary"),
                                             vmem_limit_bytes=VMEM_LIMIT_MOE),
        name="moe_experts",
    )(counts, x, h, rank_row, gate_row, wg, wu, wd, final_gain)


def _final_norm_kernel(x_ref, g_ref, o_ref):
    o_ref[...] = _rms(x_ref[...], g_ref[...])


def _final_norm(x, g):
    t, d = x.shape
    tm = TM_PROJ
    return pl.pallas_call(
        _final_norm_kernel,
        grid=(t // tm,),
        in_specs=[pl.BlockSpec((tm, d), lambda i: (i, 0)), pl.BlockSpec((1, d), lambda i: (0, 0))],
        out_specs=pl.BlockSpec((tm, d), lambda i: (i, 0)),
        out_shape=jax.ShapeDtypeStruct((t, d), F32),
        compiler_params=_cparams(("parallel",)),
        name="final_norm",
    )(x, g)


def _block_diag(blocks):
    l, g, r, c = blocks.shape
    on_diag = jnp.eye(g, dtype=bool)[None, :, None, :, None]
    out = jnp.where(on_diag, blocks[:, :, :, None, :], jnp.zeros((), blocks.dtype))
    return out.reshape(l, g * r, g * c)


def kernel(x, norm_mix, w_in, s5_lambda_re, s5_lambda_im, s5_log_dt, s5_b_re, s5_b_im, s5_c_re, s5_c_im,
           s5_d, s5_w_glu, s5_b_glu, c_sinks, w_branch_a, w_branch_b, w_branch_c, w_out, norm_ffn,
           ffn_w_gate, ffn_w_up, ffn_w_down, moe_router, moe_w_gate, moe_w_up, moe_w_down, norm_final):
    b, s, d = x.shape
    depth = w_in.shape[0]
    t = b * s
    n_split = A_QKV + S5_WIDTH + C_WIDTH + 2 * C_KV_HEADS * HEAD_DIM

    w_in_p = jnp.concatenate([w_in[:, :, n_split:], w_in[:, :, :n_split]], axis=-1).astype(BF16)
    wa, wb, wc, wo = (w.astype(BF16) for w in (w_branch_a, w_branch_b, w_branch_c, w_out))
    fg, fu, fd = (w.astype(BF16) for w in (ffn_w_gate, ffn_w_up, ffn_w_down))
    mg, mu, md = (w.astype(BF16) for w in (moe_w_gate, moe_w_up, moe_w_down))
    router_t = jnp.swapaxes(moe_router, 1, 2)
    lam_re = s5_lambda_re.reshape(depth, 1, S5_N)
    lam_im = s5_lambda_im.reshape(depth, 1, S5_N)
    log_dt = jnp.repeat(s5_log_dt, S5_STATE, axis=-1).reshape(depth, 1, S5_N)
    bre = _block_diag(jnp.swapaxes(s5_b_re, 2, 3)).astype(BF16)
    bim = _block_diag(jnp.swapaxes(s5_b_im, 2, 3)).astype(BF16)
    cre = _block_diag(jnp.swapaxes(s5_c_re, 2, 3)).astype(BF16)
    cim = _block_diag(jnp.swapaxes(s5_c_im, 2, 3)).astype(BF16)
    wglu = s5_w_glu.astype(BF16)
    cos, sin = _rope_tables(s)

    xt = x.reshape(t, d)
    for l in range(depth):
        proj = _inproj(xt, norm_mix[l][None, :], w_in_p, l)
        proj3 = proj.reshape(b, s, IN_COLS)
        o_a = _dilated_mixer(proj3).reshape(t, A_WIDTH)
        o_b = _s5_mixer(proj3, lam_re[l], lam_im[l], log_dt[l], bre[l], bim[l], cre[l], cim[l],
                        s5_d[l][None, :], wglu[l], s5_b_glu[l][None, :]).reshape(t, S5_WIDTH)
        o_c = _swa_mixer(proj3, c_sinks[l], cos, sin).reshape(t, C_WIDTH)
        i = l // 2
        if l % 2 == 0:
            xt, h = _merge(xt, o_a, o_b, o_c, proj, wa, wb, wc, wo, norm_ffn[l][None, :], l)
            xt = _ffn(xt, h, fg, fu, fd, i)
        else:
            xt, h, logits_t = _merge(xt, o_a, o_b, o_c, proj, wa, wb, wc, wo,
                                     norm_ffn[l][None, :], l, router_t[i])
            gate, rank, cnt = _route(logits_t, TM_MOE)
            counts = cnt[:, :, 0].reshape(-1)
            last = l == depth - 1
            xt = _moe(xt, h, counts, rank[:, None, :], gate[:, None, :], mg, mu, md, i, TM_MOE,
                      final_gain=norm_final[None, :] if last else None)
    if depth % 2 == 1:
        xt = _final_norm(xt, norm_final[None, :])
    return xt.reshape(b, s, d)
```

```python
import functools
import math

import jax
import jax.numpy as jnp
import numpy as np
from jax import lax
from jax.experimental import pallas as pl
from jax.experimental.pallas import tpu as pltpu

F32 = jnp.float32
BF16 = jnp.bfloat16

D_MODEL = 1024
HEAD_DIM = 64
BLOCK = 128
LANES = 128
SUBLANES = 8
DILATIONS = (1, 4, 16)
N_DIL = 3
A_HEADS = 4
A_WIDTH = A_HEADS * HEAD_DIM
A_QKV = 3 * N_DIL * A_WIDTH
S5_WIDTH = 256
S5_GROUPS = 16
S5_GROUP_CH = 16
S5_STATE = 64
S5_N = S5_GROUPS * S5_STATE
C_Q_HEADS = 8
C_KV_HEADS = 2
C_WIDTH = C_Q_HEADS * HEAD_DIM
ROPE_THETA = 150000.0
N_GATE = 3 * D_MODEL
IN_COLS = 6400
D_FF = 3584
N_EXPERTS = 8
RMS_EPS = 1e-6
ATT_SCALE = HEAD_DIM ** -0.5
LOG2E = math.log2(math.e)

OFF_GATE = 0
OFF_A = N_GATE
OFF_B = OFF_A + A_QKV
OFF_Q = OFF_B + S5_WIDTH
OFF_K = OFF_Q + C_WIDTH
OFF_V = OFF_K + C_KV_HEADS * HEAD_DIM

TM_PROJ = 2048
TN_PROJ = 1280
TM_MERGE = 1024
TM_FFN = 1024
TF_FFN = 1792
TS_S5 = 1024
TM_MOE = 2048
TF_MOE = 1792
FF_SUB_MOE = ((0, 768), (768, 1024))
CH_MOE = 128
KMIN_MOE = 4
KMAX_MOE = 5
SC_MOE = 512
VMEM_LIMIT = 56 * 1024 * 1024
VMEM_LIMIT_MOE = 60 * 1024 * 1024


def _cparams(sem):
    return pltpu.CompilerParams(dimension_semantics=sem, vmem_limit_bytes=VMEM_LIMIT)


def _dot(a, b):
    return jnp.dot(a, b, preferred_element_type=F32)


def _dot_nt(a, b):
    return lax.dot_general(a, b, (((1,), (1,)), ((), ())), preferred_element_type=F32)


def _rms(x, g):
    return x * lax.rsqrt(jnp.mean(x * x, axis=-1, keepdims=True) + RMS_EPS) * g


def _inproj_kernel(x_ref, g_ref, w_ref, o_ref, h_ref):
    @pl.when(pl.program_id(1) == 0)
    def _():
        h_ref[...] = _rms(x_ref[...], g_ref[...]).astype(BF16)

    o_ref[...] = _dot(h_ref[...], w_ref[...]).astype(o_ref.dtype)


def _inproj(x, g, w, l):
    t, d = x.shape
    n = w.shape[2]
    return pl.pallas_call(
        _inproj_kernel,
        grid=(t // TM_PROJ, n // TN_PROJ),
        in_specs=[
            pl.BlockSpec((TM_PROJ, d), lambda i, j: (i, 0)),
            pl.BlockSpec((1, d), lambda i, j: (0, 0)),
            pl.BlockSpec((None, d, TN_PROJ), lambda i, j: (l, 0, j)),
        ],
        out_specs=pl.BlockSpec((TM_PROJ, TN_PROJ), lambda i, j: (i, j)),
        out_shape=jax.ShapeDtypeStruct((t, n), BF16),
        scratch_shapes=[pltpu.VMEM((TM_PROJ, d), BF16)],
        compiler_params=_cparams(("parallel", "arbitrary")),
        name="inproj",
    )(x, g, w)


def _band_mask(rows, width):
    qi = lax.broadcasted_iota(jnp.int32, (rows, width), 0) % BLOCK
    ki = lax.broadcasted_iota(jnp.int32, (rows, width), 1)
    return (ki >= qi) & (ki <= qi + BLOCK)


def _causal_mask(rows, width):
    qi = lax.broadcasted_iota(jnp.int32, (rows, width), 0) % BLOCK
    ki = lax.broadcasted_iota(jnp.int32, (rows, width), 1)
    return ki <= qi


def _softmax_pv(s, mask, v, sink=None, base2=False):
    ex = jnp.exp2 if base2 else jnp.exp
    s = jnp.where(mask, s, -jnp.inf)
    m = jnp.max(s, axis=-1, keepdims=True)
    if sink is not None:
        m = jnp.maximum(m, sink)
    p = ex(s - m)
    den = jnp.sum(p, axis=-1, keepdims=True)
    if sink is not None:
        den = den + ex(sink - m)
    o = _dot(p.astype(BF16), v)
    return o, m, den


def _dilated_kernel(q0_ref, q1_ref, q2_ref, k0_ref, k1_ref, k2_ref, v0_ref, v1_ref, v2_ref,
                    o_ref, qf_ref, kf_ref, vf_ref, og_ref, lg_ref):
    seq = o_ref.shape[0]
    lane = lax.broadcasted_iota(jnp.int32, (1, LANES), 1)
    lo = lane < HEAD_DIM
    band = _band_mask(2 * BLOCK, 2 * BLOCK)
    causal = _causal_mask(2 * BLOCK, BLOCK)

    def attend(q, k, v, mask):
        zero = jnp.zeros_like(q)
        qs = jnp.concatenate([jnp.where(lo, q, zero), jnp.where(lo, zero, q)], axis=0)
        s = _dot_nt(qs, k)
        o, m, den = _softmax_pv(s, mask, v)
        o = o * (1.0 / den)
        lse = m + jnp.log(den)
        o_pair = jnp.where(lo, o[:BLOCK], o[BLOCK:])
        l_pair = jnp.where(lo, lse[:BLOCK], lse[BLOCK:])
        return o_pair, l_pair

    for idx, (qr, kr, vr) in enumerate(((q1_ref, k1_ref, v1_ref), (q2_ref, k2_ref, v2_ref))):
        qf_ref[idx] = qr[...].astype(F32) * ATT_SCALE
        kf_ref[idx] = kr[...].astype(F32)
        vf_ref[idx] = vr[...].astype(F32)

    o_p, l_p = attend(q0_ref[pl.ds(0, BLOCK), :] * ATT_SCALE, k0_ref[pl.ds(0, BLOCK), :],
                      v0_ref[pl.ds(0, BLOCK), :], causal)
    og_ref[0, pl.ds(0, BLOCK), :] = o_p
    lg_ref[0, pl.ds(0, BLOCK), :] = l_p

    for n in range(1, seq // BLOCK):
        r0, w0 = n * BLOCK, (n - 1) * BLOCK
        o_p, l_p = attend(q0_ref[pl.ds(r0, BLOCK), :] * ATT_SCALE, k0_ref[pl.ds(w0, 2 * BLOCK), :],
                          v0_ref[pl.ds(w0, 2 * BLOCK), :], band)
        og_ref[0, pl.ds(r0, BLOCK), :] = o_p
        lg_ref[0, pl.ds(r0, BLOCK), :] = l_p

    for idx, d in ((0, DILATIONS[1]), (1, DILATIONS[2])):
        g = idx + 1
        nblk = seq // d // BLOCK
        for r in range(d):
            for n in range(nblk):
                q = qf_ref[idx, pl.ds(r + n * BLOCK * d, BLOCK, stride=d), :].astype(BF16)
                if n == 0:
                    k = kf_ref[idx, pl.ds(r, BLOCK, stride=d), :].astype(BF16)
                    v = vf_ref[idx, pl.ds(r, BLOCK, stride=d), :].astype(BF16)
                    o_p, l_p = attend(q, k, v, causal)
                else:
                    w0 = r + (n - 1) * BLOCK * d
                    k = kf_ref[idx, pl.ds(w0, 2 * BLOCK, stride=d), :].astype(BF16)
                    v = vf_ref[idx, pl.ds(w0, 2 * BLOCK, stride=d), :].astype(BF16)
                    o_p, l_p = attend(q, k, v, band)
                og_ref[g, pl.ds(r + n * BLOCK * d, BLOCK, stride=d), :] = o_p
                lg_ref[g, pl.ds(r + n * BLOCK * d, BLOCK, stride=d), :] = l_p

    l0, l1, l2 = lg_ref[0], lg_ref[1], lg_ref[2]
    mx = jnp.maximum(jnp.maximum(l0, l1), l2)
    e0, e1, e2 = jnp.exp(l0 - mx), jnp.exp(l1 - mx), jnp.exp(l2 - mx)
    tot = e0 + e1 + e2
    o_ref[...] = ((e0 * og_ref[0] + e1 * og_ref[1] + e2 * og_ref[2]) / tot).astype(o_ref.dtype)


def _dilated_mixer(proj3):
    b, s, _ = proj3.shape
    base = OFF_A // LANES

    def spec(which, g):
        col = base + which * (N_DIL * A_WIDTH // LANES) + g * (A_WIDTH // LANES)
        return pl.BlockSpec((None, s, LANES), lambda i, hp, col=col: (i, 0, col + hp))

    in_specs = [spec(w, g) for w in range(3) for g in range(N_DIL)]
    return pl.pallas_call(
        _dilated_kernel,
        grid=(b, A_WIDTH // LANES),
        in_specs=in_specs,
        out_specs=pl.BlockSpec((None, s, LANES), lambda i, hp: (i, 0, hp)),
        out_shape=jax.ShapeDtypeStruct((b, s, A_WIDTH), BF16),
        scratch_shapes=[
            pltpu.VMEM((2, s, LANES), F32),
            pltpu.VMEM((2, s, LANES), F32),
            pltpu.VMEM((2, s, LANES), F32),
            pltpu.VMEM((N_DIL, s, LANES), F32),
            pltpu.VMEM((N_DIL, s, LANES), F32),
        ],
        compiler_params=_cparams(("parallel", "parallel")),
        name="dilated_mixer",
    )(*([proj3] * 9))


def _swa_kernel(sink_ref, q_ref, k_ref, v_ref, cos_ref, sin_ref, o_ref, qs_ref, ks_ref, vs_ref):
    seq = o_ref.shape[0]
    g = pl.program_id(1)
    rep = C_Q_HEADS // C_KV_HEADS
    lane = lax.broadcasted_iota(jnp.int32, (1, LANES), 1)
    lo = lane < HEAD_DIM
    cos = cos_ref[...]
    sin = sin_ref[...]

    src = lax.broadcasted_iota(jnp.int32, (LANES, LANES), 0)
    dst = lax.broadcasted_iota(jnp.int32, (LANES, LANES), 1)
    half = HEAD_DIM // 2
    partner = jnp.where((dst % HEAD_DIM) < half, dst + half, dst - half)
    swap_mat = jnp.where(src == partner, 1.0, 0.0).astype(BF16)
    rep_mat = jnp.where(src == g * HEAD_DIM + dst % HEAD_DIM, 1.0, 0.0).astype(BF16)

    def rope(x):
        return x.astype(F32) * cos + _dot(x, swap_mat) * sin

    ks_ref[...] = _dot(rope(k_ref[...]).astype(BF16), rep_mat).astype(BF16)
    vs_ref[...] = _dot(v_ref[...], rep_mat).astype(BF16)
    for c in range(rep // 2):
        sl = slice(c * LANES, (c + 1) * LANES)
        qs_ref[:, sl] = (rope(q_ref[:, sl]) * (ATT_SCALE * LOG2E)).astype(BF16)

    rows = rep * BLOCK
    hrow = lax.broadcasted_iota(jnp.int32, (rows, 1), 0) // BLOCK
    sink = jnp.zeros((rows, 1), F32)
    for h in range(rep):
        sink = jnp.where(hrow == h, sink_ref[g * rep + h] * LOG2E, sink)
    band = _band_mask(rows, 2 * BLOCK)
    causal = _causal_mask(rows, BLOCK)

    def block(r0, k, v, mask):
        parts = []
        for h in range(rep):
            q = qs_ref[pl.ds(r0, BLOCK), (h // 2) * LANES:(h // 2 + 1) * LANES]
            zero = jnp.zeros_like(q)
            parts.append(jnp.where(lo, q, zero) if h % 2 == 0 else jnp.where(lo, zero, q))
        s = _dot_nt(jnp.concatenate(parts, axis=0), k)
        o, _, den = _softmax_pv(s, mask, v, sink, base2=True)
        o = o * (1.0 / den)
        for c in range(rep // 2):
            pair = jnp.where(lo, o[2 * c * BLOCK:(2 * c + 1) * BLOCK], o[(2 * c + 1) * BLOCK:(2 * c + 2) * BLOCK])
            o_ref[pl.ds(r0, BLOCK), c * LANES:(c + 1) * LANES] = pair.astype(o_ref.dtype)

    block(0, ks_ref[pl.ds(0, BLOCK), :], vs_ref[pl.ds(0, BLOCK), :], causal)

    def body(n, carry):
        r0 = pl.multiple_of(n * BLOCK, BLOCK)
        w0 = pl.multiple_of((n - 1) * BLOCK, BLOCK)
        block(r0, ks_ref[pl.ds(w0, 2 * BLOCK), :], vs_ref[pl.ds(w0, 2 * BLOCK), :], band)
        return carry

    lax.fori_loop(1, seq // BLOCK, body, 0, unroll=15)


def _rope_tables(seq):
    inv = ROPE_THETA ** (-jnp.arange(0, HEAD_DIM, 2, dtype=F32) / HEAD_DIM)
    ang = jnp.arange(seq, dtype=F32)[:, None] * inv[None, :]
    reps = LANES // (HEAD_DIM // 2)
    cos = jnp.tile(jnp.cos(ang), (1, reps))
    sign = jnp.where((jnp.arange(LANES) % HEAD_DIM) < HEAD_DIM // 2, -1.0, 1.0).astype(F32)
    sin = jnp.tile(jnp.sin(ang), (1, reps)) * sign[None, :]
    return cos, sin


def _swa_mixer(proj3, sinks, cos, sin):
    b, s, _ = proj3.shape
    qw = C_WIDTH // C_KV_HEADS
    return pl.pallas_call(
        _swa_kernel,
        grid=(b, C_KV_HEADS),
        in_specs=[
            pl.BlockSpec(memory_space=pltpu.SMEM),
            pl.BlockSpec((None, s, qw), lambda i, g: (i, 0, OFF_Q // qw + g)),
            pl.BlockSpec((None, s, LANES), lambda i, g: (i, 0, OFF_K // LANES)),
            pl.BlockSpec((None, s, LANES), lambda i, g: (i, 0, OFF_V // LANES)),
            pl.BlockSpec((s, LANES), lambda i, g: (0, 0)),
            pl.BlockSpec((s, LANES), lambda i, g: (0, 0)),
        ],
        out_specs=pl.BlockSpec((None, s, qw), lambda i, g: (i, 0, g)),
        out_shape=jax.ShapeDtypeStruct((b, s, C_WIDTH), BF16),
        scratch_shapes=[
            pltpu.VMEM((s, qw), BF16),
            pltpu.VMEM((s, LANES), BF16),
            pltpu.VMEM((s, LANES), BF16),
        ],
        compiler_params=_cparams(("parallel", "parallel")),
        name="swa_mixer",
    )(sinks, proj3, proj3, proj3, cos, sin)


def _cmul(ar, ai, br, bi):
    return ar * br - ai * bi, ar * bi + ai * br


def _s5_kernel(u_ref, lr_ref, li_ref, ldt_ref, bre_ref, bim_ref, cre_ref, cim_ref, d_ref, wg_ref, bg_ref,
               o_ref, hr_ref, hi_ref, pw_ref, car_ref):
    ts = u_ref.shape[0]

    @pl.when(pl.program_id(1) == 0)
    def _():
        lr, li = lr_ref[...], li_ref[...]
        dt = jnp.exp(ldt_ref[...])
        mag = jnp.exp(lr * dt)
        a_re, a_im = mag * jnp.cos(li * dt), mag * jnp.sin(li * dt)
        nr, ni = a_re - 1.0, a_im
        den = lr * lr + li * li
        z_re = (nr * lr + ni * li) / den
        z_im = (ni * lr - nr * li) / den
        row = lax.broadcasted_iota(jnp.int32, (SUBLANES, 1), 0)
        pows = [(a_re, a_im)]
        for _ in range(SUBLANES - 1):
            pows.append(_cmul(pows[-1][0], pows[-1][1], a_re, a_im))
        for j, sft in enumerate((1, 2, 4)):
            pr, pi = pows[sft - 1]
            pw_ref[2 * j] = jnp.where(row >= sft, pr, 0.0)
            pw_ref[2 * j + 1] = jnp.where(row >= sft, pi, 0.0)
        cr = jnp.zeros((SUBLANES, S5_N), F32)
        ci = jnp.zeros((SUBLANES, S5_N), F32)
        for i in range(SUBLANES):
            cr = jnp.where(row == i, pows[i][0], cr)
            ci = jnp.where(row == i, pows[i][1], ci)
        pw_ref[6] = cr
        pw_ref[7] = ci
        pw_ref[8] = jnp.broadcast_to(z_re, (SUBLANES, S5_N))
        pw_ref[9] = jnp.broadcast_to(z_im, (SUBLANES, S5_N))
        car_ref[...] = jnp.zeros_like(car_ref)

    u = u_ref[...]
    bu_re = _dot(u, bre_ref[...])
    bu_im = _dot(u, bim_ref[...])
    z_re = pw_ref[8, 0:1, :]
    z_im = pw_ref[9, 0:1, :]
    hr_ref[...] = z_re * bu_re - z_im * bu_im
    hi_ref[...] = z_re * bu_im + z_im * bu_re

    mults = [(pw_ref[2 * j], pw_ref[2 * j + 1]) for j in range(3)]
    acr, aci = pw_ref[6], pw_ref[7]

    def chunk(k, carry):
        cr, ci = carry
        r0 = pl.multiple_of(k * SUBLANES, SUBLANES)
        xr = hr_ref[pl.ds(r0, SUBLANES), :]
        xi = hi_ref[pl.ds(r0, SUBLANES), :]
        for (mr, mi), sft in zip(mults, (1, 2, 4)):
            sr = pltpu.roll(xr, sft, 0)
            si = pltpu.roll(xi, sft, 0)
            xr, xi = xr + (mr * sr - mi * si), xi + (mr * si + mi * sr)
        xr = xr + (acr * cr - aci * ci)
        xi = xi + (acr * ci + aci * cr)
        hr_ref[pl.ds(r0, SUBLANES), :] = xr
        hi_ref[pl.ds(r0, SUBLANES), :] = xi
        return xr[SUBLANES - 1:SUBLANES, :], xi[SUBLANES - 1:SUBLANES, :]

    cr, ci = lax.fori_loop(0, ts // SUBLANES, chunk, (car_ref[0, 0:1, :], car_ref[1, 0:1, :]), unroll=2)
    car_ref[0] = jnp.broadcast_to(cr, (SUBLANES, S5_N))
    car_ref[1] = jnp.broadcast_to(ci, (SUBLANES, S5_N))

    y = _dot(hr_ref[...].astype(BF16), cre_ref[...]) - _dot(hi_ref[...].astype(BF16), cim_ref[...])
    y = y + d_ref[...] * u.astype(F32)
    gl = jax.nn.gelu(y)
    out = gl * jax.nn.sigmoid(_dot(gl.astype(BF16), wg_ref[...]) + bg_ref[...])
    o_ref[...] = out.astype(o_ref.dtype)


def _s5_mixer(proj3, lr, li, ldt, bre, bim, cre, cim, dsk, wglu, bglu):
    b, s, _ = proj3.shape
    full = lambda shape: pl.BlockSpec(shape, lambda i, t: (0,) * len(shape))
    return pl.pallas_call(
        _s5_kernel,
        grid=(b, s // TS_S5),
        in_specs=[
            pl.BlockSpec((None, TS_S5, S5_WIDTH), lambda i, t: (i, t, OFF_B // S5_WIDTH)),
            full((1, S5_N)), full((1, S5_N)), full((1, S5_N)),
            full((S5_WIDTH, S5_N)), full((S5_WIDTH, S5_N)),
            full((S5_N, S5_WIDTH)), full((S5_N, S5_WIDTH)),
            full((1, S5_WIDTH)), full((S5_WIDTH, S5_WIDTH)), full((1, S5_WIDTH)),
        ],
        out_specs=pl.BlockSpec((None, TS_S5, S5_WIDTH), lambda i, t: (i, t, 0)),
        out_shape=jax.ShapeDtypeStruct((b, s, S5_WIDTH), BF16),
        scratch_shapes=[
            pltpu.VMEM((TS_S5, S5_N), F32),
            pltpu.VMEM((TS_S5, S5_N), F32),
            pltpu.VMEM((10, SUBLANES, S5_N), F32),
            pltpu.VMEM((2, SUBLANES, S5_N), F32),
        ],
        compiler_params=_cparams(("parallel", "arbitrary")),
        name="s5_mixer",
    )(proj3, lr, li, ldt, bre, bim, cre, cim, dsk, wglu, bglu)


def _merge_kernel(x_ref, oa_ref, ob_ref, oc_ref, ga_ref, gb_ref, gc_ref, wa_ref, wb_ref, wc_ref, wo_ref,
                  gn_ref, *rest, with_router):
    if with_router:
        rt_ref, xo_ref, h_ref, lg_ref = rest
    else:
        xo_ref, h_ref = rest

    def sig(r):
        return 0.5 * jnp.tanh(0.5 * r[...].astype(F32)) + 0.5

    merged = (sig(ga_ref) * _dot(oa_ref[...], wa_ref[...])
              + sig(gb_ref) * _dot(ob_ref[...], wb_ref[...])
              + sig(gc_ref) * _dot(oc_ref[...], wc_ref[...]))
    xn = x_ref[...] + _dot(merged.astype(BF16), wo_ref[...])
    xo_ref[...] = xn
    h = _rms(xn, gn_ref[...])
    h_hi = h.astype(BF16)
    h_ref[...] = h_hi
    if with_router:
        h_lo = (h - h_hi.astype(F32)).astype(BF16)
        rt = rt_ref[...]
        r_hi = rt.astype(BF16)
        r_lo = (rt - r_hi.astype(F32)).astype(BF16)
        lg_ref[...] = _dot_nt(r_hi, h_hi) + (_dot_nt(r_hi, h_lo) + _dot_nt(r_lo, h_hi))


def _merge(x, oa, ob, oc, proj, wa, wb, wc, wo, gn, l, router_t=None):
    t, d = x.shape
    tm = TM_MERGE
    with_router = router_t is not None
    row = lambda w: pl.BlockSpec((tm, w), lambda i: (i, 0))
    full = lambda a: pl.BlockSpec(a.shape, lambda i: (0, 0))
    layer = lambda a: pl.BlockSpec((None,) + a.shape[1:], lambda i: (l, 0, 0))
    in_specs = [row(d), row(A_WIDTH), row(S5_WIDTH), row(C_WIDTH)]
    in_specs += [pl.BlockSpec((tm, d), lambda i, c=c: (i, c)) for c in range(3)]
    in_specs += [layer(wa), layer(wb), layer(wc), layer(wo), full(gn)]
    args = [x, oa, ob, oc, proj, proj, proj, wa, wb, wc, wo, gn]
    out_specs = [row(d), row(d)]
    out_shape = [jax.ShapeDtypeStruct((t, d), F32), jax.ShapeDtypeStruct((t, d), BF16)]
    if with_router:
        in_specs.append(full(router_t))
        args.append(router_t)
        out_specs.append(pl.BlockSpec((N_EXPERTS, tm), lambda i: (0, i)))
        out_shape.append(jax.ShapeDtypeStruct((N_EXPERTS, t), F32))
    return pl.pallas_call(
        functools.partial(_merge_kernel, with_router=with_router),
        grid=(t // tm,),
        in_specs=in_specs,
        out_specs=out_specs,
        out_shape=out_shape,
        compiler_params=_cparams(("parallel",)),
        name="merge_router" if with_router else "merge",
    )(*args)


def _ffn_kernel(x_ref, h_ref, wg_ref, wu_ref, wd_ref, o_ref):
    @pl.when(pl.program_id(1) == 0)
    def _():
        o_ref[...] = x_ref[...]

    h = h_ref[...]
    for c0, cw in FF_SUB_MOE:
        act = jax.nn.silu(_dot(h, wg_ref[:, c0:c0 + cw])) * _dot(h, wu_ref[:, c0:c0 + cw])
        o_ref[...] += _dot(act.astype(BF16), wd_ref[c0:c0 + cw, :])


def _ffn(x, h, wg, wu, wd, li):
    t, d = x.shape
    ff = wg.shape[2]
    return pl.pallas_call(
        _ffn_kernel,
        grid=(t // TM_FFN, ff // TF_FFN),
        in_specs=[
            pl.BlockSpec((TM_FFN, d), lambda i, f: (i, 0)),
            pl.BlockSpec((TM_FFN, d), lambda i, f: (i, 0)),
            pl.BlockSpec((None, d, TF_FFN), lambda i, f: (li, 0, f)),
            pl.BlockSpec((None, d, TF_FFN), lambda i, f: (li, 0, f)),
            pl.BlockSpec((None, TF_FFN, d), lambda i, f: (li, f, 0)),
        ],
        out_specs=pl.BlockSpec((TM_FFN, d), lambda i, f: (i, 0)),
        out_shape=jax.ShapeDtypeStruct((t, d), F32),
        compiler_params=_cparams(("parallel", "arbitrary")),
        name="dense_ffn",
    )(x, h, wg, wu, wd)


def _route_kernel(lg_ref, gate_ref, rank_ref, cnt_ref):
    lg = lg_ref[...]
    tm = lg.shape[1]
    eidx = lax.broadcasted_iota(jnp.int32, lg.shape, 0)
    m1 = jnp.max(lg, axis=0, keepdims=True)
    i1 = jnp.min(jnp.where(lg == m1, eidx, N_EXPERTS), axis=0, keepdims=True)
    rest = jnp.where(eidx == i1, -jnp.inf, lg)
    m2 = jnp.max(rest, axis=0, keepdims=True)
    i2 = jnp.min(jnp.where(rest == m2, eidx, N_EXPERTS), axis=0, keepdims=True)
    e2 = jnp.exp(m2 - m1)
    tot = 1.0 + e2
    sel1 = eidx == i1
    sel2 = eidx == i2
    gate_ref[...] = jnp.where(sel1, 1.0 / tot, jnp.where(sel2, e2 / tot, 0.0))
    sel = (sel1 | sel2).astype(jnp.int32)
    lane = lax.broadcasted_iota(jnp.int32, lg.shape, 1)
    c = sel
    sft = 1
    while sft < tm:
        c = c + jnp.where(lane >= sft, pltpu.roll(c, sft, 1), 0)
        sft *= 2
    rank_ref[...] = jnp.where(sel > 0, c - 1, -1)
    cnt_ref[...] = jnp.broadcast_to(jnp.sum(sel, axis=1, keepdims=True), cnt_ref.shape)


def _route(logits_t, tm):
    e, t = logits_t.shape
    nt = t // tm
    return pl.pallas_call(
        _route_kernel,
        grid=(nt,),
        in_specs=[pl.BlockSpec((e, tm), lambda i: (0, i))],
        out_specs=[
            pl.BlockSpec((e, tm), lambda i: (0, i)),
            pl.BlockSpec((e, tm), lambda i: (0, i)),
            pl.BlockSpec((None, e, LANES), lambda i: (i, 0, 0)),
        ],
        out_shape=[
            jax.ShapeDtypeStruct((e, t), F32),
            jax.ShapeDtypeStruct((e, t), jnp.int32),
            jax.ShapeDtypeStruct((nt, e, LANES), jnp.int32),
        ],
        compiler_params=_cparams(("parallel",)),
        name="route_top2",
    )(logits_t)


def _moe_kernel(cnt_ref, x_ref, h_ref, rrow_ref, grow_ref, wg_ref, wu_ref, wd_ref, gfin_ref,
                o_ref, hc_ref, y_ref, *, final_norm):
    i, e, f = pl.program_id(0), pl.program_id(1), pl.program_id(2)
    ne, nf = pl.num_programs(1), pl.num_programs(2)
    tm = h_ref.shape[0]
    n_sel = cnt_ref[i * N_EXPERTS + e]
    n_chunks = (n_sel + (CH_MOE - 1)) // CH_MOE
    k_path = jnp.clip(n_chunks, KMIN_MOE, KMAX_MOE)
    n_over = jnp.maximum(n_chunks - KMAX_MOE, 0)

    def expert_part(hc):
        part = None
        for c0, cw in FF_SUB_MOE:
            act = jax.nn.silu(_dot(hc, wg_ref[:, c0:c0 + cw])) * _dot(hc, wu_ref[:, c0:c0 + cw])
            p = _dot(act.astype(BF16), wd_ref[c0:c0 + cw, :])
            part = p if part is None else part + p
        return part

    def picks(r0, rows):
        slot = lax.broadcasted_iota(jnp.int32, (rows, tm), 0) + r0
        return slot == rrow_ref[...]

    def compact(r0, rows):
        onehot = jnp.where(picks(r0, rows), 1.0, 0.0).astype(BF16)
        return _dot(onehot, h_ref[...]).astype(BF16)

    def scatter_add(r0, y):
        sel = picks(r0, y.shape[0])
        gate_c = jnp.sum(jnp.where(sel, grow_ref[...], 0.0), axis=1, keepdims=True)
        yb = (y * gate_c).astype(BF16)
        onehot = jnp.where(sel, 1.0, 0.0).astype(BF16)
        for q in range(tm // SC_MOE):
            cols = slice(q * SC_MOE, (q + 1) * SC_MOE)
            o_ref[pl.ds(q * SC_MOE, SC_MOE), :] += lax.dot_general(
                onehot[:, cols], yb, (((0,), (0,)), ((), ())), preferred_element_type=F32)

    @pl.when((e == 0) & (f == 0))
    def _():
        o_ref[...] = x_ref[...]

    def static_path(rows):
        sl = pl.ds(0, rows)

        @pl.when(f == 0)
        def _():
            hc_ref[sl, :] = compact(0, rows)
            y_ref[sl, :] = jnp.zeros((rows, D_MODEL), F32)

        y_ref[sl, :] += expert_part(hc_ref[sl, :])

        @pl.when(f == nf - 1)
        def _():
            scatter_add(0, y_ref[sl, :])

    for k in range(KMIN_MOE, KMAX_MOE + 1):
        pl.when(k_path == k)(functools.partial(static_path, k * CH_MOE))

    def overflow(c, carry):
        r0 = (KMAX_MOE + c) * CH_MOE
        scatter_add(r0, expert_part(compact(r0, CH_MOE)))
        return carry

    lax.fori_loop(0, n_over, overflow, 0)

    if final_norm:
        @pl.when((e == ne - 1) & (f == nf - 1))
        def _():
            o_ref[...] = _rms(o_ref[...], gfin_ref[...])


def _moe(x, h, counts, rank_row, gate_row, wg, wu, wd, li, tm, final_gain=None):
    final_norm = final_gain is not None
    if not final_norm:
        final_gain = jnp.ones((1, x.shape[1]), F32)
    t, d = x.shape
    _, ne, _, ff = wg.shape
    nt = t // tm
    once = pl.Buffered(1)
    grid_spec = pltpu.PrefetchScalarGridSpec(
        num_scalar_prefetch=1,
        grid=(nt, ne, ff // TF_MOE),
        in_specs=[
            pl.BlockSpec((tm, d), lambda i, e, f, c: (i, 0), pipeline_mode=once),
            pl.BlockSpec((tm, d), lambda i, e, f, c: (i, 0)),
            pl.BlockSpec((None, 1, tm), lambda i, e, f, c: (e, 0, i)),
            pl.BlockSpec((None, 1, tm), lambda i, e, f, c: (e, 0, i)),
            pl.BlockSpec((None, None, d, TF_MOE), lambda i, e, f, c: (li, e, 0, f)),
            pl.BlockSpec((None, None, d, TF_MOE), lambda i, e, f, c: (li, e, 0, f)),
            pl.BlockSpec((None, None, TF_MOE, d), lambda i, e, f, c: (li, e, f, 0)),
            pl.BlockSpec((1, d), lambda i, e, f, c: (0, 0)),
        ],
        out_specs=pl.BlockSpec((tm, d), lambda i, e, f, c: (i, 0), pipeline_mode=once),
        scratch_shapes=[pltpu.VMEM((KMAX_MOE * CH_MOE, d), BF16), pltpu.VMEM((KMAX_MOE * CH_MOE, d), F32)],
    )
    return pl.pallas_call(
        functools.partial(_moe_kernel, final_norm=final_norm),
        grid_spec=grid_spec,
        out_shape=jax.ShapeDtypeStruct((t, d), F32),
        compiler_params=pltpu.CompilerParams(dimension_semantics=("parallel", "arbitrary", "arbitrary"),
                                             vmem_limit_bytes=VMEM_LIMIT_MOE),
        name="moe_experts",
    )(counts, x, h, rank_row, gate_row, wg, wu, wd, final_gain)


def _final_norm_kernel(x_ref, g_ref, o_ref):
    o_ref[...] = _rms(x_ref[...], g_ref[...])


def _final_norm(x, g):
    t, d = x.shape
    tm = TM_PROJ
    return pl.pallas_call(
        _final_norm_kernel,
        grid=(t // tm,),
        in_specs=[pl.BlockSpec((tm, d), lambda i: (i, 0)), pl.BlockSpec((1, d), lambda i: (0, 0))],
        out_specs=pl.BlockSpec((tm, d), lambda i: (i, 0)),
        out_shape=jax.ShapeDtypeStruct((t, d), F32),
        compiler_params=_cparams(("parallel",)),
        name="final_norm",
    )(x, g)


def _block_diag(blocks):
    l, g, r, c = blocks.shape
    on_diag = jnp.eye(g, dtype=bool)[None, :, None, :, None]
    out = jnp.where(on_diag, blocks[:, :, :, None, :], jnp.zeros((), blocks.dtype))
    return out.reshape(l, g * r, g * c)


def kernel(x, norm_mix, w_in, s5_lambda_re, s5_lambda_im, s5_log_dt, s5_b_re, s5_b_im, s5_c_re, s5_c_im,
           s5_d, s5_w_glu, s5_b_glu, c_sinks, w_branch_a, w_branch_b, w_branch_c, w_out, norm_ffn,
           ffn_w_gate, ffn_w_up, ffn_w_down, moe_router, moe_w_gate, moe_w_up, moe_w_down, norm_final):
    b, s, d = x.shape
    depth = w_in.shape[0]
    t = b * s
    n_split = A_QKV + S5_WIDTH + C_WIDTH + 2 * C_KV_HEADS * HEAD_DIM

    w_in_p = jnp.concatenate([w_in[:, :, n_split:], w_in[:, :, :n_split]], axis=-1).astype(BF16)
    wa, wb, wc, wo = (w.astype(BF16) for w in (w_branch_a, w_branch_b, w_branch_c, w_out))
    fg, fu, fd = (w.astype(BF16) for w in (ffn_w_gate, ffn_w_up, ffn_w_down))
    mg, mu, md = (w.astype(BF16) for w in (moe_w_gate, moe_w_up, moe_w_down))
    router_t = jnp.swapaxes(moe_router, 1, 2)
    lam_re = s5_lambda_re.reshape(depth, 1, S5_N)
    lam_im = s5_lambda_im.reshape(depth, 1, S5_N)
    log_dt = jnp.repeat(s5_log_dt, S5_STATE, axis=-1).reshape(depth, 1, S5_N)
    bre = _block_diag(jnp.swapaxes(s5_b_re, 2, 3)).astype(BF16)
    bim = _block_diag(jnp.swapaxes(s5_b_im, 2, 3)).astype(BF16)
    cre = _block_diag(jnp.swapaxes(s5_c_re, 2, 3)).astype(BF16)
    cim = _block_diag(jnp.swapaxes(s5_c_im, 2, 3)).astype(BF16)
    wglu = s5_w_glu.astype(BF16)
    cos, sin = _rope_tables(s)

    xt = x.reshape(t, d)
    for l in range(depth):
        proj = _inproj(xt, norm_mix[l][None, :], w_in_p, l)
        proj3 = proj.reshape(b, s, IN_COLS)
        o_a = _dilated_mixer(proj3).reshape(t, A_WIDTH)
        o_b = _s5_mixer(proj3, lam_re[l], lam_im[l], log_dt[l], bre[l], bim[l], cre[l], cim[l],
                        s5_d[l][None, :], wglu[l], s5_b_glu[l][None, :]).reshape(t, S5_WIDTH)
        o_c = _swa_mixer(proj3, c_sinks[l], cos, sin).reshape(t, C_WIDTH)
        i = l // 2
        if l % 2 == 0:
            xt, h = _merge(xt, o_a, o_b, o_c, proj, wa, wb, wc, wo, norm_ffn[l][None, :], l)
            xt = _ffn(xt, h, fg, fu, fd, i)
        else:
            xt, h, logits_t = _merge(xt, o_a, o_b, o_c, proj, wa, wb, wc, wo,
                                     norm_ffn[l][None, :], l, router_t[i])
            gate, rank, cnt = _route(logits_t, TM_MOE)
            counts = cnt[:, :, 0].reshape(-1)
            last = l == depth - 1
            xt = _moe(xt, h, counts, rank[:, None, :], gate[:, None, :], mg, mu, md, i, TM_MOE,
                      final_gain=norm_final[None, :] if last else None)
    if depth % 2 == 1:
        xt = _final_norm(xt, norm_final[None, :])
    return xt.reshape(b, s, d)
```
